```python
import math
import jax, jax.numpy as jnp
from jax import lax
import numpy as np

D_MODEL = 1024
BATCH = 16
SEQ = 2048
DEPTH = 1
DEC_BATCH = 4
DEC_SEQ = 8192
PAST_LEN = 128

GRID_W = 64
Q_BLOCK = 128
HA = 4
HD_A = 64
HB = 8
KV_B = 2
HD_B = 64
ROPE_THETA = 10000.0
N_BUCKETS = 32
MAX_DISTANCE = 128
A_Q = HA * 2 * HD_A
A_K = HA * 2 * HD_A
A_V = HA * 2 * HD_A
B_Q = HB * HD_B
B_K = KV_B * HD_B
B_V = KV_B * HD_B
D_PROJ = A_Q + A_K + A_V + B_Q + B_K + B_V
D_MIX_OUT = HA * 2 * HD_A + HB * HD_B
N_EXPERTS = 256
TOP_K = 8
N_GROUPS = 8
TOPK_GROUPS = 4
D_EXPERT = 256
D_SHARED = 256
ROUTE_SCALE = 2.5
MOE_BLOCK = 128
EPS = 1e-6

kernel_name = 'hybrid_diffattn_gqa_axialrope_moe_encoder'


def rms_norm(x, g):
    xf = x.astype(jnp.float32)
    y = xf * lax.rsqrt(jnp.mean(xf * xf, axis=-1, keepdims=True) + EPS)
    return (y * g.astype(jnp.float32)).astype(x.dtype)


def t5_bucket(rel):
    nb = N_BUCKETS // 2
    max_exact = nb // 2
    ret = jnp.where(rel > 0, nb, 0)
    n = jnp.abs(rel)
    nf = jnp.maximum(n, 1).astype(jnp.float32)
    large = max_exact + (jnp.log(nf / max_exact) / math.log(MAX_DISTANCE / max_exact) * (nb - max_exact)).astype(jnp.int32)
    large = jnp.minimum(large, nb - 1)
    return ret + jnp.where(n < max_exact, n, large)


def to_blocks(t):
    b, s = t.shape[:2]
    return t.reshape(b, s // Q_BLOCK, Q_BLOCK, *t.shape[2:]).swapaxes(0, 1)


def from_blocks(t):
    t = t.swapaxes(0, 1)
    return t.reshape(t.shape[0], t.shape[1] * t.shape[2], *t.shape[3:])


def diff_attention(q1, q2, k1, k2, v, lam, rel_bias):
    s_len = q1.shape[1]
    nblk = s_len // Q_BLOCK
    scale = HD_A ** -0.5
    kpos = jnp.arange(s_len, dtype=jnp.int32)

    def block(args):
        q1b, q2b, q0 = args
        qpos = q0 + jnp.arange(Q_BLOCK, dtype=jnp.int32)
        bias = rel_bias[t5_bucket(kpos[None, :] - qpos[:, None])]
        bias = jnp.transpose(bias, (2, 0, 1)).astype(jnp.float32)
        s1 = jnp.einsum('bqhd,bkhd->bhqk', q1b, k1).astype(jnp.float32) * scale + bias
        s2 = jnp.einsum('bqhd,bkhd->bhqk', q2b, k2).astype(jnp.float32) * scale + bias
        p = jax.nn.softmax(s1, axis=-1) - lam * jax.nn.softmax(s2, axis=-1)
        return jnp.einsum('bhqk,bkhe->bqhe', p.astype(v.dtype), v)

    starts = jnp.arange(nblk, dtype=jnp.int32) * Q_BLOCK
    out = lax.map(block, (to_blocks(q1), to_blocks(q2), starts))
    return from_blocks(out)


def axial_rope_tables(s_len):
    rows = s_len // GRID_W
    row_id = jnp.repeat(jnp.arange(rows, dtype=jnp.float32), GRID_W)
    col_id = jnp.tile(jnp.arange(GRID_W, dtype=jnp.float32), rows)
    half = HD_B // 2
    inv = ROPE_THETA ** (-jnp.arange(0, half, 2, dtype=jnp.float32) / half)
    ang_r = row_id[:, None] * inv[None, :]
    ang_c = col_id[:, None] * inv[None, :]
    return jnp.cos(ang_r), jnp.sin(ang_r), jnp.cos(ang_c), jnp.sin(ang_c)


def rope_rotate(x, cos, sin):
    x1, x2 = jnp.split(x.astype(jnp.float32), 2, axis=-1)
    c = cos[None, :, None, :]
    s = sin[None, :, None, :]
    return jnp.concatenate([x1 * c - x2 * s, x1 * s + x2 * c], axis=-1)


def apply_axial_rope(x, tabs):
    cr, sr, cc, sc = tabs
    half = HD_B // 2
    y = jnp.concatenate([rope_rotate(x[..., :half], cr, sr), rope_rotate(x[..., half:], cc, sc)], axis=-1)
    return y.astype(x.dtype)


def gqa_attention(q, k, v):
    b, s_len = q.shape[:2]
    g = HB // KV_B
    scale = HD_B ** -0.5
    qg = q.reshape(b, s_len, KV_B, g, HD_B)

    def block(qb):
        s = jnp.einsum('bqngd,bknd->bngqk', qb, k).astype(jnp.float32) * scale
        p = jax.nn.softmax(s, axis=-1)
        return jnp.einsum('bngqk,bknd->bqngd', p.astype(v.dtype), v)

    out = from_blocks(lax.map(block, to_blocks(qg)))
    return out.reshape(b, s_len, HB * HD_B)


def token_mixers(h, layer, rel_bias, w_in, lambda_q1, lambda_k1, lambda_q2, lambda_k2,
                 g_subln, g_qnorm, g_knorm, w_out):
    b, s_len, _ = h.shape
    proj = h @ w_in[layer]
    o1 = A_Q
    o2 = o1 + A_K
    o3 = o2 + A_V
    o4 = o3 + B_Q
    o5 = o4 + B_K
    qa = proj[..., :o1].reshape(b, s_len, HA, 2, HD_A)
    ka = proj[..., o1:o2].reshape(b, s_len, HA, 2, HD_A)
    va = proj[..., o2:o3].reshape(b, s_len, HA, 2 * HD_A)
    qb = proj[..., o3:o4].reshape(b, s_len, HB, HD_B)
    kb = proj[..., o4:o5].reshape(b, s_len, KV_B, HD_B)
    vb = proj[..., o5:].reshape(b, s_len, KV_B, HD_B)

    lam_init = 0.8 - 0.6 * math.exp(-0.3 * layer)
    lam = (jnp.exp(jnp.sum(lambda_q1[layer].astype(jnp.float32) * lambda_k1[layer].astype(jnp.float32)))
           - jnp.exp(jnp.sum(lambda_q2[layer].astype(jnp.float32) * lambda_k2[layer].astype(jnp.float32)))
           + lam_init)
    oa = diff_attention(qa[..., 0, :], qa[..., 1, :], ka[..., 0, :], ka[..., 1, :], va, lam, rel_bias)
    oa = rms_norm(oa, g_subln[layer]) * (1.0 - lam_init)
    oa = oa.reshape(b, s_len, HA * 2 * HD_A)

    tabs = axial_rope_tables(s_len)
    qb = apply_axial_rope(rms_norm(qb, g_qnorm[layer]), tabs)
    kb = apply_axial_rope(rms_norm(kb, g_knorm[layer]), tabs)
    ob = gqa_attention(qb, kb, vb)

    return jnp.concatenate([oa, ob], axis=-1) @ w_out[layer]


def swiglu(x, wg, wu, wd):
    return (jax.nn.silu(x @ wg) * (x @ wu)) @ wd


def routed_experts(xf, idx, gate, w_eg, w_eu, w_ed):
    t_len, d = xf.shape
    a_len = t_len * TOP_K
    e_flat = idx.reshape(-1)
    tok_flat = jnp.repeat(jnp.arange(t_len, dtype=jnp.int32), TOP_K)
    g_flat = gate.reshape(-1)
    order = jnp.argsort(e_flat)
    e_sorted = e_flat[order]
    counts = jnp.bincount(e_flat, length=N_EXPERTS)
    padded = (counts + MOE_BLOCK - 1) // MOE_BLOCK * MOE_BLOCK
    pad_end = jnp.cumsum(padded)
    pad_start = pad_end - padded
    start = jnp.cumsum(counts) - counts
    dest = pad_start[e_sorted] + jnp.arange(a_len, dtype=jnp.int32) - start[e_sorted]
    n_blocks = a_len // MOE_BLOCK + N_EXPERTS
    n_slots = n_blocks * MOE_BLOCK
    slot_tok = jnp.full((n_slots,), t_len, jnp.int32).at[dest].set(tok_flat[order])
    slot_gate = jnp.zeros((n_slots,), jnp.float32).at[dest].set(g_flat[order])
    blk_expert = jnp.minimum(jnp.searchsorted(pad_end, jnp.arange(n_blocks) * MOE_BLOCK, side='right'),
                             N_EXPERTS - 1)
    x_pad = jnp.concatenate([xf, jnp.zeros((1, d), xf.dtype)], axis=0)

    def block(args):
        tb, gb, e = args
        xb = x_pad[tb]
        yb = swiglu(xb, w_eg[e], w_eu[e], w_ed[e])
        return yb * gb[:, None].astype(yb.dtype)

    y = lax.map(block, (slot_tok.reshape(n_blocks, MOE_BLOCK), slot_gate.reshape(n_blocks, MOE_BLOCK), blk_expert))
    out = jnp.zeros((t_len + 1, d), xf.dtype).at[slot_tok].add(y.reshape(n_slots, d))
    return out[:t_len]


def moe_ffn(h, layer, w_router, router_bias, w_exp_gate, w_exp_up, w_exp_down, w_sh_gate, w_sh_up, w_sh_down):
    b, s_len, d = h.shape
    xf = h.reshape(-1, d)
    t_len = xf.shape[0]
    scores = jax.nn.sigmoid((xf @ w_router[layer]).astype(jnp.float32))
    sel = scores + router_bias[layer].astype(jnp.float32)
    grp = sel.reshape(t_len, N_GROUPS, N_EXPERTS // N_GROUPS)
    grp_score = lax.top_k(grp, 2)[0].sum(-1)
    _, top_g = lax.top_k(grp_score, TOPK_GROUPS)
    gmask = jax.nn.one_hot(top_g, N_GROUPS, dtype=jnp.float32).sum(1)
    emask = jnp.repeat(gmask, N_EXPERTS // N_GROUPS, axis=1) > 0
    _, idx = lax.top_k(jnp.where(emask, sel, -jnp.inf), TOP_K)
    w = jnp.take_along_axis(scores, idx, axis=-1)
    w = w / jnp.sum(w, axis=-1, keepdims=True) * ROUTE_SCALE
    routed = routed_experts(xf, idx, w, w_exp_gate[layer], w_exp_up[layer], w_exp_down[layer])
    shared = swiglu(xf, w_sh_gate[layer], w_sh_up[layer], w_sh_down[layer])
    return (routed + shared).reshape(b, s_len, d)


def trunk(x, c, rel_bias, w_ada, b_ada, g_norm1, w_in, lambda_q1, lambda_k1, lambda_q2, lambda_k2,
          g_subln, g_qnorm, g_knorm, w_out, g_norm2, w_router, router_bias, w_exp_gate, w_exp_up,
          w_exp_down, w_sh_gate, w_sh_up, w_sh_down, g_final):
    for layer in range(DEPTH):
        mod = jax.nn.silu(c) @ w_ada[layer] + b_ada[layer]
        sh1, sc1, gt1, sh2, sc2, gt2 = jnp.split(mod[:, None, :], 6, axis=-1)
        h = rms_norm(x, g_norm1[layer]) * (1.0 + sc1) + sh1
        x = x + gt1 * token_mixers(h, layer, rel_bias, w_in, lambda_q1, lambda_k1, lambda_q2, lambda_k2,
                                   g_subln, g_qnorm, g_knorm, w_out)
        h = rms_norm(x, g_norm2[layer]) * (1.0 + sc2) + sh2
        x = x + gt2 * moe_ffn(h, layer, w_router, router_bias, w_exp_gate, w_exp_up, w_exp_down,
                              w_sh_gate, w_sh_up, w_sh_down)
    return rms_norm(x, g_final)


def setup_inputs(seed: int = 0) -> dict:
    key = jax.random.key(seed)
    ks = jax.random.split(key, 32)
    f32 = jnp.float32
    D = D_MODEL

    def nrm(k, shape, scale):
        return jax.random.normal(k, shape, f32) * scale

    return {
        'x_prompt': nrm(ks[0], (BATCH, SEQ, D), 1.0),
        'x_sample': nrm(ks[1], (DEC_BATCH, DEC_SEQ, D), 1.0),
        'c_prompt': nrm(ks[2], (BATCH, D), 1.0),
        'c_sample': nrm(ks[3], (DEC_BATCH, D), 1.0),
        'rel_bias': nrm(ks[4], (N_BUCKETS, HA), 0.5),
        'w_ada': nrm(ks[5], (DEPTH, D, 6 * D), 0.2 * D ** -0.5),
        'b_ada': nrm(ks[6], (DEPTH, 6 * D), 0.02),
        'g_norm1': 1.0 + nrm(ks[7], (DEPTH, D), 0.02),
        'w_in': nrm(ks[8], (DEPTH, D, D_PROJ), D ** -0.5),
        'lambda_q1': nrm(ks[9], (DEPTH, HD_A), 0.1),
        'lambda_k1': nrm(ks[10], (DEPTH, HD_A), 0.1),
        'lambda_q2': nrm(ks[11], (DEPTH, HD_A), 0.1),
        'lambda_k2': nrm(ks[12], (DEPTH, HD_A), 0.1),
        'g_subln': 1.0 + nrm(ks[13], (DEPTH, 2 * HD_A), 0.02),
        'g_qnorm': 1.0 + nrm(ks[14], (DEPTH, HD_B), 0.02),
        'g_knorm': 1.0 + nrm(ks[15], (DEPTH, HD_B), 0.02),
        'w_out': nrm(ks[16], (DEPTH, D_MIX_OUT, D), D_MIX_OUT ** -0.5),
        'g_norm2': 1.0 + nrm(ks[17], (DEPTH, D), 0.02),
        'w_router': nrm(ks[18], (DEPTH, D, N_EXPERTS), D ** -0.5),
        'router_bias': nrm(ks[19], (DEPTH, N_EXPERTS), 0.01),
        'w_exp_gate': nrm(ks[20], (DEPTH, N_EXPERTS, D, D_EXPERT), D ** -0.5),
        'w_exp_up': nrm(ks[21], (DEPTH, N_EXPERTS, D, D_EXPERT), D ** -0.5),
        'w_exp_down': nrm(ks[22], (DEPTH, N_EXPERTS, D_EXPERT, D), D_EXPERT ** -0.5),
        'w_sh_gate': nrm(ks[23], (DEPTH, D, D_SHARED), D ** -0.5),
        'w_sh_up': nrm(ks[24], (DEPTH, D, D_SHARED), D ** -0.5),
        'w_sh_down': nrm(ks[25], (DEPTH, D_SHARED, D), D_SHARED ** -0.5),
        'g_final': 1.0 + nrm(ks[26], (D,), 0.02),
    }


def reference(x_prompt, x_sample, c_prompt, c_sample, rel_bias, w_ada, b_ada, g_norm1, w_in,
              lambda_q1, lambda_k1, lambda_q2, lambda_k2, g_subln, g_qnorm, g_knorm, w_out, g_norm2,
              w_router, router_bias, w_exp_gate, w_exp_up, w_exp_down, w_sh_gate, w_sh_up, w_sh_down,
              g_final):
    y_prompt = trunk(x_prompt, c_prompt, rel_bias, w_ada, b_ada, g_norm1, w_in, lambda_q1, lambda_k1,
                     lambda_q2, lambda_k2, g_subln, g_qnorm, g_knorm, w_out, g_norm2, w_router, router_bias,
                     w_exp_gate, w_exp_up, w_exp_down, w_sh_gate, w_sh_up, w_sh_down, g_final)
    y_sample = trunk(x_sample, c_sample, rel_bias, w_ada, b_ada, g_norm1, w_in, lambda_q1, lambda_k1,
                     lambda_q2, lambda_k2, g_subln, g_qnorm, g_knorm, w_out, g_norm2, w_router, router_bias,
                     w_exp_gate, w_exp_up, w_exp_down, w_sh_gate, w_sh_up, w_sh_down, g_final)
    return (y_prompt, y_sample)
```

```python
import functools
import math

import jax
import jax.numpy as jnp
import numpy as np
from jax import lax
from jax.experimental import pallas as pl
from jax.experimental.pallas import tpu as pltpu

F32 = jnp.float32
BF16 = jnp.bfloat16
I32 = jnp.int32

D_MODEL = 1024
GRID_W = 64
HA = 4
HD_A = 64
HB = 8
KV_B = 2
HD_B = 64
ROPE_THETA = 10000.0
N_BUCKETS = 32
MAX_DISTANCE = 128
A_Q = HA * 2 * HD_A
A_K = HA * 2 * HD_A
A_V = HA * 2 * HD_A
B_Q = HB * HD_B
B_K = KV_B * HD_B
B_V = KV_B * HD_B
N_EXPERTS = 256
TOP_K = 8
N_GROUPS = 8
TOPK_GROUPS = 4
D_EXPERT = 256
D_SHARED = 256
ROUTE_SCALE = 2.5
EPS = 1e-6
LAYER = 0
LAM_INIT = 0.8 - 0.6 * math.exp(-0.3 * LAYER)

LANES = 128
SUBLANES = 8
V7X_VMEM_BYTES = 64 * 1024 * 1024

TM = 512
TQ = 256
TK = 256
MAP_ROWS = 2 * HD_A
N_MAPS = 2 * HA + HB
K_COLS = A_K + B_K
V_ROWS = A_V + B_V
BM = 256
TD = 512
TF = 128
LOG2E = math.log2(math.e)
Q_SCALE = (HD_A ** -0.5) * LOG2E
NT_DIMS = (((1,), (1,)), ((), ()))
NEG_BIG = -1e30


def _cparams(semantics, vmem_mb):
    assert vmem_mb * 1024 * 1024 <= V7X_VMEM_BYTES
    return pltpu.CompilerParams(dimension_semantics=semantics, vmem_limit_bytes=vmem_mb * 1024 * 1024)


def _rms(x, axis):
    return x * lax.rsqrt(jnp.mean(x * x, axis=axis, keepdims=True) + EPS)


def _silu(x):
    return x * jax.nn.sigmoid(x)


def _ada_kernel(c_ref, w_ref, b_ref, o_ref):
    a = _silu(c_ref[...])
    o_ref[...] = jnp.dot(a, w_ref[...], preferred_element_type=F32) + b_ref[...]


def _ada(c_all, w, b):
    nb, d = c_all.shape
    n = w.shape[1]
    tn = 768
    return pl.pallas_call(
        _ada_kernel,
        grid=(n // tn,),
        in_specs=[pl.BlockSpec((nb, d), lambda j: (0, 0)),
                  pl.BlockSpec((d, tn), lambda j: (0, j)),
                  pl.BlockSpec((1, tn), lambda j: (0, j))],
        out_specs=pl.BlockSpec((nb, tn), lambda j: (0, j)),
        out_shape=jax.ShapeDtypeStruct((nb, n), F32),
        compiler_params=_cparams(("parallel",), 24),
        name="ada",
    )(c_all, w, b.reshape(1, n))


def _inproj_kernel(x_ref, mod_ref, g1_ref, wqT_ref, wk_ref, wvT_ref, gqT_ref, gk_ref,
                   cT_ref, sT_ref, cN_ref, sN_ref, qT_ref, k_ref, vT_ref):
    x = x_ref[...]
    tm = x.shape[0]
    shift = mod_ref[0, 0:1, :]
    scale = mod_ref[0, 1:2, :]
    h = (_rms(x, -1) * g1_ref[...]) * (1.0 + scale) + shift
    hb = h.astype(BF16)
    qT = lax.dot_general(wqT_ref[...], hb, NT_DIMS, preferred_element_type=F32)
    kn = jnp.dot(hb, wk_ref[...], preferred_element_type=F32)
    vT = lax.dot_general(wvT_ref[...], hb, NT_DIMS, preferred_element_type=F32)

    vTb = vT.astype(BF16)
    for c in range(tm // TK):
        vT_ref[0, c] = vTb[:, c * TK:(c + 1) * TK]

    zeros = jnp.zeros((HD_A, tm), BF16)

    def put_map(m, q, upper):
        r0 = m * MAP_ROWS
        if upper:
            qT_ref[0, r0:r0 + HD_A, :] = zeros
            qT_ref[0, r0 + HD_A:r0 + MAP_ROWS, :] = q
        else:
            qT_ref[0, r0:r0 + HD_A, :] = q
            qT_ref[0, r0 + HD_A:r0 + MAP_ROWS, :] = zeros

    for m in range(2 * HA):
        q = (qT[m * HD_A:(m + 1) * HD_A, :] * Q_SCALE).astype(BF16)
        put_map(m, q, upper=(m % 2 == 1))

    cT = cT_ref[...]
    sT = sT_ref[...]
    gq = gqT_ref[...]
    for j in range(HB):
        xh = qT[A_Q + j * HD_B:A_Q + (j + 1) * HD_B, :]
        y = _rms(xh, 0) * gq
        yp = jnp.concatenate([y[16:32], y[0:16], y[48:64], y[32:48]], axis=0)
        r = (y * cT + yp * sT) * Q_SCALE
        put_map(2 * HA + j, r.astype(BF16), upper=(j // (HB // KV_B) == 1))

    kb = kn[:, A_K:K_COLS]
    lane = lax.broadcasted_iota(I32, kb.shape, 1)
    first = lane < HD_B
    x2 = kb * kb
    s0 = jnp.sum(jnp.where(first, x2, 0.0), axis=-1, keepdims=True)
    s1 = jnp.sum(jnp.where(first, 0.0, x2), axis=-1, keepdims=True)
    ms = jnp.where(first, s0, s1) * (1.0 / HD_B)
    y = kb * lax.rsqrt(ms + EPS) * gk_ref[...]
    partner = jnp.where((lane % 32) < 16, pltpu.roll(y, LANES - 16, 1), pltpu.roll(y, 16, 1))
    r = y * cN_ref[...] + partner * sN_ref[...]
    k_ref[:, 0:A_K] = kn[:, 0:A_K].astype(BF16)
    k_ref[:, A_K:K_COLS] = r.astype(BF16)


def _inproj(x2d, nb, s, b_off, mod, g1, wqT, wk, wvT, gqT, gk, cT, sT, cN, sN):
    nps = s // TM
    const = lambda i: (0, 0)
    return pl.pallas_call(
        _inproj_kernel,
        grid=(nb * nps,),
        in_specs=[
            pl.BlockSpec((TM, D_MODEL), lambda i: (i, 0)),
            pl.BlockSpec((1, 6, D_MODEL), lambda i: (b_off + i // nps, 0, 0)),
            pl.BlockSpec((1, D_MODEL), const),
            pl.BlockSpec((A_Q + B_Q, D_MODEL), const),
            pl.BlockSpec((D_MODEL, K_COLS), const),
            pl.BlockSpec((V_ROWS, D_MODEL), const),
            pl.BlockSpec((HD_B, TM), const),
            pl.BlockSpec((1, LANES), const),
            pl.BlockSpec((HD_B, TM), lambda i: (0, i % nps)),
            pl.BlockSpec((HD_B, TM), lambda i: (0, i % nps)),
            pl.BlockSpec((TM, LANES), lambda i: (i % nps, 0)),
            pl.BlockSpec((TM, LANES), lambda i: (i % nps, 0)),
        ],
        out_specs=[
            pl.BlockSpec((1, N_MAPS * MAP_ROWS, TM), lambda i: (i // nps, 0, i % nps)),
            pl.BlockSpec((TM, K_COLS), lambda i: (i, 0)),
            pl.BlockSpec((1, TM // TK, V_ROWS, TK), lambda i: (i // nps, i % nps, 0, 0)),
        ],
        out_shape=[
            jax.ShapeDtypeStruct((nb, N_MAPS * MAP_ROWS, s), BF16),
            jax.ShapeDtypeStruct((nb * s, K_COLS), BF16),
            jax.ShapeDtypeStruct((nb, s // TK, V_ROWS, TK), BF16),
        ],
        compiler_params=_cparams(("parallel",), 48),
        name="inproj",
    )(x2d, mod, g1, wqT, wk, wvT, gqT, gk, cT, sT, cN, sN)


def _attn_kernel(*refs, is_diff, nk):
    if is_diff:
        qT_ref, k_ref, vT_ref, band_ref, far_ref, lam_ref, gsub_ref, o_ref = refs
    else:
        qT_ref, k_ref, vT_ref, o_ref = refs
    qi = pl.program_id(2)
    tq = qT_ref.shape[2]
    dv = vT_ref.shape[2]

    def step(c, carry, q, bias=None, const=None):
        m, l, acc = carry
        off = pl.multiple_of(c * TK, TK)
        kc = k_ref[0, pl.ds(off, TK), :]
        s = jnp.dot(kc, q, preferred_element_type=F32)
        if bias is not None:
            s = s + bias
        cm = jnp.max(s, axis=0, keepdims=True)
        if const is not None:
            cm = cm + const
        m_new = jnp.maximum(m, cm)
        sub = m_new if const is None else m_new - const
        p = jnp.exp2(s - sub)
        alpha = jnp.exp2(m - m_new)
        l = alpha * l + jnp.sum(p, axis=0, keepdims=True)
        acc = alpha * acc + jnp.dot(vT_ref[0, c], p.astype(BF16), preferred_element_type=F32)
        return m_new, l, acc

    outs = []
    for mi in range(2):
        q = qT_ref[0, mi * MAP_ROWS:(mi + 1) * MAP_ROWS, :]
        carry = (jnp.full((1, tq), NEG_BIG, F32), jnp.zeros((1, tq), F32), jnp.zeros((dv, tq), F32))
        if is_diff:
            cneg = far_ref[0, 0, 0:1, :]
            cpos = far_ref[0, 1, 0:1, :]
            carry = lax.fori_loop(0, jnp.maximum(qi - 1, 0),
                                  lambda c, cr: step(c, cr, q, const=cneg), carry)
            for dd in range(3):
                c = qi - 1 + dd
                valid = jnp.logical_and(c >= 0, c < nk)
                carry = lax.cond(valid,
                                 lambda cr, c=c, dd=dd: step(c, cr, q, bias=band_ref[0, dd]),
                                 lambda cr: cr, carry)
            carry = lax.fori_loop(jnp.minimum(qi + 2, nk), nk,
                                  lambda c, cr: step(c, cr, q, const=cpos), carry)
        else:
            carry = lax.fori_loop(0, nk, lambda c, cr: step(c, cr, q), carry)
        _, l, acc = carry
        outs.append(acc * (1.0 / l))

    if is_diff:
        lv = lam_ref[...]
        lam = (jnp.exp(jnp.sum(lv[0:1] * lv[1:2], axis=-1, keepdims=True))
               - jnp.exp(jnp.sum(lv[2:3] * lv[3:4], axis=-1, keepdims=True)) + LAM_INIT)
        o = outs[0] - lam * outs[1]
        o = _rms(o, 0) * gsub_ref[...] * (1.0 - LAM_INIT)
    else:
        o = jnp.concatenate(outs, axis=0)
    o_ref[0] = o.T.astype(BF16)


def _attention(qT, k3, vT4, is_diff, extra):
    nb, s, _ = k3.shape
    nk = s // TK
    nq = s // TQ
    if is_diff:
        q_map = lambda b, u, qi: (b, u, qi)
        k_map = lambda b, u, qi: (b, 0, u)
        v_spec = pl.BlockSpec((1, nk, 2 * HD_A, TK), lambda b, u, qi: (b, 0, u, 0))
        band, far, lam, gsub = extra
        extra_specs = [
            pl.BlockSpec((1, 3, TK, TQ), lambda b, u, qi: (u, 0, 0, 0)),
            pl.BlockSpec((1, 2, SUBLANES, TQ), lambda b, u, qi: (u, 0, 0, 0)),
            pl.BlockSpec((4, HD_A), lambda b, u, qi: (0, 0)),
            pl.BlockSpec((2 * HD_A, TQ), lambda b, u, qi: (0, 0)),
        ]
        n_units = HA
    else:
        first = (2 * HA * MAP_ROWS) // (2 * MAP_ROWS)
        q_map = lambda b, u, qi: (b, first + u, qi)
        k_map = lambda b, u, qi: (b, 0, A_K // LANES)
        v0 = A_V // HD_B
        per_kv = (HB // KV_B) // 2
        v_spec = pl.BlockSpec((1, nk, HD_B, TK), lambda b, u, qi: (b, 0, v0 + u // per_kv, 0))
        extra_specs = []
        n_units = HB // 2
    return pl.pallas_call(
        functools.partial(_attn_kernel, is_diff=is_diff, nk=nk),
        grid=(nb, n_units, nq),
        in_specs=[pl.BlockSpec((1, 2 * MAP_ROWS, TQ), q_map),
                  pl.BlockSpec((1, s, LANES), k_map),
                  v_spec] + extra_specs,
        out_specs=pl.BlockSpec((1, TQ, LANES), lambda b, u, qi: (b, qi, u)),
        out_shape=jax.ShapeDtypeStruct((nb, s, n_units * LANES), BF16),
        compiler_params=_cparams(("parallel", "parallel", "parallel"), 40),
        name="attn_diff" if is_diff else "attn_gqa",
    )(qT, k3, vT4, *extra)


def _outproj_kernel(xp_ref, aap_ref, abp_ref, xs_ref, aas_ref, abs_ref, mod_ref, woa_ref, wob_ref,
                    g2_ref, wrT_ref, x1_ref, h2_ref, lg_ref, *, n_first):
    def body(x_ref, aa_ref, ab_ref):
        att = (jnp.dot(aa_ref[...], woa_ref[...], preferred_element_type=F32)
               + jnp.dot(ab_ref[...], wob_ref[...], preferred_element_type=F32))
        x1 = x_ref[...] + mod_ref[0, 2:3, :] * att
        h2 = (_rms(x1, -1) * g2_ref[...]) * (1.0 + mod_ref[0, 4:5, :]) + mod_ref[0, 3:4, :]
        x1_ref[...] = x1
        h2_ref[...] = h2
        lg_ref[...] = lax.dot_general(wrT_ref[...], h2.astype(BF16), NT_DIMS, preferred_element_type=F32)

    i = pl.program_id(0)

    @pl.when(i < n_first)
    def _():
        body(xp_ref, aap_ref, abp_ref)

    @pl.when(i >= n_first)
    def _():
        body(xs_ref, aas_ref, abs_ref)


def _outproj(xp, aap, abp, sp, xs, aas, abs_, ss, mod, woa, wob, g2, wrT):
    tp, ts = xp.shape[0], xs.shape[0]
    n_first, n_second = tp // TM, ts // TM
    nbp = tp // sp
    first = lambda i: (jnp.minimum(i, n_first - 1), 0)
    second = lambda i: (jnp.maximum(i - n_first, 0), 0)
    const = lambda i: (0, 0)

    def mod_map(i):
        b = jnp.where(i < n_first, i // (sp // TM), nbp + (i - n_first) // (ss // TM))
        return (b, 0, 0)

    t = tp + ts
    return pl.pallas_call(
        functools.partial(_outproj_kernel, n_first=n_first),
        grid=(n_first + n_second,),
        in_specs=[
            pl.BlockSpec((TM, D_MODEL), first),
            pl.BlockSpec((TM, A_V), first),
            pl.BlockSpec((TM, B_Q), first),
            pl.BlockSpec((TM, D_MODEL), second),
            pl.BlockSpec((TM, A_V), second),
            pl.BlockSpec((TM, B_Q), second),
            pl.BlockSpec((1, 6, D_MODEL), mod_map),
            pl.BlockSpec((A_V, D_MODEL), const),
            pl.BlockSpec((B_Q, D_MODEL), const),
            pl.BlockSpec((1, D_MODEL), const),
            pl.BlockSpec((N_EXPERTS, D_MODEL), const),
        ],
        out_specs=[
            pl.BlockSpec((TM, D_MODEL), lambda i: (i, 0)),
            pl.BlockSpec((TM, D_MODEL), lambda i: (i, 0)),
            pl.BlockSpec((N_EXPERTS, TM), lambda i: (0, i)),
        ],
        out_shape=[
            jax.ShapeDtypeStruct((t, D_MODEL), F32),
            jax.ShapeDtypeStruct((t, D_MODEL), F32),
            jax.ShapeDtypeStruct((N_EXPERTS, t), F32),
        ],
        compiler_params=_cparams(("parallel",), 48),
        name="outproj",
    )(xp, aap, abp, xs, aas, abs_, mod, woa, wob, g2, wrT)


def _route_kernel(lg_ref, rb_ref, idx_ref, gate_ref, rank_ref, cnt_ref):
    i = pl.program_id(0)

    @pl.when(i == 0)
    def _():
        cnt_ref[...] = jnp.zeros_like(cnt_ref)

    scores = jax.nn.sigmoid(lg_ref[...])
    tr = scores.shape[1]
    sel = scores + rb_ref[...]
    gsz = N_EXPERTS // N_GROUPS

    rows = []
    for g in range(N_GROUPS):
        blk = sel[g * gsz:(g + 1) * gsz, :]
        m1 = jnp.max(blk, axis=0, keepdims=True)
        eq = blk == m1
        n1 = jnp.sum(jnp.where(eq, 1.0, 0.0), axis=0, keepdims=True)
        m2 = jnp.max(jnp.where(eq, -jnp.inf, blk), axis=0, keepdims=True)
        rows.append(m1 + jnp.where(n1 >= 2.0, m1, m2))
    gs = jnp.concatenate(rows, axis=0)
    gi = lax.broadcasted_iota(I32, gs.shape, 0)
    beaten = jnp.zeros(gs.shape, F32)
    for gp in range(N_GROUPS):
        row = gs[gp:gp + 1, :]
        ahead = jnp.where(row > gs, 1.0, jnp.where(jnp.logical_and(row == gs, gp < gi), 1.0, 0.0))
        beaten = beaten + ahead
    keep = jnp.where(beaten < float(TOPK_GROUPS), 1.0, 0.0)
    emask = jnp.concatenate(
        [jnp.broadcast_to(keep[g:g + 1, :], (gsz, tr)) for g in range(N_GROUPS)], axis=0) > 0.5
    masked = jnp.where(emask, sel, -jnp.inf)

    ei = lax.broadcasted_iota(I32, masked.shape, 0).astype(F32)
    idxs, ws = [], []
    for _ in range(TOP_K):
        mk = jnp.max(masked, axis=0, keepdims=True)
        ik = jnp.min(jnp.where(masked == mk, ei, float(N_EXPERTS)), axis=0, keepdims=True)
        hit = ei == ik
        ws.append(jnp.sum(jnp.where(hit, scores, 0.0), axis=0, keepdims=True))
        masked = jnp.where(hit, -jnp.inf, masked)
        idxs.append(ik)
    wsum = ws[0]
    for w in ws[1:]:
        wsum = wsum + w
    gate_ref[...] = jnp.concatenate([w / wsum * ROUTE_SCALE for w in ws], axis=0)
    idx_ref[...] = jnp.concatenate(idxs, axis=0)

    chosen = jnp.logical_and(emask, masked == -jnp.inf)
    onehot = jnp.where(chosen, 1.0, 0.0)
    before = (lax.broadcasted_iota(I32, (tr, tr), 0) < lax.broadcasted_iota(I32, (tr, tr), 1))
    upper = jnp.where(before, 1.0, 0.0).astype(BF16)
    prior = jnp.dot(onehot.astype(BF16), upper, preferred_element_type=F32) + cnt_ref[:, 0:1]
    rank_ref[...] = jnp.concatenate(
        [jnp.sum(jnp.where(ei == ik, prior, 0.0), axis=0, keepdims=True) for ik in idxs], axis=0)
    cnt_ref[...] = cnt_ref[...] + jnp.sum(onehot, axis=1, keepdims=True)


def _route(logitsT, rbias):
    e, t = logitsT.shape
    row = pl.BlockSpec((TOP_K, TM), lambda i: (0, i))
    return pl.pallas_call(
        _route_kernel,
        grid=(t // TM,),
        in_specs=[pl.BlockSpec((e, TM), lambda i: (0, i)),
                  pl.BlockSpec((e, 1), lambda i: (0, 0))],
        out_specs=[row, row, row, pl.BlockSpec((e, LANES), lambda i: (0, 0))],
        out_shape=[jax.ShapeDtypeStruct((TOP_K, t), F32),
                   jax.ShapeDtypeStruct((TOP_K, t), F32),
                   jax.ShapeDtypeStruct((TOP_K, t), F32),
                   jax.ShapeDtypeStruct((e, LANES), F32)],
        compiler_params=_cparams(("arbitrary",), 32),
        name="route",
    )(logitsT, rbias)


def _dest_kernel(idx_ref, rank_ref, ps_ref, dest_ref):
    ei = lax.broadcasted_iota(I32, (N_EXPERTS, idx_ref.shape[1]), 0).astype(F32)
    ps = ps_ref[...]
    rows = []
    for k in range(TOP_K):
        hit = ei == idx_ref[k:k + 1, :]
        rows.append(jnp.sum(jnp.where(hit, ps, 0.0), axis=0, keepdims=True) + rank_ref[k:k + 1, :])
    dest_ref[...] = jnp.concatenate(rows, axis=0).astype(I32)


def _dest(idxT, rankT, pad_start):
    t = idxT.shape[1]
    row = pl.BlockSpec((TOP_K, TM), lambda i: (0, i))
    return pl.pallas_call(
        _dest_kernel,
        grid=(t // TM,),
        in_specs=[row, row, pl.BlockSpec((N_EXPERTS, 1), lambda i: (0, 0))],
        out_specs=row,
        out_shape=jax.ShapeDtypeStruct((TOP_K, t), I32),
        compiler_params=_cparams(("parallel",), 24),
        name="dest",
    )(idxT, rankT, pad_start)


def _row_block_copy(src, dst, sem):
    return pltpu.make_async_copy(src, dst, sem)


def _dispatch_kernel(zstart_ref, zflag_ref, nused_ref, dest_ref, h_hbm, xs_hbm, zeros_ref, sem, zsem):
    i = pl.program_id(0)

    @pl.when(i == 0)
    def _():
        zeros_ref[...] = jnp.zeros_like(zeros_ref)

        def zero_copy(e):
            return _row_block_copy(zeros_ref, xs_hbm.at[pl.ds(zstart_ref[e], BM)], zsem)

        def issue(e, c):
            @pl.when(zflag_ref[e] > 0)
            def _():
                zero_copy(e).start()
            return c

        def drain(e, c):
            @pl.when(zflag_ref[e] > 0)
            def _():
                zero_copy(e).wait()
            return c

        def tail_copy(b):
            return _row_block_copy(zeros_ref, xs_hbm.at[pl.ds(b * BM, BM)], zsem)

        def issue_tail(b, c):
            tail_copy(b).start()
            return c

        def drain_tail(b, c):
            tail_copy(b).wait()
            return c

        n_blocks = xs_hbm.shape[0] // BM
        lax.fori_loop(0, N_EXPERTS, issue, 0)
        lax.fori_loop(nused_ref[0], n_blocks, issue_tail, 0)
        lax.fori_loop(0, N_EXPERTS, drain, 0)
        lax.fori_loop(nused_ref[0], n_blocks, drain_tail, 0)

    def body(t, c):
        for k in range(TOP_K):
            _row_block_copy(h_hbm.at[i * TD + t], xs_hbm.at[dest_ref[t * TOP_K + k]], sem).start()
        return c

    lax.fori_loop(0, TD, body, 0)
    _row_block_copy(xs_hbm.at[pl.ds(0, TD * TOP_K)], xs_hbm.at[pl.ds(0, TD * TOP_K)], sem).wait()


def _dispatch(zstart, zflag, n_used, dest_flat, h3, n_slots):
    t = h3.shape[0]
    return pl.pallas_call(
        _dispatch_kernel,
        grid_spec=pltpu.PrefetchScalarGridSpec(
            num_scalar_prefetch=3,
            grid=(t // TD,),
            in_specs=[pl.BlockSpec((TD * TOP_K,), lambda i, zs, zf, nu: (i,), memory_space=pltpu.SMEM),
                      pl.BlockSpec(memory_space=pl.ANY)],
            out_specs=pl.BlockSpec(memory_space=pl.ANY),
            scratch_shapes=[pltpu.VMEM((BM, SUBLANES, LANES), F32),
                            pltpu.SemaphoreType.DMA, pltpu.SemaphoreType.DMA],
        ),
        out_shape=jax.ShapeDtypeStruct((n_slots, SUBLANES, LANES), F32),
        compiler_params=_cparams(("arbitrary",), 16),
        name="dispatch",
    )(zstart, zflag, n_used, dest_flat, h3)


def _ffn_kernel(be_ref, nu_ref, xs_ref, wg_ref, wu_ref, wd_ref, y_ref, wgu_sc, wd_sc):
    b = pl.program_id(0)
    changed = jnp.logical_or(b == 0, be_ref[b] != be_ref[jnp.maximum(b - 1, 0)])

    @pl.when(changed)
    def _():
        wgu_sc[:, 0:D_EXPERT] = wg_ref[0].astype(BF16)
        wgu_sc[:, D_EXPERT:2 * D_EXPERT] = wu_ref[0].astype(BF16)
        wd_sc[...] = wd_ref[0].astype(BF16)

    @pl.when(b < nu_ref[0])
    def _():
        x = jnp.concatenate([xs_ref[:, j, :] for j in range(SUBLANES)], axis=1).astype(BF16)
        gu = jnp.dot(x, wgu_sc[...], preferred_element_type=F32)
        a = (_silu(gu[:, 0:D_EXPERT]) * gu[:, D_EXPERT:2 * D_EXPERT]).astype(BF16)
        y = jnp.dot(a, wd_sc[...], preferred_element_type=F32)
        for j in range(SUBLANES):
            y_ref[:, j, :] = y[:, j * LANES:(j + 1) * LANES]

    @pl.when(b >= nu_ref[0])
    def _():
        y_ref[...] = jnp.zeros_like(y_ref)


def _ffn(blk_e, n_used, xs3, w_eg, w_eu, w_ed):
    n_slots = xs3.shape[0]
    nblocks = n_slots // BM
    return pl.pallas_call(
        _ffn_kernel,
        grid_spec=pltpu.PrefetchScalarGridSpec(
            num_scalar_prefetch=2,
            grid=(nblocks,),
            in_specs=[
                pl.BlockSpec((BM, SUBLANES, LANES), lambda b, be, nu: (jnp.minimum(b, nu[0] - 1), 0, 0)),
                pl.BlockSpec((1, D_MODEL, D_EXPERT), lambda b, be, nu: (be[b], 0, 0)),
                pl.BlockSpec((1, D_MODEL, D_EXPERT), lambda b, be, nu: (be[b], 0, 0)),
                pl.BlockSpec((1, D_EXPERT, D_MODEL), lambda b, be, nu: (be[b], 0, 0)),
            ],
            out_specs=pl.BlockSpec((BM, SUBLANES, LANES), lambda b, be, nu: (b, 0, 0)),
            scratch_shapes=[pltpu.VMEM((D_MODEL, 2 * D_EXPERT), BF16),
                            pltpu.VMEM((D_EXPERT, D_MODEL), BF16)],
        ),
        out_shape=jax.ShapeDtypeStruct((n_slots, SUBLANES, LANES), F32),
        compiler_params=_cparams(("arbitrary",), 32),
        name="ffn",
    )(blk_e, n_used, xs3, w_eg, w_eu, w_ed)


def _final_kernel(dest_ref, x1_ref, h2_ref, gate_ref, mod_ref, wsgu_ref, wsd_ref, gf_ref, y_hbm,
                  o_ref, buf, sem):
    tf = x1_ref.shape[0]

    def issue(t, c):
        for k in range(TOP_K):
            _row_block_copy(y_hbm.at[dest_ref[t * TOP_K + k]], buf.at[k * tf + t], sem).start()
        return c

    lax.fori_loop(0, tf, issue, 0)

    hb = h2_ref[...].astype(BF16)
    gu = jnp.dot(hb, wsgu_ref[...], preferred_element_type=F32)
    a = (_silu(gu[:, 0:D_SHARED]) * gu[:, D_SHARED:2 * D_SHARED]).astype(BF16)
    shared = jnp.dot(a, wsd_ref[...], preferred_element_type=F32)

    _row_block_copy(y_hbm.at[pl.ds(0, TOP_K * tf)], buf, sem).wait()
    gate = gate_ref[...]
    pieces = []
    for j in range(SUBLANES):
        acc = gate[:, 0:1] * buf[pl.ds(0, tf), j, :]
        for k in range(1, TOP_K):
            acc = acc + gate[:, k:k + 1] * buf[pl.ds(k * tf, tf), j, :]
        pieces.append(acc)
    routed = jnp.concatenate(pieces, axis=1)
    x2 = x1_ref[...] + mod_ref[0, 5:6, :] * (routed + shared)
    o_ref[...] = _rms(x2, -1) * gf_ref[...]


def _final(dest_flat, x1, h2, gates, mod, wsgu, wsd, gf, y3, tok_off, nb, s, b_off):
    n_tiles = nb * s // TF
    t_off = tok_off // TF
    tpb = s // TF
    const = lambda i: (0, 0)
    return pl.pallas_call(
        _final_kernel,
        grid=(n_tiles,),
        in_specs=[
            pl.BlockSpec((TF * TOP_K,), lambda i: (t_off + i,), memory_space=pltpu.SMEM),
            pl.BlockSpec((TF, D_MODEL), lambda i: (t_off + i, 0)),
            pl.BlockSpec((TF, D_MODEL), lambda i: (t_off + i, 0)),
            pl.BlockSpec((TF, TOP_K), lambda i: (t_off + i, 0)),
            pl.BlockSpec((1, 6, D_MODEL), lambda i: (b_off + i // tpb, 0, 0)),
            pl.BlockSpec((D_MODEL, 2 * D_SHARED), const),
            pl.BlockSpec((D_SHARED, D_MODEL), const),
            pl.BlockSpec((1, D_MODEL), const),
            pl.BlockSpec(memory_space=pl.ANY),
        ],
        out_specs=pl.BlockSpec((TF, D_MODEL), lambda i: (i, 0)),
        out_shape=jax.ShapeDtypeStruct((nb * s, D_MODEL), F32),
        scratch_shapes=[pltpu.VMEM((TOP_K * TF, SUBLANES, LANES), F32), pltpu.SemaphoreType.DMA],
        compiler_params=_cparams(("arbitrary",), 32),
        name="final",
    )(dest_flat, x1, h2, gates, mod, wsgu, wsd, gf, y3)


def _t5_bucket(rel):
    nb = N_BUCKETS // 2
    max_exact = nb // 2
    ret = jnp.where(rel > 0, nb, 0)
    n = jnp.abs(rel)
    nf = jnp.maximum(n, 1).astype(F32)
    large = max_exact + (jnp.log(nf / max_exact) / math.log(MAX_DISTANCE / max_exact)
                         * (nb - max_exact)).astype(I32)
    large = jnp.minimum(large, nb - 1)
    return ret + jnp.where(n < max_exact, n, large)


def _bias_tables(rel_bias):
    far = TK + 1
    n_far = np.arange(far, 1 << 20, dtype=np.float64)
    assert np.all(N_BUCKETS // 4 + np.log(n_far / (N_BUCKETS // 4)) / math.log(MAX_DISTANCE / (N_BUCKETS // 4))
                  * (N_BUCKETS // 4) >= N_BUCKETS // 2), "relative-position buckets must saturate beyond one chunk"
    kk = jnp.arange(TK, dtype=I32)[:, None]
    qq = jnp.arange(TQ, dtype=I32)[None, :]
    rel = jnp.stack([(dd - 1) * TK + kk - qq for dd in range(3)])
    band = jnp.transpose(rel_bias[_t5_bucket(rel)], (3, 0, 1, 2)).astype(F32) * LOG2E
    ends = rel_bias[_t5_bucket(jnp.array([-far, far], I32))].astype(F32) * LOG2E
    far_c = jnp.broadcast_to(ends.T[:, :, None, None], (HA, 2, SUBLANES, TQ))
    return band, far_c


def _rope_tables(s):
    rows = s // GRID_W
    row_id = jnp.repeat(jnp.arange(rows, dtype=F32), GRID_W)
    col_id = jnp.tile(jnp.arange(GRID_W, dtype=F32), rows)
    half = HD_B // 2
    inv = ROPE_THETA ** (-jnp.arange(0, half, 2, dtype=F32) / half)
    ang_r = row_id[:, None] * inv[None, :]
    ang_c = col_id[:, None] * inv[None, :]
    cr, sr, cc, sc = jnp.cos(ang_r), jnp.sin(ang_r), jnp.cos(ang_c), jnp.sin(ang_c)
    c64 = jnp.concatenate([cr, cr, cc, cc], axis=1)
    s64 = jnp.concatenate([-sr, sr, -sc, sc], axis=1)
    return c64.T, s64.T, jnp.concatenate([c64, c64], axis=1), jnp.concatenate([s64, s64], axis=1)


def kernel(x_prompt, x_sample, c_prompt, c_sample, rel_bias, w_ada, b_ada, g_norm1, w_in, lambda_q1, lambda_k1, lambda_q2, lambda_k2, g_subln, g_qnorm, g_knorm, w_out, g_norm2, w_router, router_bias, w_exp_gate, w_exp_up, w_exp_down, w_sh_gate, w_sh_up, w_sh_down, g_final):
    bp, sp, d = x_prompt.shape
    bs, ss, _ = x_sample.shape
    assert d == D_MODEL and sp % TM == 0 and ss % TM == 0 and TM % TK == 0 and TQ == TK
    tp, ts = bp * sp, bs * ss
    t = tp + ts
    assert t % TD == 0 and tp % TF == 0 and TD % TF == 0

    mod = _ada(jnp.concatenate([c_prompt, c_sample], axis=0), w_ada[LAYER], b_ada[LAYER])
    mod = mod.reshape(bp + bs, 6, d)

    w = w_in[LAYER]
    o1, o2, o3 = A_Q, A_Q + A_K, A_Q + A_K + A_V
    o4, o5 = o3 + B_Q, o3 + B_Q + B_K
    wqT = jnp.concatenate([w[:, :o1], w[:, o3:o4]], axis=1).T.astype(BF16)
    wk = jnp.concatenate([w[:, o1:o2], w[:, o4:o5]], axis=1).astype(BF16)
    wvT = jnp.concatenate([w[:, o2:o3], w[:, o5:]], axis=1).T.astype(BF16)
    g1 = g_norm1[LAYER].reshape(1, d)
    gqT = jnp.broadcast_to(g_qnorm[LAYER][:, None], (HD_B, TM))
    gk = jnp.tile(g_knorm[LAYER], KV_B).reshape(1, LANES)
    band, far_c = _bias_tables(rel_bias)
    lam_rows = jnp.stack([lambda_q1[LAYER], lambda_k1[LAYER], lambda_q2[LAYER], lambda_k2[LAYER]])
    gsub = jnp.broadcast_to(g_subln[LAYER][:, None], (2 * HD_A, TQ))
    wo = w_out[LAYER].astype(BF16)
    woa, wob = wo[:A_V], wo[A_V:]

    def mixers(x, nb, s, b_off):
        cT, sT, cN, sN = _rope_tables(s)
        qT, k2, vT4 = _inproj(x.reshape(nb * s, d), nb, s, b_off, mod, g1, wqT, wk, wvT, gqT, gk, cT, sT, cN, sN)
        k3 = k2.reshape(nb, s, K_COLS)
        oa = _attention(qT, k3, vT4, True, (band, far_c, lam_rows, gsub))
        ob = _attention(qT, k3, vT4, False, ())
        return oa.reshape(nb * s, A_V), ob.reshape(nb * s, B_Q)

    aap, abp = mixers(x_prompt, bp, sp, 0)
    aas, abs_ = mixers(x_sample, bs, ss, bp)

    x1, h2, logitsT = _outproj(
        x_prompt.reshape(tp, d), aap, abp, sp, x_sample.reshape(ts, d), aas, abs_, ss, mod, woa, wob,
        g_norm2[LAYER].reshape(1, d), w_router[LAYER].T.astype(BF16))

    idxT, gateT, rankT, cnt = _route(logitsT, router_bias[LAYER].reshape(N_EXPERTS, 1))

    counts = cnt[:, 0].astype(I32)
    nblk = (counts + BM - 1) // BM
    blk_end = jnp.cumsum(nblk)
    blk_start = blk_end - nblk
    n_blocks = t * TOP_K // BM + N_EXPERTS
    n_slots = n_blocks * BM
    n_used = blk_end[-1:]
    bidx = jnp.arange(n_blocks, dtype=I32)
    blk_e = jnp.minimum(jnp.searchsorted(blk_end, bidx, side="right"), N_EXPERTS - 1).astype(I32)
    blk_e = jnp.where(bidx < n_used[0], blk_e, blk_e[jnp.maximum(n_used[0] - 1, 0)])
    pad_start = (blk_start * BM).astype(F32).reshape(N_EXPERTS, 1)
    zstart = (blk_end * BM - BM).astype(I32)
    zflag = (nblk > 0).astype(I32)

    destT = _dest(idxT, rankT, pad_start)
    dest_flat = destT.T.reshape(t * TOP_K)
    gates = gateT.T

    n_used = n_used.astype(I32)
    xs3 = _dispatch(zstart, zflag, n_used, dest_flat, h2.reshape(t, SUBLANES, LANES), n_slots)
    y3 = _ffn(blk_e, n_used, xs3, w_exp_gate[LAYER], w_exp_up[LAYER], w_exp_down[LAYER])

    wsgu = jnp.concatenate([w_sh_gate[LAYER], w_sh_up[LAYER]], axis=1).astype(BF16)
    wsd = w_sh_down[LAYER].astype(BF16)
    gf = g_final.reshape(1, d)
    y_prompt = _final(dest_flat, x1, h2, gates, mod, wsgu, wsd, gf, y3, 0, bp, sp, 0)
    y_sample = _final(dest_flat, x1, h2, gates, mod, wsgu, wsd, gf, y3, tp, bs, ss, bp)
    return (y_prompt.reshape(bp, sp, d), y_sample.reshape(bs, ss, d))
```

```python
import functools
import math

import jax
import jax.numpy as jnp
import numpy as np
from jax import lax
from jax.experimental import pallas as pl
from jax.experimental.pallas import tpu as pltpu

F32 = jnp.float32
BF16 = jnp.bfloat16
I32 = jnp.int32

D_MODEL = 1024
GRID_W = 64
HA = 4
HD_A = 64
HB = 8
KV_B = 2
HD_B = 64
ROPE_THETA = 10000.0
N_BUCKETS = 32
MAX_DISTANCE = 128
A_Q = HA * 2 * HD_A
A_K = HA * 2 * HD_A
A_V = HA * 2 * HD_A
B_Q = HB * HD_B
B_K = KV_B * HD_B
B_V = KV_B * HD_B
N_EXPERTS = 256
TOP_K = 8
N_GROUPS = 8
TOPK_GROUPS = 4
D_EXPERT = 256
D_SHARED = 256
ROUTE_SCALE = 2.5
EPS = 1e-6
LAYER = 0
LAM_INIT = 0.8 - 0.6 * math.exp(-0.3 * LAYER)

LANES = 128
SUBLANES = 8
V7X_VMEM_BYTES = 64 * 1024 * 1024

TM = 512
TQ = 256
TK = 256
KSUB = 8
MAP_ROWS = 2 * HD_A
N_MAPS = 2 * HA + HB
K_COLS = A_K + B_K
V_ROWS = A_V + B_V
BM = 256
TD = 512
TF = 128
LOG2E = math.log2(math.e)
Q_SCALE = (HD_A ** -0.5) * LOG2E
NT_DIMS = (((1,), (1,)), ((), ()))
NEG_BIG = -1e30


def _cparams(semantics, vmem_mb):
    assert vmem_mb * 1024 * 1024 <= V7X_VMEM_BYTES
    return pltpu.CompilerParams(dimension_semantics=semantics, vmem_limit_bytes=vmem_mb * 1024 * 1024)


def _rms(x, axis):
    return x * lax.rsqrt(jnp.mean(x * x, axis=axis, keepdims=True) + EPS)


def _silu(x):
    return x * jax.nn.sigmoid(x)


def _ada_kernel(c_ref, w_ref, b_ref, o_ref):
    a = _silu(c_ref[...])
    o_ref[...] = jnp.dot(a, w_ref[...], preferred_element_type=F32) + b_ref[...]


def _ada(c_all, w, b):
    nb, d = c_all.shape
    n = w.shape[1]
    tn = 768
    return pl.pallas_call(
        _ada_kernel,
        grid=(n // tn,),
        in_specs=[pl.BlockSpec((nb, d), lambda j: (0, 0)),
                  pl.BlockSpec((d, tn), lambda j: (0, j)),
                  pl.BlockSpec((1, tn), lambda j: (0, j))],
        out_specs=pl.BlockSpec((nb, tn), lambda j: (0, j)),
        out_shape=jax.ShapeDtypeStruct((nb, n), F32),
        compiler_params=_cparams(("parallel",), 24),
        name="ada",
    )(c_all, w, b.reshape(1, n))


def _inproj_kernel(x_ref, mod_ref, g1_ref, wqT_ref, wk_ref, wvT_ref, gqT_ref, gk_ref,
                   cT_ref, sT_ref, cN_ref, sN_ref, qT_ref, k_ref, vT_ref):
    x = x_ref[...]
    tm = x.shape[0]
    shift = mod_ref[0, 0:1, :]
    scale = mod_ref[0, 1:2, :]
    h = (_rms(x, -1) * g1_ref[...]) * (1.0 + scale) + shift
    hb = h.astype(BF16)
    qT = lax.dot_general(wqT_ref[...], hb, NT_DIMS, preferred_element_type=F32)
    kn = jnp.dot(hb, wk_ref[...], preferred_element_type=F32)
    vT = lax.dot_general(wvT_ref[...], hb, NT_DIMS, preferred_element_type=F32)

    vTb = vT.astype(BF16)
    for c in range(tm // TK):
        vT_ref[0, c] = vTb[:, c * TK:(c + 1) * TK]

    zeros = jnp.zeros((HD_A, tm), BF16)

    def put_map(m, q, upper):
        r0 = m * MAP_ROWS
        if upper:
            qT_ref[0, r0:r0 + HD_A, :] = zeros
            qT_ref[0, r0 + HD_A:r0 + MAP_ROWS, :] = q
        else:
            qT_ref[0, r0:r0 + HD_A, :] = q
            qT_ref[0, r0 + HD_A:r0 + MAP_ROWS, :] = zeros

    for m in range(2 * HA):
        q = (qT[m * HD_A:(m + 1) * HD_A, :] * Q_SCALE).astype(BF16)
        put_map(m, q, upper=(m % 2 == 1))

    cT = cT_ref[...]
    sT = sT_ref[...]
    gq = gqT_ref[...]
    for j in range(HB):
        xh = qT[A_Q + j * HD_B:A_Q + (j + 1) * HD_B, :]
        y = _rms(xh, 0) * gq
        yp = jnp.concatenate([y[16:32], y[0:16], y[48:64], y[32:48]], axis=0)
        r = (y * cT + yp * sT) * Q_SCALE
        put_map(2 * HA + j, r.astype(BF16), upper=(j // (HB // KV_B) == 1))

    kb = kn[:, A_K:K_COLS]
    lane = lax.broadcasted_iota(I32, kb.shape, 1)
    first = lane < HD_B
    x2 = kb * kb
    s0 = jnp.sum(jnp.where(first, x2, 0.0), axis=-1, keepdims=True)
    s1 = jnp.sum(jnp.where(first, 0.0, x2), axis=-1, keepdims=True)
    ms = jnp.where(first, s0, s1) * (1.0 / HD_B)
    y = kb * lax.rsqrt(ms + EPS) * gk_ref[...]
    partner = jnp.where((lane % 32) < 16, pltpu.roll(y, LANES - 16, 1), pltpu.roll(y, 16, 1))
    r = y * cN_ref[...] + partner * sN_ref[...]
    k_ref[:, 0:A_K] = kn[:, 0:A_K].astype(BF16)
    k_ref[:, A_K:K_COLS] = r.astype(BF16)


def _inproj(x2d, nb, s, b_off, mod, g1, wqT, wk, wvT, gqT, gk, cT, sT, cN, sN):
    nps = s // TM
    const = lambda i: (0, 0)
    return pl.pallas_call(
        _inproj_kernel,
        grid=(nb * nps,),
        in_specs=[
            pl.BlockSpec((TM, D_MODEL), lambda i: (i, 0)),
            pl.BlockSpec((1, 6, D_MODEL), lambda i: (b_off + i // nps, 0, 0)),
            pl.BlockSpec((1, D_MODEL), const),
            pl.BlockSpec((A_Q + B_Q, D_MODEL), const),
            pl.BlockSpec((D_MODEL, K_COLS), const),
            pl.BlockSpec((V_ROWS, D_MODEL), const),
            pl.BlockSpec((HD_B, TM), const),
            pl.BlockSpec((1, LANES), const),
            pl.BlockSpec((HD_B, TM), lambda i: (0, i % nps)),
            pl.BlockSpec((HD_B, TM), lambda i: (0, i % nps)),
            pl.BlockSpec((TM, LANES), lambda i: (i % nps, 0)),
            pl.BlockSpec((TM, LANES), lambda i: (i % nps, 0)),
        ],
        out_specs=[
            pl.BlockSpec((1, N_MAPS * MAP_ROWS, TM), lambda i: (i // nps, 0, i % nps)),
            pl.BlockSpec((TM, K_COLS), lambda i: (i, 0)),
            pl.BlockSpec((1, TM // TK, V_ROWS, TK), lambda i: (i // nps, i % nps, 0, 0)),
        ],
        out_shape=[
            jax.ShapeDtypeStruct((nb, N_MAPS * MAP_ROWS, s), BF16),
            jax.ShapeDtypeStruct((nb * s, K_COLS), BF16),
            jax.ShapeDtypeStruct((nb, s // TK, V_ROWS, TK), BF16),
        ],
        compiler_params=_cparams(("parallel",), 48),
        name="inproj",
    )(x2d, mod, g1, wqT, wk, wvT, gqT, gk, cT, sT, cN, sN)


def _attn_kernel(*refs, is_diff, n_big):
    if is_diff:
        qT_ref, k_ref, vT_ref, band_ref, lam_ref, gsub_ref, o_ref = refs
    else:
        qT_ref, k_ref, vT_ref, o_ref = refs
    qi = pl.program_id(2)
    tq = qT_ref.shape[2]
    dv = vT_ref.shape[2]
    qs = [qT_ref[0, mi * MAP_ROWS:(mi + 1) * MAP_ROWS, :] for mi in range(2)]

    def step(cb, carry):
        kcs = [k_ref[0, pl.ds(pl.multiple_of((cb * KSUB + j) * TK, TK), TK), :] for j in range(KSUB)]
        new = []
        for mi in range(2):
            m, l, acc = carry[mi]
            parts = []
            for j in range(KSUB):
                sj = jnp.dot(kcs[j], qs[mi], preferred_element_type=F32)
                if is_diff:
                    sj = sj + band_ref[0, jnp.clip(cb * KSUB + j - qi + 2, 0, 4)]
                parts.append(sj)
            cm = jnp.max(parts[0], axis=0, keepdims=True)
            for sj in parts[1:]:
                cm = jnp.maximum(cm, jnp.max(sj, axis=0, keepdims=True))
            m_new = jnp.maximum(m, cm)
            alpha = jnp.exp2(m - m_new)
            l = alpha * l
            acc = alpha * acc
            for j in range(KSUB):
                p = jnp.exp2(parts[j] - m_new)
                l = l + jnp.sum(p, axis=0, keepdims=True)
                acc = acc + jnp.dot(vT_ref[0, cb * KSUB + j], p.astype(BF16), preferred_element_type=F32)
            new.append((m_new, l, acc))
        return tuple(new)

    init = (jnp.full((1, tq), NEG_BIG, F32), jnp.zeros((1, tq), F32), jnp.zeros((dv, tq), F32))
    carry = lax.fori_loop(0, n_big, step, (init, init))
    outs = [acc * (1.0 / l) for _, l, acc in carry]

    if is_diff:
        lv = lam_ref[...]
        lam = (jnp.exp(jnp.sum(lv[0:1] * lv[1:2], axis=-1, keepdims=True))
               - jnp.exp(jnp.sum(lv[2:3] * lv[3:4], axis=-1, keepdims=True)) + LAM_INIT)
        o = outs[0] - lam * outs[1]
        o = _rms(o, 0) * gsub_ref[...] * (1.0 - LAM_INIT)
    else:
        o = jnp.concatenate(outs, axis=0)
    o_ref[0] = o.T.astype(BF16)


def _attention(qT, k3, vT4, is_diff, extra):
    nb, s, _ = k3.shape
    nk = s // TK
    nq = s // TQ
    if is_diff:
        q_map = lambda b, u, qi: (b, u, qi)
        k_map = lambda b, u, qi: (b, 0, u)
        v_spec = pl.BlockSpec((1, nk, 2 * HD_A, TK), lambda b, u, qi: (b, 0, u, 0))
        extra_specs = [
            pl.BlockSpec((1, 5, TK, TQ), lambda b, u, qi: (u, 0, 0, 0)),
            pl.BlockSpec((4, HD_A), lambda b, u, qi: (0, 0)),
            pl.BlockSpec((2 * HD_A, TQ), lambda b, u, qi: (0, 0)),
        ]
        n_units = HA
    else:
        first = (2 * HA * MAP_ROWS) // (2 * MAP_ROWS)
        q_map = lambda b, u, qi: (b, first + u, qi)
        k_map = lambda b, u, qi: (b, 0, A_K // LANES)
        v0 = A_V // HD_B
        per_kv = (HB // KV_B) // 2
        v_spec = pl.BlockSpec((1, nk, HD_B, TK), lambda b, u, qi: (b, 0, v0 + u // per_kv, 0))
        extra_specs = []
        n_units = HB // 2
    return pl.pallas_call(
        functools.partial(_attn_kernel, is_diff=is_diff, n_big=nk // KSUB),
        grid=(nb, n_units, nq),
        in_specs=[pl.BlockSpec((1, 2 * MAP_ROWS, TQ), q_map),
                  pl.BlockSpec((1, s, LANES), k_map),
                  v_spec] + extra_specs,
        out_specs=pl.BlockSpec((1, TQ, LANES), lambda b, u, qi: (b, qi, u)),
        out_shape=jax.ShapeDtypeStruct((nb, s, n_units * LANES), BF16),
        compiler_params=_cparams(("parallel", "parallel", "parallel"), 40),
        name="attn_diff" if is_diff else "attn_gqa",
    )(qT, k3, vT4, *extra)


def _outproj_kernel(xp_ref, aap_ref, abp_ref, xs_ref, aas_ref, abs_ref, mod_ref, woa_ref, wob_ref,
                    g2_ref, wrT_ref, x1_ref, h2_ref, lg_ref, *, n_first):
    def body(x_ref, aa_ref, ab_ref):
        att = (jnp.dot(aa_ref[...], woa_ref[...], preferred_element_type=F32)
               + jnp.dot(ab_ref[...], wob_ref[...], preferred_element_type=F32))
        x1 = x_ref[...] + mod_ref[0, 2:3, :] * att
        h2 = (_rms(x1, -1) * g2_ref[...]) * (1.0 + mod_ref[0, 4:5, :]) + mod_ref[0, 3:4, :]
        x1_ref[...] = x1
        h2_ref[...] = h2
        lg_ref[...] = lax.dot_general(wrT_ref[...], h2.astype(BF16), NT_DIMS, preferred_element_type=F32)

    i = pl.program_id(0)

    @pl.when(i < n_first)
    def _():
        body(xp_ref, aap_ref, abp_ref)

    @pl.when(i >= n_first)
    def _():
        body(xs_ref, aas_ref, abs_ref)


def _outproj(xp, aap, abp, sp, xs, aas, abs_, ss, mod, woa, wob, g2, wrT):
    tp, ts = xp.shape[0], xs.shape[0]
    n_first, n_second = tp // TM, ts // TM
    nbp = tp // sp
    first = lambda i: (jnp.minimum(i, n_first - 1), 0)
    second = lambda i: (jnp.maximum(i - n_first, 0), 0)
    const = lambda i: (0, 0)

    def mod_map(i):
        b = jnp.where(i < n_first, i // (sp // TM), nbp + (i - n_first) // (ss // TM))
        return (b, 0, 0)

    t = tp + ts
    return pl.pallas_call(
        functools.partial(_outproj_kernel, n_first=n_first),
        grid=(n_first + n_second,),
        in_specs=[
            pl.BlockSpec((TM, D_MODEL), first),
            pl.BlockSpec((TM, A_V), first),
            pl.BlockSpec((TM, B_Q), first),
            pl.BlockSpec((TM, D_MODEL), second),
            pl.BlockSpec((TM, A_V), second),
            pl.BlockSpec((TM, B_Q), second),
            pl.BlockSpec((1, 6, D_MODEL), mod_map),
            pl.BlockSpec((A_V, D_MODEL), const),
            pl.BlockSpec((B_Q, D_MODEL), const),
            pl.BlockSpec((1, D_MODEL), const),
            pl.BlockSpec((N_EXPERTS, D_MODEL), const),
        ],
        out_specs=[
            pl.BlockSpec((TM, D_MODEL), lambda i: (i, 0)),
            pl.BlockSpec((TM, D_MODEL), lambda i: (i, 0)),
            pl.BlockSpec((N_EXPERTS, TM), lambda i: (0, i)),
        ],
        out_shape=[
            jax.ShapeDtypeStruct((t, D_MODEL), F32),
            jax.ShapeDtypeStruct((t, D_MODEL), F32),
            jax.ShapeDtypeStruct((N_EXPERTS, t), F32),
        ],
        compiler_params=_cparams(("parallel",), 48),
        name="outproj",
    )(xp, aap, abp, xs, aas, abs_, mod, woa, wob, g2, wrT)


def _route_kernel(lg_ref, rb_ref, idx_ref, gate_ref, rank_ref, cnt_ref):
    i = pl.program_id(0)

    @pl.when(i == 0)
    def _():
        cnt_ref[...] = jnp.zeros_like(cnt_ref)

    scores = jax.nn.sigmoid(lg_ref[...])
    tr = scores.shape[1]
    sel = scores + rb_ref[...]
    gsz = N_EXPERTS // N_GROUPS

    rows = []
    for g in range(N_GROUPS):
        blk = sel[g * gsz:(g + 1) * gsz, :]
        m1 = jnp.max(blk, axis=0, keepdims=True)
        eq = blk == m1
        n1 = jnp.sum(jnp.where(eq, 1.0, 0.0), axis=0, keepdims=True)
        m2 = jnp.max(jnp.where(eq, -jnp.inf, blk), axis=0, keepdims=True)
        rows.append(m1 + jnp.where(n1 >= 2.0, m1, m2))
    gs = jnp.concatenate(rows, axis=0)
    gi = lax.broadcasted_iota(I32, gs.shape, 0)
    beaten = jnp.zeros(gs.shape, F32)
    for gp in range(N_GROUPS):
        row = gs[gp:gp + 1, :]
        ahead = jnp.where(row > gs, 1.0, jnp.where(jnp.logical_and(row == gs, gp < gi), 1.0, 0.0))
        beaten = beaten + ahead
    keep = jnp.where(beaten < float(TOPK_GROUPS), 1.0, 0.0)
    emask = jnp.concatenate(
        [jnp.broadcast_to(keep[g:g + 1, :], (gsz, tr)) for g in range(N_GROUPS)], axis=0) > 0.5
    masked = jnp.where(emask, sel, -jnp.inf)

    ei = lax.broadcasted_iota(I32, masked.shape, 0).astype(F32)
    idxs, ws = [], []
    for _ in range(TOP_K):
        mk = jnp.max(masked, axis=0, keepdims=True)
        ik = jnp.min(jnp.where(masked == mk, ei, float(N_EXPERTS)), axis=0, keepdims=True)
        hit = ei == ik
        ws.append(jnp.sum(jnp.where(hit, scores, 0.0), axis=0, keepdims=True))
        masked = jnp.where(hit, -jnp.inf, masked)
        idxs.append(ik)
    wsum = ws[0]
    for w in ws[1:]:
        wsum = wsum + w
    gate_ref[...] = jnp.concatenate([w / wsum * ROUTE_SCALE for w in ws], axis=0)
    idx_ref[...] = jnp.concatenate(idxs, axis=0)

    chosen = jnp.logical_and(emask, masked == -jnp.inf)
    onehot = jnp.where(chosen, 1.0, 0.0)
    before = (lax.broadcasted_iota(I32, (tr, tr), 0) < lax.broadcasted_iota(I32, (tr, tr), 1))
    upper = jnp.where(before, 1.0, 0.0).astype(BF16)
    prior = jnp.dot(onehot.astype(BF16), upper, preferred_element_type=F32) + cnt_ref[:, 0:1]
    rank_ref[...] = jnp.concatenate(
        [jnp.sum(jnp.where(ei == ik, prior, 0.0), axis=0, keepdims=True) for ik in idxs], axis=0)
    cnt_ref[...] = cnt_ref[...] + jnp.sum(onehot, axis=1, keepdims=True)


def _route(logitsT, rbias):
    e, t = logitsT.shape
    row = pl.BlockSpec((TOP_K, TM), lambda i: (0, i))
    return pl.pallas_call(
        _route_kernel,
        grid=(t // TM,),
        in_specs=[pl.BlockSpec((e, TM), lambda i: (0, i)),
                  pl.BlockSpec((e, 1), lambda i: (0, 0))],
        out_specs=[row, row, row, pl.BlockSpec((e, LANES), lambda i: (0, 0))],
        out_shape=[jax.ShapeDtypeStruct((TOP_K, t), F32),
                   jax.ShapeDtypeStruct((TOP_K, t), F32),
                   jax.ShapeDtypeStruct((TOP_K, t), F32),
                   jax.ShapeDtypeStruct((e, LANES), F32)],
        compiler_params=_cparams(("arbitrary",), 32),
        name="route",
    )(logitsT, rbias)


def _dest_kernel(idx_ref, rank_ref, ps_ref, dest_ref):
    ei = lax.broadcasted_iota(I32, (N_EXPERTS, idx_ref.shape[1]), 0).astype(F32)
    ps = ps_ref[...]
    rows = []
    for k in range(TOP_K):
        hit = ei == idx_ref[k:k + 1, :]
        rows.append(jnp.sum(jnp.where(hit, ps, 0.0), axis=0, keepdims=True) + rank_ref[k:k + 1, :])
    dest_ref[...] = jnp.concatenate(rows, axis=0).astype(I32)


def _dest(idxT, rankT, pad_start):
    t = idxT.shape[1]
    row = pl.BlockSpec((TOP_K, TM), lambda i: (0, i))
    return pl.pallas_call(
        _dest_kernel,
        grid=(t // TM,),
        in_specs=[row, row, pl.BlockSpec((N_EXPERTS, 1), lambda i: (0, 0))],
        out_specs=row,
        out_shape=jax.ShapeDtypeStruct((TOP_K, t), I32),
        compiler_params=_cparams(("parallel",), 24),
        name="dest",
    )(idxT, rankT, pad_start)


def _row_block_copy(src, dst, sem):
    return pltpu.make_async_copy(src, dst, sem)


def _dispatch_kernel(zstart_ref, zflag_ref, nused_ref, dest_ref, h_ref, xs_hbm, zeros_ref, sem, zsem):
    i = pl.program_id(0)

    @pl.when(i == 0)
    def _():
        zeros_ref[...] = jnp.zeros_like(zeros_ref)

        def zero_copy(e):
            return _row_block_copy(zeros_ref, xs_hbm.at[pl.ds(pl.multiple_of(zstart_ref[e], BM), BM)], zsem)

        def issue(e, c):
            @pl.when(zflag_ref[e] > 0)
            def _():
                zero_copy(e).start()
            return c

        def drain(e, c):
            @pl.when(zflag_ref[e] > 0)
            def _():
                zero_copy(e).wait()
            return c

        def tail_copy(b):
            return _row_block_copy(zeros_ref, xs_hbm.at[pl.ds(pl.multiple_of(b * BM, BM), BM)], zsem)

        def issue_tail(b, c):
            tail_copy(b).start()
            return c

        def drain_tail(b, c):
            tail_copy(b).wait()
            return c

        n_blocks = xs_hbm.shape[0] // BM
        lax.fori_loop(0, N_EXPERTS, issue, 0)
        lax.fori_loop(nused_ref[0], n_blocks, issue_tail, 0)
        lax.fori_loop(0, N_EXPERTS, drain, 0)
        lax.fori_loop(nused_ref[0], n_blocks, drain_tail, 0)

    def body(t, c):
        for k in range(TOP_K):
            _row_block_copy(h_ref.at[pl.ds(t, 1)], xs_hbm.at[pl.ds(dest_ref[t * TOP_K + k], 1)], sem).start()
        return c

    lax.fori_loop(0, TD, body, 0)
    _row_block_copy(xs_hbm.at[pl.ds(0, TD * TOP_K)], xs_hbm.at[pl.ds(0, TD * TOP_K)], sem).wait()


def _dispatch(zstart, zflag, n_used, dest_flat, h3, n_slots):
    t = h3.shape[0]
    return pl.pallas_call(
        _dispatch_kernel,
        grid_spec=pltpu.PrefetchScalarGridSpec(
            num_scalar_prefetch=3,
            grid=(t // TD,),
            in_specs=[pl.BlockSpec((TD * TOP_K,), lambda i, zs, zf, nu: (i,), memory_space=pltpu.SMEM),
                      pl.BlockSpec((TD, D_MODEL), lambda i, zs, zf, nu: (i, 0))],
            out_specs=pl.BlockSpec(memory_space=pl.ANY),
            scratch_shapes=[pltpu.VMEM((BM, D_MODEL), F32),
                            pltpu.SemaphoreType.DMA, pltpu.SemaphoreType.DMA],
        ),
        out_shape=jax.ShapeDtypeStruct((n_slots, D_MODEL), F32),
        compiler_params=_cparams(("arbitrary",), 16),
        name="dispatch",
    )(zstart, zflag, n_used, dest_flat, h3)


def _ffn_kernel(be_ref, nu_ref, xs_ref, wg_ref, wu_ref, wd_ref, y_ref, wgu_sc, wd_sc):
    b = pl.program_id(0)
    changed = jnp.logical_or(b == 0, be_ref[b] != be_ref[jnp.maximum(b - 1, 0)])

    @pl.when(changed)
    def _():
        wgu_sc[:, 0:D_EXPERT] = wg_ref[0].astype(BF16)
        wgu_sc[:, D_EXPERT:2 * D_EXPERT] = wu_ref[0].astype(BF16)
        wd_sc[...] = wd_ref[0].astype(BF16)

    @pl.when(b < nu_ref[0])
    def _():
        x = xs_ref[...].astype(BF16)
        gu = jnp.dot(x, wgu_sc[...], preferred_element_type=F32)
        a = (_silu(gu[:, 0:D_EXPERT]) * gu[:, D_EXPERT:2 * D_EXPERT]).astype(BF16)
        y_ref[...] = jnp.dot(a, wd_sc[...], preferred_element_type=F32)

    @pl.when(b >= nu_ref[0])
    def _():
        y_ref[...] = jnp.zeros_like(y_ref)


def _ffn(blk_e, n_used, xs3, w_eg, w_eu, w_ed):
    n_slots = xs3.shape[0]
    nblocks = n_slots // BM
    return pl.pallas_call(
        _ffn_kernel,
        grid_spec=pltpu.PrefetchScalarGridSpec(
            num_scalar_prefetch=2,
            grid=(nblocks,),
            in_specs=[
                pl.BlockSpec((BM, D_MODEL), lambda b, be, nu: (jnp.minimum(b, nu[0] - 1), 0)),
                pl.BlockSpec((1, D_MODEL, D_EXPERT), lambda b, be, nu: (be[b], 0, 0)),
                pl.BlockSpec((1, D_MODEL, D_EXPERT), lambda b, be, nu: (be[b], 0, 0)),
                pl.BlockSpec((1, D_EXPERT, D_MODEL), lambda b, be, nu: (be[b], 0, 0)),
            ],
            out_specs=pl.BlockSpec((BM, D_MODEL), lambda b, be, nu: (b, 0)),
            scratch_shapes=[pltpu.VMEM((D_MODEL, 2 * D_EXPERT), BF16),
                            pltpu.VMEM((D_EXPERT, D_MODEL), BF16)],
        ),
        out_shape=jax.ShapeDtypeStruct((n_slots, D_MODEL), F32),
        compiler_params=_cparams(("arbitrary",), 32),
        name="ffn",
    )(blk_e, n_used, xs3, w_eg, w_eu, w_ed)


def _final_kernel(dest_ref, x1_ref, h2_ref, gate_ref, mod_ref, wsgu_ref, wsd_ref, gf_ref, y_hbm,
                  o_ref, buf, sem):
    tf = x1_ref.shape[0]

    def issue(t, c):
        for k in range(TOP_K):
            _row_block_copy(y_hbm.at[pl.ds(dest_ref[t * TOP_K + k], 1)], buf.at[pl.ds(k * tf + t, 1)], sem).start()
        return c

    lax.fori_loop(0, tf, issue, 0)

    hb = h2_ref[...].astype(BF16)
    gu = jnp.dot(hb, wsgu_ref[...], preferred_element_type=F32)
    a = (_silu(gu[:, 0:D_SHARED]) * gu[:, D_SHARED:2 * D_SHARED]).astype(BF16)
    shared = jnp.dot(a, wsd_ref[...], preferred_element_type=F32)

    _row_block_copy(y_hbm.at[pl.ds(0, TOP_K * tf)], buf, sem).wait()
    gate = gate_ref[...]
    routed = gate[:, 0:1] * buf[0:tf, :]
    for k in range(1, TOP_K):
        routed = routed + gate[:, k:k + 1] * buf[k * tf:(k + 1) * tf, :]
    x2 = x1_ref[...] + mod_ref[0, 5:6, :] * (routed + shared)
    o_ref[...] = _rms(x2, -1) * gf_ref[...]


def _final(dest_flat, x1, h2, gates, mod, wsgu, wsd, gf, y3, tok_off, nb, s, b_off):
    n_tiles = nb * s // TF
    t_off = tok_off // TF
    tpb = s // TF
    const = lambda i: (0, 0)
    return pl.pallas_call(
        _final_kernel,
        grid=(n_tiles,),
        in_specs=[
            pl.BlockSpec((TF * TOP_K,), lambda i: (t_off + i,), memory_space=pltpu.SMEM),
            pl.BlockSpec((TF, D_MODEL), lambda i: (t_off + i, 0)),
            pl.BlockSpec((TF, D_MODEL), lambda i: (t_off + i, 0)),
            pl.BlockSpec((TF, TOP_K), lambda i: (t_off + i, 0)),
            pl.BlockSpec((1, 6, D_MODEL), lambda i: (b_off + i // tpb, 0, 0)),
            pl.BlockSpec((D_MODEL, 2 * D_SHARED), const),
            pl.BlockSpec((D_SHARED, D_MODEL), const),
            pl.BlockSpec((1, D_MODEL), const),
            pl.BlockSpec(memory_space=pl.ANY),
        ],
        out_specs=pl.BlockSpec((TF, D_MODEL), lambda i: (i, 0)),
        out_shape=jax.ShapeDtypeStruct((nb * s, D_MODEL), F32),
        scratch_shapes=[pltpu.VMEM((TOP_K * TF, D_MODEL), F32), pltpu.SemaphoreType.DMA],
        compiler_params=_cparams(("arbitrary",), 32),
        name="final",
    )(dest_flat, x1, h2, gates, mod, wsgu, wsd, gf, y3)


def _t5_bucket(rel):
    nb = N_BUCKETS // 2
    max_exact = nb // 2
    ret = jnp.where(rel > 0, nb, 0)
    n = jnp.abs(rel)
    nf = jnp.maximum(n, 1).astype(F32)
    large = max_exact + (jnp.log(nf / max_exact) / math.log(MAX_DISTANCE / max_exact)
                         * (nb - max_exact)).astype(I32)
    large = jnp.minimum(large, nb - 1)
    return ret + jnp.where(n < max_exact, n, large)


def _bias_tables(rel_bias):
    n_far = np.arange(TK + 1, 1 << 20, dtype=np.float64)
    assert np.all(N_BUCKETS // 4 + np.log(n_far / (N_BUCKETS // 4)) / math.log(MAX_DISTANCE / (N_BUCKETS // 4))
                  * (N_BUCKETS // 4) >= N_BUCKETS // 2), "relative-position buckets must saturate beyond one chunk"
    kk = jnp.arange(TK, dtype=I32)[:, None]
    qq = jnp.arange(TQ, dtype=I32)[None, :]
    rel = jnp.stack([(dd - 2) * TK + kk - qq for dd in range(5)])
    onehot = jax.nn.one_hot(_t5_bucket(rel), N_BUCKETS, dtype=F32)
    band = jnp.einsum("dkqb,bh->hdkq", onehot, rel_bias.astype(F32), precision=lax.Precision.HIGHEST)
    return band * LOG2E


def _rope_tables(s):
    rows = s // GRID_W
    row_id = jnp.repeat(jnp.arange(rows, dtype=F32), GRID_W)
    col_id = jnp.tile(jnp.arange(GRID_W, dtype=F32), rows)
    half = HD_B // 2
    inv = ROPE_THETA ** (-jnp.arange(0, half, 2, dtype=F32) / half)
    ang_r = row_id[:, None] * inv[None, :]
    ang_c = col_id[:, None] * inv[None, :]
    cr, sr, cc, sc = jnp.cos(ang_r), jnp.sin(ang_r), jnp.cos(ang_c), jnp.sin(ang_c)
    c64 = jnp.concatenate([cr, cr, cc, cc], axis=1)
    s64 = jnp.concatenate([-sr, sr, -sc, sc], axis=1)
    return c64.T, s64.T, jnp.concatenate([c64, c64], axis=1), jnp.concatenate([s64, s64], axis=1)


def kernel(x_prompt, x_sample, c_prompt, c_sample, rel_bias, w_ada, b_ada, g_norm1, w_in, lambda_q1, lambda_k1, lambda_q2, lambda_k2, g_subln, g_qnorm, g_knorm, w_out, g_norm2, w_router, router_bias, w_exp_gate, w_exp_up, w_exp_down, w_sh_gate, w_sh_up, w_sh_down, g_final):
    bp, sp, d = x_prompt.shape
    bs, ss, _ = x_sample.shape
    assert d == D_MODEL and sp % TM == 0 and ss % TM == 0 and TM % TK == 0 and TQ == TK
    assert sp % (KSUB * TK) == 0 and ss % (KSUB * TK) == 0
    tp, ts = bp * sp, bs * ss
    t = tp + ts
    assert t % TD == 0 and tp % TF == 0 and TD % TF == 0

    mod = _ada(jnp.concatenate([c_prompt, c_sample], axis=0), w_ada[LAYER], b_ada[LAYER])
    mod = mod.reshape(bp + bs, 6, d)

    w = w_in[LAYER]
    o1, o2, o3 = A_Q, A_Q + A_K, A_Q + A_K + A_V
    o4, o5 = o3 + B_Q, o3 + B_Q + B_K
    wqT = jnp.concatenate([w[:, :o1], w[:, o3:o4]], axis=1).T.astype(BF16)
    wk = jnp.concatenate([w[:, o1:o2], w[:, o4:o5]], axis=1).astype(BF16)
    wvT = jnp.concatenate([w[:, o2:o3], w[:, o5:]], axis=1).T.astype(BF16)
    g1 = g_norm1[LAYER].reshape(1, d)
    gqT = jnp.broadcast_to(g_qnorm[LAYER][:, None], (HD_B, TM))
    gk = jnp.tile(g_knorm[LAYER], KV_B).reshape(1, LANES)
    band = _bias_tables(rel_bias)
    lam_rows = jnp.stack([lambda_q1[LAYER], lambda_k1[LAYER], lambda_q2[LAYER], lambda_k2[LAYER]])
    gsub = jnp.broadcast_to(g_subln[LAYER][:, None], (2 * HD_A, TQ))
    wo = w_out[LAYER].astype(BF16)
    woa, wob = wo[:A_V], wo[A_V:]

    def mixers(x, nb, s, b_off):
        cT, sT, cN, sN = _rope_tables(s)
        qT, k2, vT4 = _inproj(x.reshape(nb * s, d), nb, s, b_off, mod, g1, wqT, wk, wvT, gqT, gk, cT, sT, cN, sN)
        k3 = k2.reshape(nb, s, K_COLS)
        oa = _attention(qT, k3, vT4, True, (band, lam_rows, gsub))
        ob = _attention(qT, k3, vT4, False, ())
        return oa.reshape(nb * s, A_V), ob.reshape(nb * s, B_Q)

    aap, abp = mixers(x_prompt, bp, sp, 0)
    aas, abs_ = mixers(x_sample, bs, ss, bp)

    x1, h2, logitsT = _outproj(
        x_prompt.reshape(tp, d), aap, abp, sp, x_sample.reshape(ts, d), aas, abs_, ss, mod, woa, wob,
        g_norm2[LAYER].reshape(1, d), w_router[LAYER].T.astype(BF16))

    idxT, gateT, rankT, cnt = _route(logitsT, router_bias[LAYER].reshape(N_EXPERTS, 1))

    counts = cnt[:, 0].astype(I32)
    nblk = (counts + BM - 1) // BM
    blk_end = jnp.cumsum(nblk)
    blk_start = blk_end - nblk
    n_blocks = t * TOP_K // BM + N_EXPERTS
    n_slots = n_blocks * BM
    n_used = blk_end[-1:]
    bidx = jnp.arange(n_blocks, dtype=I32)
    blk_e = jnp.minimum(jnp.searchsorted(blk_end, bidx, side="right"), N_EXPERTS - 1).astype(I32)
    blk_e = jnp.where(bidx < n_used[0], blk_e, blk_e[jnp.maximum(n_used[0] - 1, 0)])
    pad_start = (blk_start * BM).astype(F32).reshape(N_EXPERTS, 1)
    zstart = (blk_end * BM - BM).astype(I32)
    zflag = (nblk > 0).astype(I32)

    destT = _dest(idxT, rankT, pad_start)
    dest_flat = destT.T.reshape(t * TOP_K)
    gates = gateT.T

    n_used = n_used.astype(I32)
    xs3 = _dispatch(zstart, zflag, n_used, dest_flat, h2, n_slots)
    y3 = _ffn(blk_e, n_used, xs3, w_exp_gate[LAYER], w_exp_up[LAYER], w_exp_down[LAYER])

    wsgu = jnp.concatenate([w_sh_gate[LAYER], w_sh_up[LAYER]], axis=1).astype(BF16)
    wsd = w_sh_down[LAYER].astype(BF16)
    gf = g_final.reshape(1, d)
    y_prompt = _final(dest_flat, x1, h2, gates, mod, wsgu, wsd, gf, y3, 0, bp, sp, 0)
    y_sample = _final(dest_flat, x1, h2, gates, mod, wsgu, wsd, gf, y3, tp, bs, ss, bp)
    return (y_prompt.reshape(bp, sp, d), y_sample.reshape(bs, ss, d))
```

```python
import functools
import math

import jax
import jax.numpy as jnp
import numpy as np
from jax import lax
from jax.experimental import pallas as pl
from jax.experimental.pallas import tpu as pltpu
from jax.experimental.pallas import tpu_sc as plsc

F32 = jnp.float32
BF16 = jnp.bfloat16
I32 = jnp.int32

D_MODEL = 1024
GRID_W = 64
HA = 4
HD_A = 64
HB = 8
KV_B = 2
HD_B = 64
ROPE_THETA = 10000.0
N_BUCKETS = 32
MAX_DISTANCE = 128
A_Q = HA * 2 * HD_A
A_K = HA * 2 * HD_A
A_V = HA * 2 * HD_A
B_Q = HB * HD_B
B_K = KV_B * HD_B
B_V = KV_B * HD_B
N_EXPERTS = 256
TOP_K = 8
N_GROUPS = 8
TOPK_GROUPS = 4
D_EXPERT = 256
D_SHARED = 256
ROUTE_SCALE = 2.5
EPS = 1e-6
LAYER = 0
LAM_INIT = 0.8 - 0.6 * math.exp(-0.3 * LAYER)

LANES = 128
SUBLANES = 8
V7X_VMEM_BYTES = 64 * 1024 * 1024

TM = 512
TQ = 256
TK = 256
KSUB = 8
MAP_ROWS = 2 * HD_A
N_MAPS = 2 * HA + HB
K_COLS = A_K + B_K
V_ROWS = A_V + B_V
BM = 256
TF = 256
HALF = D_MODEL // 2
SC_CHUNK = 128
LOG2E = math.log2(math.e)
Q_SCALE = (HD_A ** -0.5) * LOG2E
NT_DIMS = (((1,), (1,)), ((), ()))
NEG_BIG = -1e30


def _cparams(semantics, vmem_mb):
    assert vmem_mb * 1024 * 1024 <= V7X_VMEM_BYTES
    return pltpu.CompilerParams(dimension_semantics=semantics, vmem_limit_bytes=vmem_mb * 1024 * 1024)


def _rms(x, axis):
    return x * lax.rsqrt(jnp.mean(x * x, axis=axis, keepdims=True) + EPS)


def _silu(x):
    return x * jax.nn.sigmoid(x)


def _pack_row(x):
    hi = pltpu.bitcast(x[:, :HALF].astype(BF16).astype(F32), I32)
    lo = pltpu.bitcast(x[:, HALF:].astype(BF16).astype(F32), I32)
    return hi | lax.shift_right_logical(lo, 16)


def _unpack_halves(p):
    hi = pltpu.bitcast(p & jnp.int32(-65536), F32)
    lo = pltpu.bitcast(lax.shift_left(p, 16), F32)
    return hi, lo


def _ada_kernel(c_ref, w_ref, b_ref, o_ref):
    a = _silu(c_ref[...])
    o_ref[...] = jnp.dot(a, w_ref[...], preferred_element_type=F32) + b_ref[...]


def _ada(c_all, w, b):
    nb, d = c_all.shape
    n = w.shape[1]
    tn = 768
    return pl.pallas_call(
        _ada_kernel,
        grid=(n // tn,),
        in_specs=[pl.BlockSpec((nb, d), lambda j: (0, 0)),
                  pl.BlockSpec((d, tn), lambda j: (0, j)),
                  pl.BlockSpec((1, tn), lambda j: (0, j))],
        out_specs=pl.BlockSpec((nb, tn), lambda j: (0, j)),
        out_shape=jax.ShapeDtypeStruct((nb, n), F32),
        compiler_params=_cparams(("parallel",), 24),
        name="ada",
    )(c_all, w, b.reshape(1, n))


def _inproj_kernel(x_ref, mod_ref, g1_ref, wqT_ref, wk_ref, wvT_ref, gqT_ref, gk_ref,
                   cT_ref, sT_ref, cN_ref, sN_ref, qT_ref, k_ref, vT_ref):
    x = x_ref[...]
    tm = x.shape[0]
    shift = mod_ref[0, 0:1, :]
    scale = mod_ref[0, 1:2, :]
    h = (_rms(x, -1) * g1_ref[...]) * (1.0 + scale) + shift
    hb = h.astype(BF16)
    qT = lax.dot_general(wqT_ref[...], hb, NT_DIMS, preferred_element_type=F32)
    kn = jnp.dot(hb, wk_ref[...], preferred_element_type=F32)
    vT = lax.dot_general(wvT_ref[...], hb, NT_DIMS, preferred_element_type=F32)

    vTb = vT.astype(BF16)
    for c in range(tm // TK):
        vT_ref[0, c] = vTb[:, c * TK:(c + 1) * TK]

    zeros = jnp.zeros((HD_A, tm), BF16)

    def put_map(m, q, upper):
        r0 = m * MAP_ROWS
        if upper:
            qT_ref[0, r0:r0 + HD_A, :] = zeros
            qT_ref[0, r0 + HD_A:r0 + MAP_ROWS, :] = q
        else:
            qT_ref[0, r0:r0 + HD_A, :] = q
            qT_ref[0, r0 + HD_A:r0 + MAP_ROWS, :] = zeros

    for m in range(2 * HA):
        q = (qT[m * HD_A:(m + 1) * HD_A, :] * Q_SCALE).astype(BF16)
        put_map(m, q, upper=(m % 2 == 1))

    cT = cT_ref[...]
    sT = sT_ref[...]
    gq = gqT_ref[...]
    for j in range(HB):
        xh = qT[A_Q + j * HD_B:A_Q + (j + 1) * HD_B, :]
        y = _rms(xh, 0) * gq
        yp = jnp.concatenate([y[16:32], y[0:16], y[48:64], y[32:48]], axis=0)
        r = (y * cT + yp * sT) * Q_SCALE
        put_map(2 * HA + j, r.astype(BF16), upper=(j // (HB // KV_B) == 1))

    kb = kn[:, A_K:K_COLS]
    lane = lax.broadcasted_iota(I32, kb.shape, 1)
    first = lane < HD_B
    x2 = kb * kb
    s0 = jnp.sum(jnp.where(first, x2, 0.0), axis=-1, keepdims=True)
    s1 = jnp.sum(jnp.where(first, 0.0, x2), axis=-1, keepdims=True)
    ms = jnp.where(first, s0, s1) * (1.0 / HD_B)
    y = kb * lax.rsqrt(ms + EPS) * gk_ref[...]
    partner = jnp.where((lane % 32) < 16, pltpu.roll(y, LANES - 16, 1), pltpu.roll(y, 16, 1))
    r = y * cN_ref[...] + partner * sN_ref[...]
    k_ref[:, 0:A_K] = kn[:, 0:A_K].astype(BF16)
    k_ref[:, A_K:K_COLS] = r.astype(BF16)


def _inproj(x2d, nb, s, b_off, mod, g1, wqT, wk, wvT, gqT, gk, cT, sT, cN, sN):
    nps = s // TM
    const = lambda i: (0, 0)
    return pl.pallas_call(
        _inproj_kernel,
        grid=(nb * nps,),
        in_specs=[
            pl.BlockSpec((TM, D_MODEL), lambda i: (i, 0)),
            pl.BlockSpec((1, 6, D_MODEL), lambda i: (b_off + i // nps, 0, 0)),
            pl.BlockSpec((1, D_MODEL), const),
            pl.BlockSpec((A_Q + B_Q, D_MODEL), const),
            pl.BlockSpec((D_MODEL, K_COLS), const),
            pl.BlockSpec((V_ROWS, D_MODEL), const),
            pl.BlockSpec((HD_B, TM), const),
            pl.BlockSpec((1, LANES), const),
            pl.BlockSpec((HD_B, TM), lambda i: (0, i % nps)),
            pl.BlockSpec((HD_B, TM), lambda i: (0, i % nps)),
            pl.BlockSpec((TM, LANES), lambda i: (i % nps, 0)),
            pl.BlockSpec((TM, LANES), lambda i: (i % nps, 0)),
        ],
        out_specs=[
            pl.BlockSpec((1, N_MAPS * MAP_ROWS, TM), lambda i: (i // nps, 0, i % nps)),
            pl.BlockSpec((TM, K_COLS), lambda i: (i, 0)),
            pl.BlockSpec((1, TM // TK, V_ROWS, TK), lambda i: (i // nps, i % nps, 0, 0)),
        ],
        out_shape=[
            jax.ShapeDtypeStruct((nb, N_MAPS * MAP_ROWS, s), BF16),
            jax.ShapeDtypeStruct((nb * s, K_COLS), BF16),
            jax.ShapeDtypeStruct((nb, s // TK, V_ROWS, TK), BF16),
        ],
        compiler_params=_cparams(("parallel",), 48),
        name="inproj",
    )(x2d, mod, g1, wqT, wk, wvT, gqT, gk, cT, sT, cN, sN)


def _attn_kernel(*refs, is_diff, n_big):
    if is_diff:
        qT_ref, k_ref, vT_ref, band_ref, lam_ref, gsub_ref, o_ref = refs
    else:
        qT_ref, k_ref, vT_ref, o_ref = refs
    qi = pl.program_id(2)
    tq = qT_ref.shape[2]
    dv = vT_ref.shape[2]
    qs = [qT_ref[0, mi * MAP_ROWS:(mi + 1) * MAP_ROWS, :] for mi in range(2)]

    def step(cb, carry):
        kcs = [k_ref[0, pl.ds(pl.multiple_of((cb * KSUB + j) * TK, TK), TK), :] for j in range(KSUB)]
        new = []
        for mi in range(2):
            m, l, acc = carry[mi]
            parts = []
            for j in range(KSUB):
                sj = jnp.dot(kcs[j], qs[mi], preferred_element_type=F32)
                if is_diff:
                    sj = sj + band_ref[0, jnp.clip(cb * KSUB + j - qi + 2, 0, 4)]
                parts.append(sj)
            cm = jnp.max(parts[0], axis=0, keepdims=True)
            for sj in parts[1:]:
                cm = jnp.maximum(cm, jnp.max(sj, axis=0, keepdims=True))
            m_new = jnp.maximum(m, cm)
            alpha = jnp.exp2(m - m_new)
            l = alpha * l
            acc = alpha * acc
            for j in range(KSUB):
                p = jnp.exp2(parts[j] - m_new)
                l = l + jnp.sum(p, axis=0, keepdims=True)
                acc = acc + jnp.dot(vT_ref[0, cb * KSUB + j], p.astype(BF16), preferred_element_type=F32)
            new.append((m_new, l, acc))
        return tuple(new)

    init = (jnp.full((1, tq), NEG_BIG, F32), jnp.zeros((1, tq), F32), jnp.zeros((dv, tq), F32))
    carry = lax.fori_loop(0, n_big, step, (init, init))
    outs = [acc * (1.0 / l) for _, l, acc in carry]

    if is_diff:
        lv = lam_ref[...]
        lam = (jnp.exp(jnp.sum(lv[0:1] * lv[1:2], axis=-1, keepdims=True))
               - jnp.exp(jnp.sum(lv[2:3] * lv[3:4], axis=-1, keepdims=True)) + LAM_INIT)
        o = outs[0] - lam * outs[1]
        o = _rms(o, 0) * gsub_ref[...] * (1.0 - LAM_INIT)
    else:
        o = jnp.concatenate(outs, axis=0)
    o_ref[0] = o.T.astype(BF16)


def _attention(qT, k3, vT4, is_diff, extra):
    nb, s, _ = k3.shape
    nk = s // TK
    nq = s // TQ
    if is_diff:
        q_map = lambda b, u, qi: (b, u, qi)
        k_map = lambda b, u, qi: (b, 0, u)
        v_spec = pl.BlockSpec((1, nk, 2 * HD_A, TK), lambda b, u, qi: (b, 0, u, 0))
        extra_specs = [
            pl.BlockSpec((1, 5, TK, TQ), lambda b, u, qi: (u, 0, 0, 0)),
            pl.BlockSpec((4, HD_A), lambda b, u, qi: (0, 0)),
            pl.BlockSpec((2 * HD_A, TQ), lambda b, u, qi: (0, 0)),
        ]
        n_units = HA
    else:
        first = (2 * HA * MAP_ROWS) // (2 * MAP_ROWS)
        q_map = lambda b, u, qi: (b, first + u, qi)
        k_map = lambda b, u, qi: (b, 0, A_K // LANES)
        v0 = A_V // HD_B
        per_kv = (HB // KV_B) // 2
        v_spec = pl.BlockSpec((1, nk, HD_B, TK), lambda b, u, qi: (b, 0, v0 + u // per_kv, 0))
        extra_specs = []
        n_units = HB // 2
    return pl.pallas_call(
        functools.partial(_attn_kernel, is_diff=is_diff, n_big=nk // KSUB),
        grid=(nb, n_units, nq),
        in_specs=[pl.BlockSpec((1, 2 * MAP_ROWS, TQ), q_map),
                  pl.BlockSpec((1, s, LANES), k_map),
                  v_spec] + extra_specs,
        out_specs=pl.BlockSpec((1, TQ, LANES), lambda b, u, qi: (b, qi, u)),
        out_shape=jax.ShapeDtypeStruct((nb, s, n_units * LANES), BF16),
        compiler_params=_cparams(("parallel", "parallel", "parallel"), 40),
        name="attn_diff" if is_diff else "attn_gqa",
    )(qT, k3, vT4, *extra)


def _outproj_kernel(xp_ref, aap_ref, abp_ref, xs_ref, aas_ref, abs_ref, mod_ref, woa_ref, wob_ref,
                    g2_ref, wrT_ref, x1_ref, h2_ref, lg_ref, *, n_first):
    def body(x_ref, aa_ref, ab_ref):
        att = (jnp.dot(aa_ref[...], woa_ref[...], preferred_element_type=F32)
               + jnp.dot(ab_ref[...], wob_ref[...], preferred_element_type=F32))
        x1 = x_ref[...] + mod_ref[0, 2:3, :] * att
        h2 = (_rms(x1, -1) * g2_ref[...]) * (1.0 + mod_ref[0, 4:5, :]) + mod_ref[0, 3:4, :]
        x1_ref[...] = x1
        h2_ref[...] = _pack_row(h2)
        lg_ref[...] = lax.dot_general(wrT_ref[...], h2.astype(BF16), NT_DIMS, preferred_element_type=F32)

    i = pl.program_id(0)

    @pl.when(i < n_first)
    def _():
        body(xp_ref, aap_ref, abp_ref)

    @pl.when(i >= n_first)
    def _():
        body(xs_ref, aas_ref, abs_ref)


def _outproj(xp, aap, abp, sp, xs, aas, abs_, ss, mod, woa, wob, g2, wrT):
    tp, ts = xp.shape[0], xs.shape[0]
    n_first, n_second = tp // TM, ts // TM
    nbp = tp // sp
    first = lambda i: (jnp.minimum(i, n_first - 1), 0)
    second = lambda i: (jnp.maximum(i - n_first, 0), 0)
    const = lambda i: (0, 0)

    def mod_map(i):
        b = jnp.where(i < n_first, i // (sp // TM), nbp + (i - n_first) // (ss // TM))
        return (b, 0, 0)

    t = tp + ts
    return pl.pallas_call(
        functools.partial(_outproj_kernel, n_first=n_first),
        grid=(n_first + n_second,),
        in_specs=[
            pl.BlockSpec((TM, D_MODEL), first),
            pl.BlockSpec((TM, A_V), first),
            pl.BlockSpec((TM, B_Q), first),
            pl.BlockSpec((TM, D_MODEL), second),
            pl.BlockSpec((TM, A_V), second),
            pl.BlockSpec((TM, B_Q), second),
            pl.BlockSpec((1, 6, D_MODEL), mod_map),
            pl.BlockSpec((A_V, D_MODEL), const),
            pl.BlockSpec((B_Q, D_MODEL), const),
            pl.BlockSpec((1, D_MODEL), const),
            pl.BlockSpec((N_EXPERTS, D_MODEL), const),
        ],
        out_specs=[
            pl.BlockSpec((TM, D_MODEL), lambda i: (i, 0)),
            pl.BlockSpec((TM, HALF), lambda i: (i, 0)),
            pl.BlockSpec((N_EXPERTS, TM), lambda i: (0, i)),
        ],
        out_shape=[
            jax.ShapeDtypeStruct((t, D_MODEL), F32),
            jax.ShapeDtypeStruct((t, HALF), I32),
            jax.ShapeDtypeStruct((N_EXPERTS, t), F32),
        ],
        compiler_params=_cparams(("parallel",), 48),
        name="outproj",
    )(xp, aap, abp, xs, aas, abs_, mod, woa, wob, g2, wrT)


def _route_kernel(lg_ref, rb_ref, idx_ref, gate_ref, rank_ref, cnt_ref):
    i = pl.program_id(0)

    @pl.when(i == 0)
    def _():
        cnt_ref[...] = jnp.zeros_like(cnt_ref)

    scores = jax.nn.sigmoid(lg_ref[...])
    tr = scores.shape[1]
    sel = scores + rb_ref[...]
    gsz = N_EXPERTS // N_GROUPS

    rows = []
    for g in range(N_GROUPS):
        blk = sel[g * gsz:(g + 1) * gsz, :]
        m1 = jnp.max(blk, axis=0, keepdims=True)
        eq = blk == m1
        n1 = jnp.sum(jnp.where(eq, 1.0, 0.0), axis=0, keepdims=True)
        m2 = jnp.max(jnp.where(eq, -jnp.inf, blk), axis=0, keepdims=True)
        rows.append(m1 + jnp.where(n1 >= 2.0, m1, m2))
    gs = jnp.concatenate(rows, axis=0)
    gi = lax.broadcasted_iota(I32, gs.shape, 0)
    beaten = jnp.zeros(gs.shape, F32)
    for gp in range(N_GROUPS):
        row = gs[gp:gp + 1, :]
        ahead = jnp.where(row > gs, 1.0, jnp.where(jnp.logical_and(row == gs, gp < gi), 1.0, 0.0))
        beaten = beaten + ahead
    keep = jnp.where(beaten < float(TOPK_GROUPS), 1.0, 0.0)
    emask = jnp.concatenate(
        [jnp.broadcast_to(keep[g:g + 1, :], (gsz, tr)) for g in range(N_GROUPS)], axis=0) > 0.5
    masked = jnp.where(emask, sel, -jnp.inf)

    ei = lax.broadcasted_iota(I32, masked.shape, 0).astype(F32)
    idxs, ws = [], []
    for _ in range(TOP_K):
        mk = jnp.max(masked, axis=0, keepdims=True)
        ik = jnp.min(jnp.where(masked == mk, ei, float(N_EXPERTS)), axis=0, keepdims=True)
        hit = ei == ik
        ws.append(jnp.sum(jnp.where(hit, scores, 0.0), axis=0, keepdims=True))
        masked = jnp.where(hit, -jnp.inf, masked)
        idxs.append(ik)
    wsum = ws[0]
    for w in ws[1:]:
        wsum = wsum + w
    gate_ref[...] = jnp.concatenate([w / wsum * ROUTE_SCALE for w in ws], axis=0)
    idx_ref[...] = jnp.concatenate(idxs, axis=0)

    chosen = jnp.logical_and(emask, masked == -jnp.inf)
    onehot = jnp.where(chosen, 1.0, 0.0)
    before = (lax.broadcasted_iota(I32, (tr, tr), 0) < lax.broadcasted_iota(I32, (tr, tr), 1))
    upper = jnp.where(before, 1.0, 0.0).astype(BF16)
    prior = jnp.dot(onehot.astype(BF16), upper, preferred_element_type=F32) + cnt_ref[:, 0:1]
    rank_ref[...] = jnp.concatenate(
        [jnp.sum(jnp.where(ei == ik, prior, 0.0), axis=0, keepdims=True) for ik in idxs], axis=0)
    cnt_ref[...] = cnt_ref[...] + jnp.sum(onehot, axis=1, keepdims=True)


def _route(logitsT, rbias):
    e, t = logitsT.shape
    row = pl.BlockSpec((TOP_K, TM), lambda i: (0, i))
    return pl.pallas_call(
        _route_kernel,
        grid=(t // TM,),
        in_specs=[pl.BlockSpec((e, TM), lambda i: (0, i)),
                  pl.BlockSpec((e, 1), lambda i: (0, 0))],
        out_specs=[row, row, row, pl.BlockSpec((e, LANES), lambda i: (0, 0))],
        out_shape=[jax.ShapeDtypeStruct((TOP_K, t), F32),
                   jax.ShapeDtypeStruct((TOP_K, t), F32),
                   jax.ShapeDtypeStruct((TOP_K, t), F32),
                   jax.ShapeDtypeStruct((e, LANES), F32)],
        compiler_params=_cparams(("arbitrary",), 32),
        name="route",
    )(logitsT, rbias)


def _dest_kernel(idx_ref, rank_ref, ps_ref, dest_ref):
    ei = lax.broadcasted_iota(I32, (N_EXPERTS, idx_ref.shape[1]), 0).astype(F32)
    ps = ps_ref[...]
    rows = []
    for k in range(TOP_K):
        hit = ei == idx_ref[k:k + 1, :]
        rows.append(jnp.sum(jnp.where(hit, ps, 0.0), axis=0, keepdims=True) + rank_ref[k:k + 1, :])
    dest_ref[...] = jnp.concatenate(rows, axis=0).astype(I32)


def _dest(idxT, rankT, pad_start):
    t = idxT.shape[1]
    row = pl.BlockSpec((TOP_K, TM), lambda i: (0, i))
    return pl.pallas_call(
        _dest_kernel,
        grid=(t // TM,),
        in_specs=[row, row, pl.BlockSpec((N_EXPERTS, 1), lambda i: (0, 0))],
        out_specs=row,
        out_shape=jax.ShapeDtypeStruct((TOP_K, t), I32),
        compiler_params=_cparams(("parallel",), 24),
        name="dest",
    )(idxT, rankT, pad_start)


def _sc_workers(mesh, n_tokens):
    n_workers = mesh.num_cores * mesh.num_subcores
    per_worker = n_tokens // n_workers
    assert per_worker * n_workers == n_tokens and per_worker % SC_CHUNK == 0
    return per_worker


def _sc_scratch(width):
    return ([pltpu.VMEM((SC_CHUNK,), I32)] * TOP_K
            + [pltpu.VMEM((SC_CHUNK, width), I32), pltpu.SemaphoreType.DMA])


def _sc_dispatch(rows, destT, n_slots):
    t, width = rows.shape
    mesh = plsc.VectorSubcoreMesh(core_axis_name="c", subcore_axis_name="s")
    per_worker = _sc_workers(mesh, t)

    @functools.partial(pl.kernel, mesh=mesh, out_type=jax.ShapeDtypeStruct((n_slots, width), I32),
                       scratch_types=_sc_scratch(width))
    def scatter_rows(rows_hbm, dest_hbm, xs_hbm, *scratch):
        idx, buf, sem = scratch[:TOP_K], scratch[TOP_K], scratch[TOP_K + 1]
        base = (lax.axis_index("s") * mesh.num_cores + lax.axis_index("c")) * per_worker

        @pl.loop(0, per_worker // SC_CHUNK)
        def _(i):
            t0 = pl.multiple_of(base + i * SC_CHUNK, SC_CHUNK)
            for k in range(TOP_K):
                pltpu.sync_copy(dest_hbm.at[k, pl.ds(t0, SC_CHUNK)], idx[k])
            pltpu.sync_copy(rows_hbm.at[pl.ds(t0, SC_CHUNK)], buf)
            copies = [pltpu.async_copy(buf, xs_hbm.at[idx[k]], sem) for k in range(TOP_K)]
            for c in copies:
                c.wait()

    return scatter_rows(rows, destT)


def _sc_combine(y, destT):
    width = y.shape[1]
    t = destT.shape[1]
    mesh = plsc.VectorSubcoreMesh(core_axis_name="c", subcore_axis_name="s")
    per_worker = _sc_workers(mesh, t)

    @functools.partial(pl.kernel, mesh=mesh, out_type=jax.ShapeDtypeStruct((TOP_K, t, width), I32),
                       scratch_types=_sc_scratch(width))
    def gather_rows(y_hbm, dest_hbm, out_hbm, *scratch):
        idx, buf, sem = scratch[:TOP_K], scratch[TOP_K], scratch[TOP_K + 1]
        base = (lax.axis_index("s") * mesh.num_cores + lax.axis_index("c")) * per_worker

        @pl.loop(0, per_worker // SC_CHUNK)
        def _(i):
            t0 = pl.multiple_of(base + i * SC_CHUNK, SC_CHUNK)
            for k in range(TOP_K):
                pltpu.sync_copy(dest_hbm.at[k, pl.ds(t0, SC_CHUNK)], idx[k])
            for k in range(TOP_K):
                pltpu.async_copy(y_hbm.at[idx[k]], buf, sem).wait()
                pltpu.sync_copy(buf, out_hbm.at[k, pl.ds(t0, SC_CHUNK)])

    return gather_rows(y, destT)


def _ffn_kernel(be_ref, bv_ref, nu_ref, xs_ref, wg_ref, wu_ref, wd_ref, y_ref, wgu_sc, wd_sc):
    b = pl.program_id(0)
    changed = jnp.logical_or(b == 0, be_ref[b] != be_ref[jnp.maximum(b - 1, 0)])

    @pl.when(changed)
    def _():
        wgu_sc[:, 0:D_EXPERT] = wg_ref[0].astype(BF16)
        wgu_sc[:, D_EXPERT:2 * D_EXPERT] = wu_ref[0].astype(BF16)
        wd_sc[...] = wd_ref[0].astype(BF16)

    @pl.when(b < nu_ref[0])
    def _():
        row = lax.broadcasted_iota(I32, (BM, 1), 0)
        hi, lo = _unpack_halves(jnp.where(row < bv_ref[b], xs_ref[...], 0))
        x = jnp.concatenate([hi, lo], axis=1).astype(BF16)
        gu = jnp.dot(x, wgu_sc[...], preferred_element_type=F32)
        a = (_silu(gu[:, 0:D_EXPERT]) * gu[:, D_EXPERT:2 * D_EXPERT]).astype(BF16)
        y_ref[...] = _pack_row(jnp.dot(a, wd_sc[...], preferred_element_type=F32))

    @pl.when(b >= nu_ref[0])
    def _():
        y_ref[...] = jnp.zeros_like(y_ref)


def _ffn(blk_e, blk_valid, n_used, xs, w_eg, w_eu, w_ed):
    n_slots = xs.shape[0]
    nblocks = n_slots // BM
    return pl.pallas_call(
        _ffn_kernel,
        grid_spec=pltpu.PrefetchScalarGridSpec(
            num_scalar_prefetch=3,
            grid=(nblocks,),
            in_specs=[
                pl.BlockSpec((BM, HALF), lambda b, be, bv, nu: (jnp.minimum(b, nu[0] - 1), 0)),
                pl.BlockSpec((1, D_MODEL, D_EXPERT), lambda b, be, bv, nu: (be[b], 0, 0)),
                pl.BlockSpec((1, D_MODEL, D_EXPERT), lambda b, be, bv, nu: (be[b], 0, 0)),
                pl.BlockSpec((1, D_EXPERT, D_MODEL), lambda b, be, bv, nu: (be[b], 0, 0)),
            ],
            out_specs=pl.BlockSpec((BM, HALF), lambda b, be, bv, nu: (b, 0)),
            scratch_shapes=[pltpu.VMEM((D_MODEL, 2 * D_EXPERT), BF16),
                            pltpu.VMEM((D_EXPERT, D_MODEL), BF16)],
        ),
        out_shape=jax.ShapeDtypeStruct((n_slots, HALF), I32),
        compiler_params=_cparams(("arbitrary",), 32),
        name="ffn",
    )(blk_e, blk_valid, n_used, xs, w_eg, w_eu, w_ed)


def _final_kernel(x1_ref, h2_ref, yk_ref, gate_ref, mod_ref, wsgu_ref, wsd_ref, gf_ref, o_ref):
    hi, lo = _unpack_halves(h2_ref[...])
    hb = jnp.concatenate([hi, lo], axis=1).astype(BF16)
    gu = jnp.dot(hb, wsgu_ref[...], preferred_element_type=F32)
    a = (_silu(gu[:, 0:D_SHARED]) * gu[:, D_SHARED:2 * D_SHARED]).astype(BF16)
    shared = jnp.dot(a, wsd_ref[...], preferred_element_type=F32)

    gate = gate_ref[...]
    acc_hi = acc_lo = None
    for k in range(TOP_K):
        hi, lo = _unpack_halves(yk_ref[k])
        g = gate[:, k:k + 1]
        acc_hi = g * hi if acc_hi is None else acc_hi + g * hi
        acc_lo = g * lo if acc_lo is None else acc_lo + g * lo
    routed = jnp.concatenate([acc_hi, acc_lo], axis=1)
    x2 = x1_ref[...] + mod_ref[0, 5:6, :] * (routed + shared)
    o_ref[...] = _rms(x2, -1) * gf_ref[...]


def _final(x1, h2p, yk, gates, mod, wsgu, wsd, gf, tok_off, nb, s, b_off):
    n_tiles = nb * s // TF
    t_off = tok_off // TF
    tpb = s // TF
    const = lambda i: (0, 0)
    return pl.pallas_call(
        _final_kernel,
        grid=(n_tiles,),
        in_specs=[
            pl.BlockSpec((TF, D_MODEL), lambda i: (t_off + i, 0)),
            pl.BlockSpec((TF, HALF), lambda i: (t_off + i, 0)),
            pl.BlockSpec((TOP_K, TF, HALF), lambda i: (0, t_off + i, 0)),
            pl.BlockSpec((TF, TOP_K), lambda i: (t_off + i, 0)),
            pl.BlockSpec((1, 6, D_MODEL), lambda i: (b_off + i // tpb, 0, 0)),
            pl.BlockSpec((D_MODEL, 2 * D_SHARED), const),
            pl.BlockSpec((D_SHARED, D_MODEL), const),
            pl.BlockSpec((1, D_MODEL), const),
        ],
        out_specs=pl.BlockSpec((TF, D_MODEL), lambda i: (i, 0)),
        out_shape=jax.ShapeDtypeStruct((nb * s, D_MODEL), F32),
        compiler_params=_cparams(("parallel",), 40),
        name="final",
    )(x1, h2p, yk, gates, mod, wsgu, wsd, gf)


def _t5_bucket(rel):
    nb = N_BUCKETS // 2
    max_exact = nb // 2
    ret = jnp.where(rel > 0, nb, 0)
    n = jnp.abs(rel)
    nf = jnp.maximum(n, 1).astype(F32)
    large = max_exact + (jnp.log(nf / max_exact) / math.log(MAX_DISTANCE / max_exact)
                         * (nb - max_exact)).astype(I32)
    large = jnp.minimum(large, nb - 1)
    return ret + jnp.where(n < max_exact, n, large)


def _bias_tables(rel_bias):
    n_far = np.arange(TK + 1, 1 << 20, dtype=np.float64)
    assert np.all(N_BUCKETS // 4 + np.log(n_far / (N_BUCKETS // 4)) / math.log(MAX_DISTANCE / (N_BUCKETS // 4))
                  * (N_BUCKETS // 4) >= N_BUCKETS // 2), "relative-position buckets must saturate beyond one chunk"
    kk = jnp.arange(TK, dtype=I32)[:, None]
    qq = jnp.arange(TQ, dtype=I32)[None, :]
    rel = jnp.stack([(dd - 2) * TK + kk - qq for dd in range(5)])
    onehot = jax.nn.one_hot(_t5_bucket(rel), N_BUCKETS, dtype=F32)
    band = jnp.einsum("dkqb,bh->hdkq", onehot, rel_bias.astype(F32), precision=lax.Precision.HIGHEST)
    return band * LOG2E


def _rope_tables(s):
    rows = s // GRID_W
    row_id = jnp.repeat(jnp.arange(rows, dtype=F32), GRID_W)
    col_id = jnp.tile(jnp.arange(GRID_W, dtype=F32), rows)
    half = HD_B // 2
    inv = ROPE_THETA ** (-jnp.arange(0, half, 2, dtype=F32) / half)
    ang_r = row_id[:, None] * inv[None, :]
    ang_c = col_id[:, None] * inv[None, :]
    cr, sr, cc, sc = jnp.cos(ang_r), jnp.sin(ang_r), jnp.cos(ang_c), jnp.sin(ang_c)
    c64 = jnp.concatenate([cr, cr, cc, cc], axis=1)
    s64 = jnp.concatenate([-sr, sr, -sc, sc], axis=1)
    return c64.T, s64.T, jnp.concatenate([c64, c64], axis=1), jnp.concatenate([s64, s64], axis=1)


def kernel(x_prompt, x_sample, c_prompt, c_sample, rel_bias, w_ada, b_ada, g_norm1, w_in, lambda_q1, lambda_k1, lambda_q2, lambda_k2, g_subln, g_qnorm, g_knorm, w_out, g_norm2, w_router, router_bias, w_exp_gate, w_exp_up, w_exp_down, w_sh_gate, w_sh_up, w_sh_down, g_final):
    bp, sp, d = x_prompt.shape
    bs, ss, _ = x_sample.shape
    assert d == D_MODEL and sp % TM == 0 and ss % TM == 0 and TM % TK == 0 and TQ == TK
    assert sp % (KSUB * TK) == 0 and ss % (KSUB * TK) == 0
    tp, ts = bp * sp, bs * ss
    t = tp + ts
    assert tp % TF == 0 and ts % TF == 0

    mod = _ada(jnp.concatenate([c_prompt, c_sample], axis=0), w_ada[LAYER], b_ada[LAYER])
    mod = mod.reshape(bp + bs, 6, d)

    w = w_in[LAYER]
    o1, o2, o3 = A_Q, A_Q + A_K, A_Q + A_K + A_V
    o4, o5 = o3 + B_Q, o3 + B_Q + B_K
    wqT = jnp.concatenate([w[:, :o1], w[:, o3:o4]], axis=1).T.astype(BF16)
    wk = jnp.concatenate([w[:, o1:o2], w[:, o4:o5]], axis=1).astype(BF16)
    wvT = jnp.concatenate([w[:, o2:o3], w[:, o5:]], axis=1).T.astype(BF16)
    g1 = g_norm1[LAYER].reshape(1, d)
    gqT = jnp.broadcast_to(g_qnorm[LAYER][:, None], (HD_B, TM))
    gk = jnp.tile(g_knorm[LAYER], KV_B).reshape(1, LANES)
    band = _bias_tables(rel_bias)
    lam_rows = jnp.stack([lambda_q1[LAYER], lambda_k1[LAYER], lambda_q2[LAYER], lambda_k2[LAYER]])
    gsub = jnp.broadcast_to(g_subln[LAYER][:, None], (2 * HD_A, TQ))
    wo = w_out[LAYER].astype(BF16)
    woa, wob = wo[:A_V], wo[A_V:]

    def mixers(x, nb, s, b_off):
        cT, sT, cN, sN = _rope_tables(s)
        qT, k2, vT4 = _inproj(x.reshape(nb * s, d), nb, s, b_off, mod, g1, wqT, wk, wvT, gqT, gk, cT, sT, cN, sN)
        k3 = k2.reshape(nb, s, K_COLS)
        oa = _attention(qT, k3, vT4, True, (band, lam_rows, gsub))
        ob = _attention(qT, k3, vT4, False, ())
        return oa.reshape(nb * s, A_V), ob.reshape(nb * s, B_Q)

    aap, abp = mixers(x_prompt, bp, sp, 0)
    aas, abs_ = mixers(x_sample, bs, ss, bp)

    x1, h2p, logitsT = _outproj(
        x_prompt.reshape(tp, d), aap, abp, sp, x_sample.reshape(ts, d), aas, abs_, ss, mod, woa, wob,
        g_norm2[LAYER].reshape(1, d), w_router[LAYER].T.astype(BF16))

    idxT, gateT, rankT, cnt = _route(logitsT, router_bias[LAYER].reshape(N_EXPERTS, 1))

    counts = cnt[:, 0].astype(I32)
    nblk = (counts + BM - 1) // BM
    blk_end = jnp.cumsum(nblk)
    blk_start = blk_end - nblk
    n_blocks = t * TOP_K // BM + N_EXPERTS
    n_slots = n_blocks * BM
    n_used = blk_end[-1:]
    bidx = jnp.arange(n_blocks, dtype=I32)
    blk_e = jnp.minimum(jnp.searchsorted(blk_end, bidx, side="right"), N_EXPERTS - 1).astype(I32)
    blk_e = jnp.where(bidx < n_used[0], blk_e, blk_e[jnp.maximum(n_used[0] - 1, 0)])
    pad_start = (blk_start * BM).astype(F32).reshape(N_EXPERTS, 1)
    blk_valid = jnp.clip(counts[blk_e] - (bidx - blk_start[blk_e]) * BM, 0, BM).astype(I32)

    destT = _dest(idxT, rankT, pad_start)
    gates = gateT.T

    n_used = n_used.astype(I32)
    xs = _sc_dispatch(h2p, destT, n_slots)
    y = _ffn(blk_e, blk_valid, n_used, xs, w_exp_gate[LAYER], w_exp_up[LAYER], w_exp_down[LAYER])
    yk = _sc_combine(y, destT)

    wsgu = jnp.concatenate([w_sh_gate[LAYER], w_sh_up[LAYER]], axis=1).astype(BF16)
    wsd = w_sh_down[LAYER].astype(BF16)
    gf = g_final.reshape(1, d)
    y_prompt = _final(x1, h2p, yk, gates, mod, wsgu, wsd, gf, 0, bp, sp, 0)
    y_sample = _final(x1, h2p, yk, gates, mod, wsgu, wsd, gf, tp, bs, ss, bp)
    return (y_prompt.reshape(bp, sp, d), y_sample.reshape(bs, ss, d))
```

```python
import functools
import math

import jax
import jax.numpy as jnp
import numpy as np
from jax import lax
from jax.experimental import pallas as pl
from jax.experimental.pallas import tpu as pltpu
from jax.experimental.pallas import tpu_sc as plsc

F32 = jnp.float32
BF16 = jnp.bfloat16
I32 = jnp.int32

D_MODEL = 1024
GRID_W = 64
HA = 4
HD_A = 64
HB = 8
KV_B = 2
HD_B = 64
ROPE_THETA = 10000.0
N_BUCKETS = 32
MAX_DISTANCE = 128
A_Q = HA * 2 * HD_A
A_K = HA * 2 * HD_A
A_V = HA * 2 * HD_A
B_Q = HB * HD_B
B_K = KV_B * HD_B
B_V = KV_B * HD_B
N_EXPERTS = 256
TOP_K = 8
N_GROUPS = 8
TOPK_GROUPS = 4
D_EXPERT = 256
D_SHARED = 256
ROUTE_SCALE = 2.5
EPS = 1e-6
LAYER = 0
LAM_INIT = 0.8 - 0.6 * math.exp(-0.3 * LAYER)

LANES = 128
SUBLANES = 8
V7X_VMEM_BYTES = 64 * 1024 * 1024

TM = 512
TQ = 256
TK = 256
KSUB = 4
MAP_ROWS = 2 * HD_A
N_MAPS = 2 * HA + HB
K_COLS = A_K + B_K
V_ROWS = A_V + B_V
BM = 256
TF = 256
HALF = D_MODEL // 2
SC_CHUNK = 128
LOG2E = math.log2(math.e)
Q_SCALE = (HD_A ** -0.5) * LOG2E
NT_DIMS = (((1,), (1,)), ((), ()))
NEG_BIG = -1e30


def _cparams(semantics, vmem_mb):
    assert vmem_mb * 1024 * 1024 <= V7X_VMEM_BYTES
    return pltpu.CompilerParams(dimension_semantics=semantics, vmem_limit_bytes=vmem_mb * 1024 * 1024)


def _rms(x, axis):
    return x * lax.rsqrt(jnp.mean(x * x, axis=axis, keepdims=True) + EPS)


def _silu(x):
    return x * jax.nn.sigmoid(x)


def _pack_row(x):
    hi = pltpu.bitcast(x[:, :HALF].astype(BF16).astype(F32), I32)
    lo = pltpu.bitcast(x[:, HALF:].astype(BF16).astype(F32), I32)
    return hi | lax.shift_right_logical(lo, 16)


def _unpack_halves(p):
    hi = pltpu.bitcast(p & jnp.int32(-65536), F32)
    lo = pltpu.bitcast(lax.shift_left(p, 16), F32)
    return hi, lo


def _ada_kernel(c_ref, w_ref, b_ref, o_ref):
    a = _silu(c_ref[...])
    o_ref[...] = jnp.dot(a, w_ref[...], preferred_element_type=F32) + b_ref[...]


def _ada(c_all, w, b):
    nb, d = c_all.shape
    n = w.shape[1]
    tn = 768
    return pl.pallas_call(
        _ada_kernel,
        grid=(n // tn,),
        in_specs=[pl.BlockSpec((nb, d), lambda j: (0, 0)),
                  pl.BlockSpec((d, tn), lambda j: (0, j)),
                  pl.BlockSpec((1, tn), lambda j: (0, j))],
        out_specs=pl.BlockSpec((nb, tn), lambda j: (0, j)),
        out_shape=jax.ShapeDtypeStruct((nb, n), F32),
        compiler_params=_cparams(("parallel",), 24),
        name="ada",
    )(c_all, w, b.reshape(1, n))


def _inproj_kernel(x_ref, mod_ref, g1_ref, wqT_ref, wk_ref, wvT_ref, gqT_ref, gk_ref,
                   cT_ref, sT_ref, cN_ref, sN_ref, qT_ref, k_ref, vT_ref):
    x = x_ref[...]
    tm = x.shape[0]
    shift = mod_ref[0, 0:1, :]
    scale = mod_ref[0, 1:2, :]
    h = (_rms(x, -1) * g1_ref[...]) * (1.0 + scale) + shift
    hb = h.astype(BF16)
    qT = lax.dot_general(wqT_ref[...], hb, NT_DIMS, preferred_element_type=F32)
    kn = jnp.dot(hb, wk_ref[...], preferred_element_type=F32)
    vT = lax.dot_general(wvT_ref[...], hb, NT_DIMS, preferred_element_type=F32)

    vTb = vT.astype(BF16)
    for c in range(tm // TK):
        vT_ref[0, c] = vTb[:, c * TK:(c + 1) * TK]

    zeros = jnp.zeros((HD_A, tm), BF16)

    def put_map(m, q, upper):
        r0 = m * MAP_ROWS
        if upper:
            qT_ref[0, r0:r0 + HD_A, :] = zeros
            qT_ref[0, r0 + HD_A:r0 + MAP_ROWS, :] = q
        else:
            qT_ref[0, r0:r0 + HD_A, :] = q
            qT_ref[0, r0 + HD_A:r0 + MAP_ROWS, :] = zeros

    for m in range(2 * HA):
        q = (qT[m * HD_A:(m + 1) * HD_A, :] * Q_SCALE).astype(BF16)
        put_map(m, q, upper=(m % 2 == 1))

    cT = cT_ref[...]
    sT = sT_ref[...]
    gq = gqT_ref[...]
    for j in range(HB):
        xh = qT[A_Q + j * HD_B:A_Q + (j + 1) * HD_B, :]
        y = _rms(xh, 0) * gq
        yp = jnp.concatenate([y[16:32], y[0:16], y[48:64], y[32:48]], axis=0)
        r = (y * cT + yp * sT) * Q_SCALE
        put_map(2 * HA + j, r.astype(BF16), upper=(j // (HB // KV_B) == 1))

    kb = kn[:, A_K:K_COLS]
    lane = lax.broadcasted_iota(I32, kb.shape, 1)
    first = lane < HD_B
    x2 = kb * kb
    s0 = jnp.sum(jnp.where(first, x2, 0.0), axis=-1, keepdims=True)
    s1 = jnp.sum(jnp.where(first, 0.0, x2), axis=-1, keepdims=True)
    ms = jnp.where(first, s0, s1) * (1.0 / HD_B)
    y = kb * lax.rsqrt(ms + EPS) * gk_ref[...]
    partner = jnp.where((lane % 32) < 16, pltpu.roll(y, LANES - 16, 1), pltpu.roll(y, 16, 1))
    r = y * cN_ref[...] + partner * sN_ref[...]
    k_ref[:, 0:A_K] = kn[:, 0:A_K].astype(BF16)
    k_ref[:, A_K:K_COLS] = r.astype(BF16)


def _inproj(x2d, nb, s, b_off, mod, g1, wqT, wk, wvT, gqT, gk, cT, sT, cN, sN):
    nps = s // TM
    const = lambda i: (0, 0)
    return pl.pallas_call(
        _inproj_kernel,
        grid=(nb * nps,),
        in_specs=[
            pl.BlockSpec((TM, D_MODEL), lambda i: (i, 0)),
            pl.BlockSpec((1, 6, D_MODEL), lambda i: (b_off + i // nps, 0, 0)),
            pl.BlockSpec((1, D_MODEL), const),
            pl.BlockSpec((A_Q + B_Q, D_MODEL), const),
            pl.BlockSpec((D_MODEL, K_COLS), const),
            pl.BlockSpec((V_ROWS, D_MODEL), const),
            pl.BlockSpec((HD_B, TM), const),
            pl.BlockSpec((1, LANES), const),
            pl.BlockSpec((HD_B, TM), lambda i: (0, i % nps)),
            pl.BlockSpec((HD_B, TM), lambda i: (0, i % nps)),
            pl.BlockSpec((TM, LANES), lambda i: (i % nps, 0)),
            pl.BlockSpec((TM, LANES), lambda i: (i % nps, 0)),
        ],
        out_specs=[
            pl.BlockSpec((1, N_MAPS * MAP_ROWS, TM), lambda i: (i // nps, 0, i % nps)),
            pl.BlockSpec((TM, K_COLS), lambda i: (i, 0)),
            pl.BlockSpec((1, TM // TK, V_ROWS, TK), lambda i: (i // nps, i % nps, 0, 0)),
        ],
        out_shape=[
            jax.ShapeDtypeStruct((nb, N_MAPS * MAP_ROWS, s), BF16),
            jax.ShapeDtypeStruct((nb * s, K_COLS), BF16),
            jax.ShapeDtypeStruct((nb, s // TK, V_ROWS, TK), BF16),
        ],
        compiler_params=_cparams(("parallel",), 48),
        name="inproj",
    )(x2d, mod, g1, wqT, wk, wvT, gqT, gk, cT, sT, cN, sN)


def _attn_kernel(*refs, is_diff, n_big, ksub):
    if is_diff:
        qT_ref, k_ref, vT_ref, band_ref, lam_ref, gsub_ref, o_ref, s_a, s_b = refs
    else:
        qT_ref, k_ref, vT_ref, o_ref, s_a, s_b = refs
    qi = pl.program_id(2)
    tq = qT_ref.shape[2]
    dv = vT_ref.shape[2]
    q2 = jnp.concatenate([qT_ref[0, 0:MAP_ROWS, :], qT_ref[0, MAP_ROWS:2 * MAP_ROWS, :]], axis=1)

    def score_stage(g, s_buf):
        cm = None
        for j in range(ksub):
            c = g * ksub + j
            kc = k_ref[0, pl.ds(pl.multiple_of(c * TK, TK), TK), :]
            sj = jnp.dot(kc, q2, preferred_element_type=F32)
            if is_diff:
                bias = band_ref[0, jnp.clip(c - qi + 2, 0, 4)]
                sj = sj + jnp.concatenate([bias, bias], axis=1)
            s_buf[j * TK:(j + 1) * TK, :] = sj
            cj = jnp.max(sj, axis=0, keepdims=True)
            cm = cj if cm is None else jnp.maximum(cm, cj)
        return cm

    def softmax_stage(g, s_buf, cm, carry):
        m, l, acc = carry
        m_new = jnp.maximum(m, cm)
        alpha = jnp.exp2(m - m_new)
        l = alpha * l
        acc = alpha * acc
        for j in range(ksub):
            p = jnp.exp2(s_buf[j * TK:(j + 1) * TK, :] - m_new)
            l = l + jnp.sum(p, axis=0, keepdims=True)
            acc = acc + jnp.dot(vT_ref[0, g * ksub + j], p.astype(BF16), preferred_element_type=F32)
        return m_new, l, acc

    def pair(i, state):
        carry, cm_a = state
        g = 2 * i
        cm_b = score_stage(g + 1, s_b)
        carry = softmax_stage(g, s_a, cm_a, carry)
        cm_a = score_stage(g + 2, s_a)
        carry = softmax_stage(g + 1, s_b, cm_b, carry)
        return carry, cm_a

    init = (jnp.full((1, 2 * tq), NEG_BIG, F32), jnp.zeros((1, 2 * tq), F32), jnp.zeros((dv, 2 * tq), F32))
    carry, cm_a = lax.fori_loop(0, n_big // 2 - 1, pair, (init, score_stage(0, s_a)))
    cm_b = score_stage(n_big - 1, s_b)
    carry = softmax_stage(n_big - 2, s_a, cm_a, carry)
    _, l, acc = softmax_stage(n_big - 1, s_b, cm_b, carry)
    o2 = acc * (1.0 / l)
    outs = [o2[:, 0:tq], o2[:, tq:2 * tq]]

    if is_diff:
        lv = lam_ref[...]
        lam = (jnp.exp(jnp.sum(lv[0:1] * lv[1:2], axis=-1, keepdims=True))
               - jnp.exp(jnp.sum(lv[2:3] * lv[3:4], axis=-1, keepdims=True)) + LAM_INIT)
        o = outs[0] - lam * outs[1]
        o = _rms(o, 0) * gsub_ref[...] * (1.0 - LAM_INIT)
    else:
        o = jnp.concatenate(outs, axis=0)
    o_ref[0] = o.T.astype(BF16)


def _attention(qT, k3, vT4, is_diff, extra):
    nb, s, _ = k3.shape
    nk = s // TK
    nq = s // TQ
    if is_diff:
        q_map = lambda b, u, qi: (b, u, qi)
        k_map = lambda b, u, qi: (b, 0, u)
        v_spec = pl.BlockSpec((1, nk, 2 * HD_A, TK), lambda b, u, qi: (b, 0, u, 0))
        extra_specs = [
            pl.BlockSpec((1, 5, TK, TQ), lambda b, u, qi: (u, 0, 0, 0)),
            pl.BlockSpec((4, HD_A), lambda b, u, qi: (0, 0)),
            pl.BlockSpec((2 * HD_A, TQ), lambda b, u, qi: (0, 0)),
        ]
        n_units = HA
    else:
        first = (2 * HA * MAP_ROWS) // (2 * MAP_ROWS)
        q_map = lambda b, u, qi: (b, first + u, qi)
        k_map = lambda b, u, qi: (b, 0, A_K // LANES)
        v0 = A_V // HD_B
        per_kv = (HB // KV_B) // 2
        v_spec = pl.BlockSpec((1, nk, HD_B, TK), lambda b, u, qi: (b, 0, v0 + u // per_kv, 0))
        extra_specs = []
        n_units = HB // 2
    ksub = KSUB if nk >= 4 * KSUB else KSUB // 2
    n_big = nk // ksub
    assert n_big * ksub == nk and n_big % 2 == 0
    return pl.pallas_call(
        functools.partial(_attn_kernel, is_diff=is_diff, n_big=n_big, ksub=ksub),
        grid=(nb, n_units, nq),
        in_specs=[pl.BlockSpec((1, 2 * MAP_ROWS, TQ), q_map),
                  pl.BlockSpec((1, s, LANES), k_map),
                  v_spec] + extra_specs,
        out_specs=pl.BlockSpec((1, TQ, LANES), lambda b, u, qi: (b, qi, u)),
        out_shape=jax.ShapeDtypeStruct((nb, s, n_units * LANES), BF16),
        scratch_shapes=[pltpu.VMEM((ksub * TK, 2 * TQ), F32)] * 2,
        compiler_params=_cparams(("parallel", "parallel", "parallel"), 40),
        name="attn_diff" if is_diff else "attn_gqa",
    )(qT, k3, vT4, *extra)


def _outproj_kernel(xp_ref, aap_ref, abp_ref, xs_ref, aas_ref, abs_ref, mod_ref, woa_ref, wob_ref,
                    g2_ref, wrT_ref, x1_ref, h2_ref, lg_ref, *, n_first):
    def body(x_ref, aa_ref, ab_ref):
        att = (jnp.dot(aa_ref[...], woa_ref[...], preferred_element_type=F32)
               + jnp.dot(ab_ref[...], wob_ref[...], preferred_element_type=F32))
        x1 = x_ref[...] + mod_ref[0, 2:3, :] * att
        h2 = (_rms(x1, -1) * g2_ref[...]) * (1.0 + mod_ref[0, 4:5, :]) + mod_ref[0, 3:4, :]
        x1_ref[...] = x1
        h2_ref[...] = _pack_row(h2)
        lg_ref[...] = lax.dot_general(wrT_ref[...], h2.astype(BF16), NT_DIMS, preferred_element_type=F32)

    i = pl.program_id(0)

    @pl.when(i < n_first)
    def _():
        body(xp_ref, aap_ref, abp_ref)

    @pl.when(i >= n_first)
    def _():
        body(xs_ref, aas_ref, abs_ref)


def _outproj(xp, aap, abp, sp, xs, aas, abs_, ss, mod, woa, wob, g2, wrT):
    tp, ts = xp.shape[0], xs.shape[0]
    n_first, n_second = tp // TM, ts // TM
    nbp = tp // sp
    first = lambda i: (jnp.minimum(i, n_first - 1), 0)
    second = lambda i: (jnp.maximum(i - n_first, 0), 0)
    const = lambda i: (0, 0)

    def mod_map(i):
        b = jnp.where(i < n_first, i // (sp // TM), nbp + (i - n_first) // (ss // TM))
        return (b, 0, 0)

    t = tp + ts
    return pl.pallas_call(
        functools.partial(_outproj_kernel, n_first=n_first),
        grid=(n_first + n_second,),
        in_specs=[
            pl.BlockSpec((TM, D_MODEL), first),
            pl.BlockSpec((TM, A_V), first),
            pl.BlockSpec((TM, B_Q), first),
            pl.BlockSpec((TM, D_MODEL), second),
            pl.BlockSpec((TM, A_V), second),
            pl.BlockSpec((TM, B_Q), second),
            pl.BlockSpec((1, 6, D_MODEL), mod_map),
            pl.BlockSpec((A_V, D_MODEL), const),
            pl.BlockSpec((B_Q, D_MODEL), const),
            pl.BlockSpec((1, D_MODEL), const),
            pl.BlockSpec((N_EXPERTS, D_MODEL), const),
        ],
        out_specs=[
            pl.BlockSpec((TM, D_MODEL), lambda i: (i, 0)),
            pl.BlockSpec((TM, HALF), lambda i: (i, 0)),
            pl.BlockSpec((N_EXPERTS, TM), lambda i: (0, i)),
        ],
        out_shape=[
            jax.ShapeDtypeStruct((t, D_MODEL), F32),
            jax.ShapeDtypeStruct((t, HALF), I32),
            jax.ShapeDtypeStruct((N_EXPERTS, t), F32),
        ],
        compiler_params=_cparams(("parallel",), 48),
        name="outproj",
    )(xp, aap, abp, xs, aas, abs_, mod, woa, wob, g2, wrT)


def _route_kernel(lg_ref, rb_ref, idx_ref, gate_ref, rank_ref, cnt_ref):
    i = pl.program_id(0)

    @pl.when(i == 0)
    def _():
        cnt_ref[...] = jnp.zeros_like(cnt_ref)

    scores = jax.nn.sigmoid(lg_ref[...])
    tr = scores.shape[1]
    sel = scores + rb_ref[...]
    gsz = N_EXPERTS // N_GROUPS

    rows = []
    for g in range(N_GROUPS):
        blk = sel[g * gsz:(g + 1) * gsz, :]
        m1 = jnp.max(blk, axis=0, keepdims=True)
        eq = blk == m1
        n1 = jnp.sum(jnp.where(eq, 1.0, 0.0), axis=0, keepdims=True)
        m2 = jnp.max(jnp.where(eq, -jnp.inf, blk), axis=0, keepdims=True)
        rows.append(m1 + jnp.where(n1 >= 2.0, m1, m2))
    gs = jnp.concatenate(rows, axis=0)
    gi = lax.broadcasted_iota(I32, gs.shape, 0)
    beaten = jnp.zeros(gs.shape, F32)
    for gp in range(N_GROUPS):
        row = gs[gp:gp + 1, :]
        ahead = jnp.where(row > gs, 1.0, jnp.where(jnp.logical_and(row == gs, gp < gi), 1.0, 0.0))
        beaten = beaten + ahead
    keep = jnp.where(beaten < float(TOPK_GROUPS), 1.0, 0.0)
    emask = jnp.concatenate(
        [jnp.broadcast_to(keep[g:g + 1, :], (gsz, tr)) for g in range(N_GROUPS)], axis=0) > 0.5
    masked = jnp.where(emask, sel, -jnp.inf)

    ei = lax.broadcasted_iota(I32, masked.shape, 0).astype(F32)
    idxs, ws = [], []
    for _ in range(TOP_K):
        mk = jnp.max(masked, axis=0, keepdims=True)
        ik = jnp.min(jnp.where(masked == mk, ei, float(N_EXPERTS)), axis=0, keepdims=True)
        hit = ei == ik
        ws.append(jnp.sum(jnp.where(hit, scores, 0.0), axis=0, keepdims=True))
        masked = jnp.where(hit, -jnp.inf, masked)
        idxs.append(ik)
    wsum = ws[0]
    for w in ws[1:]:
        wsum = wsum + w
    gate_ref[...] = jnp.concatenate([w / wsum * ROUTE_SCALE for w in ws], axis=0)
    idx_ref[...] = jnp.concatenate(idxs, axis=0)

    chosen = jnp.logical_and(emask, masked == -jnp.inf)
    onehot = jnp.where(chosen, 1.0, 0.0)
    before = (lax.broadcasted_iota(I32, (tr, tr), 0) < lax.broadcasted_iota(I32, (tr, tr), 1))
    upper = jnp.where(before, 1.0, 0.0).astype(BF16)
    prior = jnp.dot(onehot.astype(BF16), upper, preferred_element_type=F32) + cnt_ref[:, 0:1]
    rank_ref[...] = jnp.concatenate(
        [jnp.sum(jnp.where(ei == ik, prior, 0.0), axis=0, keepdims=True) for ik in idxs], axis=0)
    cnt_ref[...] = cnt_ref[...] + jnp.sum(onehot, axis=1, keepdims=True)


def _route(logitsT, rbias):
    e, t = logitsT.shape
    row = pl.BlockSpec((TOP_K, TM), lambda i: (0, i))
    return pl.pallas_call(
        _route_kernel,
        grid=(t // TM,),
        in_specs=[pl.BlockSpec((e, TM), lambda i: (0, i)),
                  pl.BlockSpec((e, 1), lambda i: (0, 0))],
        out_specs=[row, row, row, pl.BlockSpec((e, LANES), lambda i: (0, 0))],
        out_shape=[jax.ShapeDtypeStruct((TOP_K, t), F32),
                   jax.ShapeDtypeStruct((TOP_K, t), F32),
                   jax.ShapeDtypeStruct((TOP_K, t), F32),
                   jax.ShapeDtypeStruct((e, LANES), F32)],
        compiler_params=_cparams(("arbitrary",), 32),
        name="route",
    )(logitsT, rbias)


def _dest_kernel(idx_ref, rank_ref, ps_ref, dest_ref):
    ei = lax.broadcasted_iota(I32, (N_EXPERTS, idx_ref.shape[1]), 0).astype(F32)
    ps = ps_ref[...]
    rows = []
    for k in range(TOP_K):
        hit = ei == idx_ref[k:k + 1, :]
        rows.append(jnp.sum(jnp.where(hit, ps, 0.0), axis=0, keepdims=True) + rank_ref[k:k + 1, :])
    dest_ref[...] = jnp.concatenate(rows, axis=0).astype(I32)


def _dest(idxT, rankT, pad_start):
    t = idxT.shape[1]
    row = pl.BlockSpec((TOP_K, TM), lambda i: (0, i))
    return pl.pallas_call(
        _dest_kernel,
        grid=(t // TM,),
        in_specs=[row, row, pl.BlockSpec((N_EXPERTS, 1), lambda i: (0, 0))],
        out_specs=row,
        out_shape=jax.ShapeDtypeStruct((TOP_K, t), I32),
        compiler_params=_cparams(("parallel",), 24),
        name="dest",
    )(idxT, rankT, pad_start)


def _sc_workers(mesh, n_tokens):
    n_workers = mesh.num_cores * mesh.num_subcores
    per_worker = n_tokens // n_workers
    assert per_worker * n_workers == n_tokens and per_worker % SC_CHUNK == 0
    return per_worker


def _sc_scratch(width):
    return ([pltpu.VMEM((SC_CHUNK,), I32)] * TOP_K
            + [pltpu.VMEM((SC_CHUNK, width), I32), pltpu.SemaphoreType.DMA])


def _sc_dispatch(rows, destT, n_slots):
    t, width = rows.shape
    mesh = plsc.VectorSubcoreMesh(core_axis_name="c", subcore_axis_name="s")
    per_worker = _sc_workers(mesh, t)

    @functools.partial(pl.kernel, mesh=mesh, out_type=jax.ShapeDtypeStruct((n_slots, width), I32),
                       scratch_types=_sc_scratch(width))
    def scatter_rows(rows_hbm, dest_hbm, xs_hbm, *scratch):
        idx, buf, sem = scratch[:TOP_K], scratch[TOP_K], scratch[TOP_K + 1]
        base = (lax.axis_index("s") * mesh.num_cores + lax.axis_index("c")) * per_worker

        @pl.loop(0, per_worker // SC_CHUNK)
        def _(i):
            t0 = pl.multiple_of(base + i * SC_CHUNK, SC_CHUNK)
            for k in range(TOP_K):
                pltpu.sync_copy(dest_hbm.at[k, pl.ds(t0, SC_CHUNK)], idx[k])
            pltpu.sync_copy(rows_hbm.at[pl.ds(t0, SC_CHUNK)], buf)
            copies = [pltpu.async_copy(buf, xs_hbm.at[idx[k]], sem) for k in range(TOP_K)]
            for c in copies:
                c.wait()

    return scatter_rows(rows, destT)


def _sc_combine(y, destT):
    width = y.shape[1]
    t = destT.shape[1]
    mesh = plsc.VectorSubcoreMesh(core_axis_name="c", subcore_axis_name="s")
    per_worker = _sc_workers(mesh, t)

    @functools.partial(pl.kernel, mesh=mesh, out_type=jax.ShapeDtypeStruct((TOP_K, t, width), I32),
                       scratch_types=_sc_scratch(width))
    def gather_rows(y_hbm, dest_hbm, out_hbm, *scratch):
        idx, buf, sem = scratch[:TOP_K], scratch[TOP_K], scratch[TOP_K + 1]
        base = (lax.axis_index("s") * mesh.num_cores + lax.axis_index("c")) * per_worker

        @pl.loop(0, per_worker // SC_CHUNK)
        def _(i):
            t0 = pl.multiple_of(base + i * SC_CHUNK, SC_CHUNK)
            for k in range(TOP_K):
                pltpu.sync_copy(dest_hbm.at[k, pl.ds(t0, SC_CHUNK)], idx[k])
            for k in range(TOP_K):
                pltpu.async_copy(y_hbm.at[idx[k]], buf, sem).wait()
                pltpu.sync_copy(buf, out_hbm.at[k, pl.ds(t0, SC_CHUNK)])

    return gather_rows(y, destT)


def _ffn_kernel(be_ref, bv_ref, nu_ref, xs_ref, wg_ref, wu_ref, wd_ref, y_ref, wgu_sc, wd_sc):
    b = pl.program_id(0)
    changed = jnp.logical_or(b == 0, be_ref[b] != be_ref[jnp.maximum(b - 1, 0)])

    @pl.when(changed)
    def _():
        wgu_sc[:, 0:D_EXPERT] = wg_ref[0].astype(BF16)
        wgu_sc[:, D_EXPERT:2 * D_EXPERT] = wu_ref[0].astype(BF16)
        wd_sc[...] = wd_ref[0].astype(BF16)

    @pl.when(b < nu_ref[0])
    def _():
        row = lax.broadcasted_iota(I32, (BM, 1), 0)
        hi, lo = _unpack_halves(jnp.where(row < bv_ref[b], xs_ref[...], 0))
        x = jnp.concatenate([hi, lo], axis=1).astype(BF16)
        gu = jnp.dot(x, wgu_sc[...], preferred_element_type=F32)
        a = (_silu(gu[:, 0:D_EXPERT]) * gu[:, D_EXPERT:2 * D_EXPERT]).astype(BF16)
        y_ref[...] = _pack_row(jnp.dot(a, wd_sc[...], preferred_element_type=F32))

    @pl.when(b >= nu_ref[0])
    def _():
        y_ref[...] = jnp.zeros_like(y_ref)


def _ffn(blk_e, blk_valid, n_used, xs, w_eg, w_eu, w_ed):
    n_slots = xs.shape[0]
    nblocks = n_slots // BM
    return pl.pallas_call(
        _ffn_kernel,
        grid_spec=pltpu.PrefetchScalarGridSpec(
            num_scalar_prefetch=3,
            grid=(nblocks,),
            in_specs=[
                pl.BlockSpec((BM, HALF), lambda b, be, bv, nu: (jnp.minimum(b, nu[0] - 1), 0)),
                pl.BlockSpec((1, D_MODEL, D_EXPERT), lambda b, be, bv, nu: (be[b], 0, 0)),
                pl.BlockSpec((1, D_MODEL, D_EXPERT), lambda b, be, bv, nu: (be[b], 0, 0)),
                pl.BlockSpec((1, D_EXPERT, D_MODEL), lambda b, be, bv, nu: (be[b], 0, 0)),
            ],
            out_specs=pl.BlockSpec((BM, HALF), lambda b, be, bv, nu: (b, 0)),
            scratch_shapes=[pltpu.VMEM((D_MODEL, 2 * D_EXPERT), BF16),
                            pltpu.VMEM((D_EXPERT, D_MODEL), BF16)],
        ),
        out_shape=jax.ShapeDtypeStruct((n_slots, HALF), I32),
        compiler_params=_cparams(("arbitrary",), 32),
        name="ffn",
    )(blk_e, blk_valid, n_used, xs, w_eg, w_eu, w_ed)


def _final_kernel(x1_ref, h2_ref, yk_ref, gate_ref, mod_ref, wsgu_ref, wsd_ref, gf_ref, o_ref):
    hi, lo = _unpack_halves(h2_ref[...])
    hb = jnp.concatenate([hi, lo], axis=1).astype(BF16)
    gu = jnp.dot(hb, wsgu_ref[...], preferred_element_type=F32)
    a = (_silu(gu[:, 0:D_SHARED]) * gu[:, D_SHARED:2 * D_SHARED]).astype(BF16)
    shared = jnp.dot(a, wsd_ref[...], preferred_element_type=F32)

    gate = gate_ref[...]
    acc_hi = acc_lo = None
    for k in range(TOP_K):
        hi, lo = _unpack_halves(yk_ref[k])
        g = gate[:, k:k + 1]
        acc_hi = g * hi if acc_hi is None else acc_hi + g * hi
        acc_lo = g * lo if acc_lo is None else acc_lo + g * lo
    routed = jnp.concatenate([acc_hi, acc_lo], axis=1)
    x2 = x1_ref[...] + mod_ref[0, 5:6, :] * (routed + shared)
    o_ref[...] = _rms(x2, -1) * gf_ref[...]


def _final(x1, h2p, yk, gates, mod, wsgu, wsd, gf, tok_off, nb, s, b_off):
    n_tiles = nb * s // TF
    t_off = tok_off // TF
    tpb = s // TF
    const = lambda i: (0, 0)
    return pl.pallas_call(
        _final_kernel,
        grid=(n_tiles,),
        in_specs=[
            pl.BlockSpec((TF, D_MODEL), lambda i: (t_off + i, 0)),
            pl.BlockSpec((TF, HALF), lambda i: (t_off + i, 0)),
            pl.BlockSpec((TOP_K, TF, HALF), lambda i: (0, t_off + i, 0)),
            pl.BlockSpec((TF, TOP_K), lambda i: (t_off + i, 0)),
            pl.BlockSpec((1, 6, D_MODEL), lambda i: (b_off + i // tpb, 0, 0)),
            pl.BlockSpec((D_MODEL, 2 * D_SHARED), const),
            pl.BlockSpec((D_SHARED, D_MODEL), const),
            pl.BlockSpec((1, D_MODEL), const),
        ],
        out_specs=pl.BlockSpec((TF, D_MODEL), lambda i: (i, 0)),
        out_shape=jax.ShapeDtypeStruct((nb * s, D_MODEL), F32),
        compiler_params=_cparams(("parallel",), 40),
        name="final",
    )(x1, h2p, yk, gates, mod, wsgu, wsd, gf)


def _t5_bucket(rel):
    nb = N_BUCKETS // 2
    max_exact = nb // 2
    ret = jnp.where(rel > 0, nb, 0)
    n = jnp.abs(rel)
    nf = jnp.maximum(n, 1).astype(F32)
    large = max_exact + (jnp.log(nf / max_exact) / math.log(MAX_DISTANCE / max_exact)
                         * (nb - max_exact)).astype(I32)
    large = jnp.minimum(large, nb - 1)
    return ret + jnp.where(n < max_exact, n, large)


def _bias_tables(rel_bias):
    n_far = np.arange(TK + 1, 1 << 20, dtype=np.float64)
    assert np.all(N_BUCKETS // 4 + np.log(n_far / (N_BUCKETS // 4)) / math.log(MAX_DISTANCE / (N_BUCKETS // 4))
                  * (N_BUCKETS // 4) >= N_BUCKETS // 2), "relative-position buckets must saturate beyond one chunk"
    kk = jnp.arange(TK, dtype=I32)[:, None]
    qq = jnp.arange(TQ, dtype=I32)[None, :]
    rel = jnp.stack([(dd - 2) * TK + kk - qq for dd in range(5)])
    onehot = jax.nn.one_hot(_t5_bucket(rel), N_BUCKETS, dtype=F32)
    band = jnp.einsum("dkqb,bh->hdkq", onehot, rel_bias.astype(F32), precision=lax.Precision.HIGHEST)
    return band * LOG2E


def _rope_tables(s):
    rows = s // GRID_W
    row_id = jnp.repeat(jnp.arange(rows, dtype=F32), GRID_W)
    col_id = jnp.tile(jnp.arange(GRID_W, dtype=F32), rows)
    half = HD_B // 2
    inv = ROPE_THETA ** (-jnp.arange(0, half, 2, dtype=F32) / half)
    ang_r = row_id[:, None] * inv[None, :]
    ang_c = col_id[:, None] * inv[None, :]
    cr, sr, cc, sc = jnp.cos(ang_r), jnp.sin(ang_r), jnp.cos(ang_c), jnp.sin(ang_c)
    c64 = jnp.concatenate([cr, cr, cc, cc], axis=1)
    s64 = jnp.concatenate([-sr, sr, -sc, sc], axis=1)
    return c64.T, s64.T, jnp.concatenate([c64, c64], axis=1), jnp.concatenate([s64, s64], axis=1)


def kernel(x_prompt, x_sample, c_prompt, c_sample, rel_bias, w_ada, b_ada, g_norm1, w_in, lambda_q1, lambda_k1, lambda_q2, lambda_k2, g_subln, g_qnorm, g_knorm, w_out, g_norm2, w_router, router_bias, w_exp_gate, w_exp_up, w_exp_down, w_sh_gate, w_sh_up, w_sh_down, g_final):
    bp, sp, d = x_prompt.shape
    bs, ss, _ = x_sample.shape
    assert d == D_MODEL and sp % TM == 0 and ss % TM == 0 and TM % TK == 0 and TQ == TK
    assert sp % (KSUB * TK) == 0 and ss % (KSUB * TK) == 0
    tp, ts = bp * sp, bs * ss
    t = tp + ts
    assert tp % TF == 0 and ts % TF == 0

    mod = _ada(jnp.concatenate([c_prompt, c_sample], axis=0), w_ada[LAYER], b_ada[LAYER])
    mod = mod.reshape(bp + bs, 6, d)

    w = w_in[LAYER]
    o1, o2, o3 = A_Q, A_Q + A_K, A_Q + A_K + A_V
    o4, o5 = o3 + B_Q, o3 + B_Q + B_K
    wqT = jnp.concatenate([w[:, :o1], w[:, o3:o4]], axis=1).T.astype(BF16)
    wk = jnp.concatenate([w[:, o1:o2], w[:, o4:o5]], axis=1).astype(BF16)
    wvT = jnp.concatenate([w[:, o2:o3], w[:, o5:]], axis=1).T.astype(BF16)
    g1 = g_norm1[LAYER].reshape(1, d)
    gqT = jnp.broadcast_to(g_qnorm[LAYER][:, None], (HD_B, TM))
    gk = jnp.tile(g_knorm[LAYER], KV_B).reshape(1, LANES)
    band = _bias_tables(rel_bias)
    lam_rows = jnp.stack([lambda_q1[LAYER], lambda_k1[LAYER], lambda_q2[LAYER], lambda_k2[LAYER]])
    gsub = jnp.broadcast_to(g_subln[LAYER][:, None], (2 * HD_A, TQ))
    wo = w_out[LAYER].astype(BF16)
    woa, wob = wo[:A_V], wo[A_V:]

    def mixers(x, nb, s, b_off):
        cT, sT, cN, sN = _rope_tables(s)
        qT, k2, vT4 = _inproj(x.reshape(nb * s, d), nb, s, b_off, mod, g1, wqT, wk, wvT, gqT, gk, cT, sT, cN, sN)
        k3 = k2.reshape(nb, s, K_COLS)
        oa = _attention(qT, k3, vT4, True, (band, lam_rows, gsub))
        ob = _attention(qT, k3, vT4, False, ())
        return oa.reshape(nb * s, A_V), ob.reshape(nb * s, B_Q)

    aap, abp = mixers(x_prompt, bp, sp, 0)
    aas, abs_ = mixers(x_sample, bs, ss, bp)

    x1, h2p, logitsT = _outproj(
        x_prompt.reshape(tp, d), aap, abp, sp, x_sample.reshape(ts, d), aas, abs_, ss, mod, woa, wob,
        g_norm2[LAYER].reshape(1, d), w_router[LAYER].T.astype(BF16))

    idxT, gateT, rankT, cnt = _route(logitsT, router_bias[LAYER].reshape(N_EXPERTS, 1))

    counts = cnt[:, 0].astype(I32)
    nblk = (counts + BM - 1) // BM
    blk_end = jnp.cumsum(nblk)
    blk_start = blk_end - nblk
    n_blocks = t * TOP_K // BM + N_EXPERTS
    n_slots = n_blocks * BM
    n_used = blk_end[-1:]
    bidx = jnp.arange(n_blocks, dtype=I32)
    blk_e = jnp.minimum(jnp.searchsorted(blk_end, bidx, side="right"), N_EXPERTS - 1).astype(I32)
    blk_e = jnp.where(bidx < n_used[0], blk_e, blk_e[jnp.maximum(n_used[0] - 1, 0)])
    pad_start = (blk_start * BM).astype(F32).reshape(N_EXPERTS, 1)
    blk_valid = jnp.clip(counts[blk_e] - (bidx - blk_start[blk_e]) * BM, 0, BM).astype(I32)

    destT = _dest(idxT, rankT, pad_start)
    gates = gateT.T

    n_used = n_used.astype(I32)
    xs = _sc_dispatch(h2p, destT, n_slots)
    y = _ffn(blk_e, blk_valid, n_used, xs, w_exp_gate[LAYER], w_exp_up[LAYER], w_exp_down[LAYER])
    yk = _sc_combine(y, destT)

    wsgu = jnp.concatenate([w_sh_gate[LAYER], w_sh_up[LAYER]], axis=1).astype(BF16)
    wsd = w_sh_down[LAYER].astype(BF16)
    gf = g_final.reshape(1, d)
    y_prompt = _final(x1, h2p, yk, gates, mod, wsgu, wsd, gf, 0, bp, sp, 0)
    y_sample = _final(x1, h2p, yk, gates, mod, wsgu, wsd, gf, tp, bs, ss, bp)
    return (y_prompt.reshape(bp, sp, d), y_sample.reshape(bs, ss, d))
```

```python
import functools
import math

import jax
import jax.numpy as jnp
import numpy as np
from jax import lax
from jax.experimental import pallas as pl
from jax.experimental.pallas import tpu as pltpu
from jax.experimental.pallas import tpu_sc as plsc

F32 = jnp.float32
BF16 = jnp.bfloat16
I32 = jnp.int32

D_MODEL = 1024
GRID_W = 64
HA = 4
HD_A = 64
HB = 8
KV_B = 2
HD_B = 64
ROPE_THETA = 10000.0
N_BUCKETS = 32
MAX_DISTANCE = 128
A_Q = HA * 2 * HD_A
A_K = HA * 2 * HD_A
A_V = HA * 2 * HD_A
B_Q = HB * HD_B
B_K = KV_B * HD_B
B_V = KV_B * HD_B
N_EXPERTS = 256
TOP_K = 8
N_GROUPS = 8
TOPK_GROUPS = 4
D_EXPERT = 256
D_SHARED = 256
ROUTE_SCALE = 2.5
EPS = 1e-6
LAYER = 0
LAM_INIT = 0.8 - 0.6 * math.exp(-0.3 * LAYER)

LANES = 128
SUBLANES = 8
V7X_VMEM_BYTES = 64 * 1024 * 1024

TM = 512
TQ = 256
TK = 256
KSUB = 4
MAP_ROWS = 2 * HD_A
ONES_ROWS = 16
N_MAPS = 2 * HA + HB
K_COLS = A_K + B_K
V_ROWS = A_V + B_V
BM = 512
TF = 256
HALF = D_MODEL // 2
SC_CHUNK = 128
LOG2E = math.log2(math.e)
Q_SCALE = (HD_A ** -0.5) * LOG2E
NT_DIMS = (((1,), (1,)), ((), ()))
NEG_BIG = -1e30


def _cparams(semantics, vmem_mb):
    assert vmem_mb * 1024 * 1024 <= V7X_VMEM_BYTES
    return pltpu.CompilerParams(dimension_semantics=semantics, vmem_limit_bytes=vmem_mb * 1024 * 1024)


def _rms(x, axis):
    return x * lax.rsqrt(jnp.mean(x * x, axis=axis, keepdims=True) + EPS)


def _silu(x):
    return x * jax.nn.sigmoid(x)


def _pack_row(x):
    hi = pltpu.bitcast(x[:, :HALF].astype(BF16).astype(F32), I32)
    lo = pltpu.bitcast(x[:, HALF:].astype(BF16).astype(F32), I32)
    return hi | lax.shift_right_logical(lo, 16)


def _unpack_halves(p):
    hi = pltpu.bitcast(p & jnp.int32(-65536), F32)
    lo = pltpu.bitcast(lax.shift_left(p, 16), F32)
    return hi, lo


def _ada_kernel(c_ref, w_ref, b_ref, o_ref):
    a = _silu(c_ref[...])
    o_ref[...] = jnp.dot(a, w_ref[...], preferred_element_type=F32) + b_ref[...]


def _ada(c_all, w, b):
    nb, d = c_all.shape
    n = w.shape[1]
    tn = 768
    return pl.pallas_call(
        _ada_kernel,
        grid=(n // tn,),
        in_specs=[pl.BlockSpec((nb, d), lambda j: (0, 0)),
                  pl.BlockSpec((d, tn), lambda j: (0, j)),
                  pl.BlockSpec((1, tn), lambda j: (0, j))],
        out_specs=pl.BlockSpec((nb, tn), lambda j: (0, j)),
        out_shape=jax.ShapeDtypeStruct((nb, n), F32),
        compiler_params=_cparams(("parallel",), 24),
        name="ada",
    )(c_all, w, b.reshape(1, n))


def _inproj_kernel(x_ref, mod_ref, g1_ref, wqT_ref, wk_ref, wvT_ref, gqT_ref, gk_ref,
                   cT_ref, sT_ref, cN_ref, sN_ref, qT_ref, k_ref, vT_ref):
    x = x_ref[...]
    tm = x.shape[0]
    shift = mod_ref[0, 0:1, :]
    scale = mod_ref[0, 1:2, :]
    h = (_rms(x, -1) * g1_ref[...]) * (1.0 + scale) + shift
    hb = h.astype(BF16)
    qT = lax.dot_general(wqT_ref[...], hb, NT_DIMS, preferred_element_type=F32)
    kn = jnp.dot(hb, wk_ref[...], preferred_element_type=F32)
    vT = lax.dot_general(wvT_ref[...], hb, NT_DIMS, preferred_element_type=F32)

    vTb = vT.astype(BF16)
    for c in range(tm // TK):
        vT_ref[0, c] = vTb[:, c * TK:(c + 1) * TK]

    zeros = jnp.zeros((HD_A, tm), BF16)

    def put_map(m, q, upper):
        r0 = m * MAP_ROWS
        if upper:
            qT_ref[0, r0:r0 + HD_A, :] = zeros
            qT_ref[0, r0 + HD_A:r0 + MAP_ROWS, :] = q
        else:
            qT_ref[0, r0:r0 + HD_A, :] = q
            qT_ref[0, r0 + HD_A:r0 + MAP_ROWS, :] = zeros

    for m in range(2 * HA):
        q = (qT[m * HD_A:(m + 1) * HD_A, :] * Q_SCALE).astype(BF16)
        put_map(m, q, upper=(m % 2 == 1))

    cT = cT_ref[...]
    sT = sT_ref[...]
    gq = gqT_ref[...]
    for j in range(HB):
        xh = qT[A_Q + j * HD_B:A_Q + (j + 1) * HD_B, :]
        y = _rms(xh, 0) * gq
        yp = jnp.concatenate([y[16:32], y[0:16], y[48:64], y[32:48]], axis=0)
        r = (y * cT + yp * sT) * Q_SCALE
        put_map(2 * HA + j, r.astype(BF16), upper=(j // (HB // KV_B) == 1))

    kb = kn[:, A_K:K_COLS]
    lane = lax.broadcasted_iota(I32, kb.shape, 1)
    first = lane < HD_B
    x2 = kb * kb
    s0 = jnp.sum(jnp.where(first, x2, 0.0), axis=-1, keepdims=True)
    s1 = jnp.sum(jnp.where(first, 0.0, x2), axis=-1, keepdims=True)
    ms = jnp.where(first, s0, s1) * (1.0 / HD_B)
    y = kb * lax.rsqrt(ms + EPS) * gk_ref[...]
    partner = jnp.where((lane % 32) < 16, pltpu.roll(y, LANES - 16, 1), pltpu.roll(y, 16, 1))
    r = y * cN_ref[...] + partner * sN_ref[...]
    k_ref[:, 0:A_K] = kn[:, 0:A_K].astype(BF16)
    k_ref[:, A_K:K_COLS] = r.astype(BF16)


def _inproj(x2d, nb, s, b_off, mod, g1, wqT, wk, wvT, gqT, gk, cT, sT, cN, sN):
    nps = s // TM
    const = lambda i: (0, 0)
    return pl.pallas_call(
        _inproj_kernel,
        grid=(nb * nps,),
        in_specs=[
            pl.BlockSpec((TM, D_MODEL), lambda i: (i, 0)),
            pl.BlockSpec((1, 6, D_MODEL), lambda i: (b_off + i // nps, 0, 0)),
            pl.BlockSpec((1, D_MODEL), const),
            pl.BlockSpec((A_Q + B_Q, D_MODEL), const),
            pl.BlockSpec((D_MODEL, K_COLS), const),
            pl.BlockSpec((V_ROWS, D_MODEL), const),
            pl.BlockSpec((HD_B, TM), const),
            pl.BlockSpec((1, LANES), const),
            pl.BlockSpec((HD_B, TM), lambda i: (0, i % nps)),
            pl.BlockSpec((HD_B, TM), lambda i: (0, i % nps)),
            pl.BlockSpec((TM, LANES), lambda i: (i % nps, 0)),
            pl.BlockSpec((TM, LANES), lambda i: (i % nps, 0)),
        ],
        out_specs=[
            pl.BlockSpec((1, N_MAPS * MAP_ROWS, TM), lambda i: (i // nps, 0, i % nps)),
            pl.BlockSpec((TM, K_COLS), lambda i: (i, 0)),
            pl.BlockSpec((1, TM // TK, V_ROWS, TK), lambda i: (i // nps, i % nps, 0, 0)),
        ],
        out_shape=[
            jax.ShapeDtypeStruct((nb, N_MAPS * MAP_ROWS, s), BF16),
            jax.ShapeDtypeStruct((nb * s, K_COLS), BF16),
            jax.ShapeDtypeStruct((nb, s // TK, V_ROWS, TK), BF16),
        ],
        compiler_params=_cparams(("parallel",), 48),
        name="inproj",
    )(x2d, mod, g1, wqT, wk, wvT, gqT, gk, cT, sT, cN, sN)


def _attn_kernel(*refs, is_diff, n_big, ksub):
    if is_diff:
        qT_ref, k_ref, vT_ref, band_ref, lam_ref, gsub_ref, o_ref, s_a, s_b = refs
    else:
        qT_ref, k_ref, vT_ref, o_ref, s_a, s_b = refs
    qi = pl.program_id(2)
    tq = qT_ref.shape[2]
    dv = vT_ref.shape[2]
    q2 = jnp.concatenate([qT_ref[0, 0:MAP_ROWS, :], qT_ref[0, MAP_ROWS:2 * MAP_ROWS, :]], axis=1)

    def score_stage(g, s_buf):
        cm = None
        for j in range(ksub):
            c = g * ksub + j
            kc = k_ref[0, pl.ds(pl.multiple_of(c * TK, TK), TK), :]
            sj = jnp.dot(kc, q2, preferred_element_type=F32)
            if is_diff:
                bias = band_ref[0, jnp.clip(c - qi + 2, 0, 4)]
                sj = sj + jnp.concatenate([bias, bias], axis=1)
            s_buf[j * TK:(j + 1) * TK, :] = sj
            cj = jnp.max(sj, axis=0, keepdims=True)
            cm = cj if cm is None else jnp.maximum(cm, cj)
        return cm

    ones = jnp.ones((ONES_ROWS, TK), BF16)

    def softmax_stage(g, s_buf, cm, carry):
        m, acc = carry
        m_new = jnp.maximum(m, cm)
        acc = jnp.exp2(m - m_new) * acc
        for j in range(ksub):
            d = s_buf[j * TK:(j + 1) * TK, :] - m_new
            vT = vT_ref[0, g * ksub + j]
            if is_diff:
                p = jnp.exp2(d.astype(BF16))
                acc = acc + jnp.dot(jnp.concatenate([vT, ones], axis=0), p, preferred_element_type=F32)
            else:
                p = jnp.exp2(d)
                pv = jnp.dot(vT, p.astype(BF16), preferred_element_type=F32)
                lsum = jnp.broadcast_to(jnp.sum(p, axis=0, keepdims=True), (ONES_ROWS, p.shape[1]))
                acc = acc + jnp.concatenate([pv, lsum], axis=0)
        return m_new, acc

    def pair(i, state):
        carry, cm_a = state
        g = 2 * i
        cm_b = score_stage(g + 1, s_b)
        carry = softmax_stage(g, s_a, cm_a, carry)
        cm_a = score_stage(g + 2, s_a)
        carry = softmax_stage(g + 1, s_b, cm_b, carry)
        return carry, cm_a

    init = (jnp.full((1, 2 * tq), NEG_BIG, F32), jnp.zeros((dv + ONES_ROWS, 2 * tq), F32))
    carry, cm_a = lax.fori_loop(0, n_big // 2 - 1, pair, (init, score_stage(0, s_a)))
    cm_b = score_stage(n_big - 1, s_b)
    carry = softmax_stage(n_big - 2, s_a, cm_a, carry)
    _, acc = softmax_stage(n_big - 1, s_b, cm_b, carry)
    o2 = acc[0:dv, :] * (1.0 / acc[dv:dv + 1, :])
    outs = [o2[:, 0:tq], o2[:, tq:2 * tq]]

    if is_diff:
        lv = lam_ref[...]
        lam = (jnp.exp(jnp.sum(lv[0:1] * lv[1:2], axis=-1, keepdims=True))
               - jnp.exp(jnp.sum(lv[2:3] * lv[3:4], axis=-1, keepdims=True)) + LAM_INIT)
        o = outs[0] - lam * outs[1]
        o = _rms(o, 0) * gsub_ref[...] * (1.0 - LAM_INIT)
    else:
        o = jnp.concatenate(outs, axis=0)
    o_ref[0] = o.T.astype(BF16)


def _attention(qT, k3, vT4, is_diff, extra):
    nb, s, _ = k3.shape
    nk = s // TK
    nq = s // TQ
    if is_diff:
        q_map = lambda b, u, qi: (b, u, qi)
        k_map = lambda b, u, qi: (b, 0, u)
        v_spec = pl.BlockSpec((1, nk, 2 * HD_A, TK), lambda b, u, qi: (b, 0, u, 0))
        extra_specs = [
            pl.BlockSpec((1, 5, TK, TQ), lambda b, u, qi: (u, 0, 0, 0)),
            pl.BlockSpec((4, HD_A), lambda b, u, qi: (0, 0)),
            pl.BlockSpec((2 * HD_A, TQ), lambda b, u, qi: (0, 0)),
        ]
        n_units = HA
    else:
        first = (2 * HA * MAP_ROWS) // (2 * MAP_ROWS)
        q_map = lambda b, u, qi: (b, first + u, qi)
        k_map = lambda b, u, qi: (b, 0, A_K // LANES)
        v0 = A_V // HD_B
        per_kv = (HB // KV_B) // 2
        v_spec = pl.BlockSpec((1, nk, HD_B, TK), lambda b, u, qi: (b, 0, v0 + u // per_kv, 0))
        extra_specs = []
        n_units = HB // 2
    ksub = KSUB if nk >= 4 * KSUB else KSUB // 2
    n_big = nk // ksub
    assert n_big * ksub == nk and n_big % 2 == 0
    return pl.pallas_call(
        functools.partial(_attn_kernel, is_diff=is_diff, n_big=n_big, ksub=ksub),
        grid=(nb, n_units, nq),
        in_specs=[pl.BlockSpec((1, 2 * MAP_ROWS, TQ), q_map),
                  pl.BlockSpec((1, s, LANES), k_map),
                  v_spec] + extra_specs,
        out_specs=pl.BlockSpec((1, TQ, LANES), lambda b, u, qi: (b, qi, u)),
        out_shape=jax.ShapeDtypeStruct((nb, s, n_units * LANES), BF16),
        scratch_shapes=[pltpu.VMEM((ksub * TK, 2 * TQ), F32)] * 2,
        compiler_params=_cparams(("parallel", "parallel", "parallel"), 40),
        name="attn_diff" if is_diff else "attn_gqa",
    )(qT, k3, vT4, *extra)


def _outproj_kernel(xp_ref, aap_ref, abp_ref, xs_ref, aas_ref, abs_ref, mod_ref, woa_ref, wob_ref,
                    g2_ref, wrT_ref, x1_ref, h2_ref, lg_ref, *, n_first):
    def body(x_ref, aa_ref, ab_ref):
        att = (jnp.dot(aa_ref[...], woa_ref[...], preferred_element_type=F32)
               + jnp.dot(ab_ref[...], wob_ref[...], preferred_element_type=F32))
        x1 = x_ref[...] + mod_ref[0, 2:3, :] * att
        h2 = (_rms(x1, -1) * g2_ref[...]) * (1.0 + mod_ref[0, 4:5, :]) + mod_ref[0, 3:4, :]
        x1_ref[...] = x1
        h2_ref[...] = _pack_row(h2)
        lg_ref[...] = lax.dot_general(wrT_ref[...], h2.astype(BF16), NT_DIMS, preferred_element_type=F32)

    i = pl.program_id(0)

    @pl.when(i < n_first)
    def _():
        body(xp_ref, aap_ref, abp_ref)

    @pl.when(i >= n_first)
    def _():
        body(xs_ref, aas_ref, abs_ref)


def _outproj(xp, aap, abp, sp, xs, aas, abs_, ss, mod, woa, wob, g2, wrT):
    tp, ts = xp.shape[0], xs.shape[0]
    n_first, n_second = tp // TM, ts // TM
    nbp = tp // sp
    first = lambda i: (jnp.minimum(i, n_first - 1), 0)
    second = lambda i: (jnp.maximum(i - n_first, 0), 0)
    const = lambda i: (0, 0)

    def mod_map(i):
        b = jnp.where(i < n_first, i // (sp // TM), nbp + (i - n_first) // (ss // TM))
        return (b, 0, 0)

    t = tp + ts
    return pl.pallas_call(
        functools.partial(_outproj_kernel, n_first=n_first),
        grid=(n_first + n_second,),
        in_specs=[
            pl.BlockSpec((TM, D_MODEL), first),
            pl.BlockSpec((TM, A_V), first),
            pl.BlockSpec((TM, B_Q), first),
            pl.BlockSpec((TM, D_MODEL), second),
            pl.BlockSpec((TM, A_V), second),
            pl.BlockSpec((TM, B_Q), second),
            pl.BlockSpec((1, 6, D_MODEL), mod_map),
            pl.BlockSpec((A_V, D_MODEL), const),
            pl.BlockSpec((B_Q, D_MODEL), const),
            pl.BlockSpec((1, D_MODEL), const),
            pl.BlockSpec((N_EXPERTS, D_MODEL), const),
        ],
        out_specs=[
            pl.BlockSpec((TM, D_MODEL), lambda i: (i, 0)),
            pl.BlockSpec((TM, HALF), lambda i: (i, 0)),
            pl.BlockSpec((N_EXPERTS, TM), lambda i: (0, i)),
        ],
        out_shape=[
            jax.ShapeDtypeStruct((t, D_MODEL), F32),
            jax.ShapeDtypeStruct((t, HALF), I32),
            jax.ShapeDtypeStruct((N_EXPERTS, t), F32),
        ],
        compiler_params=_cparams(("parallel",), 48),
        name="outproj",
    )(xp, aap, abp, xs, aas, abs_, mod, woa, wob, g2, wrT)


def _route_kernel(lg_ref, rb_ref, idx_ref, gate_ref, rank_ref, cnt_ref):
    i = pl.program_id(0)

    @pl.when(i == 0)
    def _():
        cnt_ref[...] = jnp.zeros_like(cnt_ref)

    scores = jax.nn.sigmoid(lg_ref[...])
    tr = scores.shape[1]
    sel = scores + rb_ref[...]
    gsz = N_EXPERTS // N_GROUPS

    rows = []
    for g in range(N_GROUPS):
        blk = sel[g * gsz:(g + 1) * gsz, :]
        m1 = jnp.max(blk, axis=0, keepdims=True)
        eq = blk == m1
        n1 = jnp.sum(jnp.where(eq, 1.0, 0.0), axis=0, keepdims=True)
        m2 = jnp.max(jnp.where(eq, -jnp.inf, blk), axis=0, keepdims=True)
        rows.append(m1 + jnp.where(n1 >= 2.0, m1, m2))
    gs = jnp.concatenate(rows, axis=0)
    gi = lax.broadcasted_iota(I32, gs.shape, 0)
    beaten = jnp.zeros(gs.shape, F32)
    for gp in range(N_GROUPS):
        row = gs[gp:gp + 1, :]
        ahead = jnp.where(row > gs, 1.0, jnp.where(jnp.logical_and(row == gs, gp < gi), 1.0, 0.0))
        beaten = beaten + ahead
    keep = jnp.where(beaten < float(TOPK_GROUPS), 1.0, 0.0)
    emask = jnp.concatenate(
        [jnp.broadcast_to(keep[g:g + 1, :], (gsz, tr)) for g in range(N_GROUPS)], axis=0) > 0.5
    masked = jnp.where(emask, sel, -jnp.inf)

    ei = lax.broadcasted_iota(I32, masked.shape, 0).astype(F32)
    idxs, ws = [], []
    for _ in range(TOP_K):
        mk = jnp.max(masked, axis=0, keepdims=True)
        ik = jnp.min(jnp.where(masked == mk, ei, float(N_EXPERTS)), axis=0, keepdims=True)
        hit = ei == ik
        ws.append(jnp.sum(jnp.where(hit, scores, 0.0), axis=0, keepdims=True))
        masked = jnp.where(hit, -jnp.inf, masked)
        idxs.append(ik)
    wsum = ws[0]
    for w in ws[1:]:
        wsum = wsum + w
    gate_ref[...] = jnp.concatenate([w / wsum * ROUTE_SCALE for w in ws], axis=0)
    idx_ref[...] = jnp.concatenate(idxs, axis=0)

    chosen = jnp.logical_and(emask, masked == -jnp.inf)
    onehot = jnp.where(chosen, 1.0, 0.0)
    before = (lax.broadcasted_iota(I32, (tr, tr), 0) < lax.broadcasted_iota(I32, (tr, tr), 1))
    upper = jnp.where(before, 1.0, 0.0).astype(BF16)
    prior = jnp.dot(onehot.astype(BF16), upper, preferred_element_type=F32) + cnt_ref[:, 0:1]
    rank_ref[...] = jnp.concatenate(
        [jnp.sum(jnp.where(ei == ik, prior, 0.0), axis=0, keepdims=True) for ik in idxs], axis=0)
    cnt_ref[...] = cnt_ref[...] + jnp.sum(onehot, axis=1, keepdims=True)


def _route(logitsT, rbias):
    e, t = logitsT.shape
    row = pl.BlockSpec((TOP_K, TM), lambda i: (0, i))
    return pl.pallas_call(
        _route_kernel,
        grid=(t // TM,),
        in_specs=[pl.BlockSpec((e, TM), lambda i: (0, i)),
                  pl.BlockSpec((e, 1), lambda i: (0, 0))],
        out_specs=[row, row, row, pl.BlockSpec((e, LANES), lambda i: (0, 0))],
        out_shape=[jax.ShapeDtypeStruct((TOP_K, t), F32),
                   jax.ShapeDtypeStruct((TOP_K, t), F32),
                   jax.ShapeDtypeStruct((TOP_K, t), F32),
                   jax.ShapeDtypeStruct((e, LANES), F32)],
        compiler_params=_cparams(("arbitrary",), 32),
        name="route",
    )(logitsT, rbias)


def _dest_kernel(idx_ref, rank_ref, ps_ref, dest_ref):
    ei = lax.broadcasted_iota(I32, (N_EXPERTS, idx_ref.shape[1]), 0).astype(F32)
    ps = ps_ref[...]
    rows = []
    for k in range(TOP_K):
        hit = ei == idx_ref[k:k + 1, :]
        rows.append(jnp.sum(jnp.where(hit, ps, 0.0), axis=0, keepdims=True) + rank_ref[k:k + 1, :])
    dest_ref[...] = jnp.concatenate(rows, axis=0).astype(I32)


def _dest(idxT, rankT, pad_start):
    t = idxT.shape[1]
    row = pl.BlockSpec((TOP_K, TM), lambda i: (0, i))
    return pl.pallas_call(
        _dest_kernel,
        grid=(t // TM,),
        in_specs=[row, row, pl.BlockSpec((N_EXPERTS, 1), lambda i: (0, 0))],
        out_specs=row,
        out_shape=jax.ShapeDtypeStruct((TOP_K, t), I32),
        compiler_params=_cparams(("parallel",), 24),
        name="dest",
    )(idxT, rankT, pad_start)


def _sc_workers(mesh, n_tokens):
    n_workers = mesh.num_cores * mesh.num_subcores
    per_worker = n_tokens // n_workers
    assert per_worker * n_workers == n_tokens and per_worker % SC_CHUNK == 0
    return per_worker


def _sc_scratch(width):
    return ([pltpu.VMEM((SC_CHUNK,), I32)] * TOP_K
            + [pltpu.VMEM((SC_CHUNK, width), I32), pltpu.SemaphoreType.DMA])


def _sc_dispatch(rows, destT, n_slots):
    t, width = rows.shape
    mesh = plsc.VectorSubcoreMesh(core_axis_name="c", subcore_axis_name="s")
    per_worker = _sc_workers(mesh, t)

    @functools.partial(pl.kernel, mesh=mesh, out_type=jax.ShapeDtypeStruct((n_slots, width), I32),
                       scratch_types=_sc_scratch(width))
    def scatter_rows(rows_hbm, dest_hbm, xs_hbm, *scratch):
        idx, buf, sem = scratch[:TOP_K], scratch[TOP_K], scratch[TOP_K + 1]
        base = (lax.axis_index("s") * mesh.num_cores + lax.axis_index("c")) * per_worker

        @pl.loop(0, per_worker // SC_CHUNK)
        def _(i):
            t0 = pl.multiple_of(base + i * SC_CHUNK, SC_CHUNK)
            for k in range(TOP_K):
                pltpu.sync_copy(dest_hbm.at[k, pl.ds(t0, SC_CHUNK)], idx[k])
            pltpu.sync_copy(rows_hbm.at[pl.ds(t0, SC_CHUNK)], buf)
            copies = [pltpu.async_copy(buf, xs_hbm.at[idx[k]], sem) for k in range(TOP_K)]
            for c in copies:
                c.wait()

    return scatter_rows(rows, destT)


def _sc_combine(y, destT, tok_off, n_tok):
    width = y.shape[1]
    mesh = plsc.VectorSubcoreMesh(core_axis_name="c", subcore_axis_name="s")
    per_worker = _sc_workers(mesh, n_tok)
    half = SC_CHUNK // 2

    @functools.partial(pl.kernel, mesh=mesh, out_type=jax.ShapeDtypeStruct((TOP_K, n_tok, width), I32),
                       scratch_types=([pltpu.VMEM((half,), I32)] * TOP_K + [pltpu.VMEM((half, width), I32)] * 2
                                      + [pltpu.SemaphoreType.DMA] * 4))
    def gather_rows(y_hbm, dest_hbm, out_hbm, *scratch):
        idx, bufs = scratch[:TOP_K], scratch[TOP_K:TOP_K + 2]
        gsem, wsem = scratch[TOP_K + 2:TOP_K + 4], scratch[TOP_K + 4:TOP_K + 6]
        base = (lax.axis_index("s") * mesh.num_cores + lax.axis_index("c")) * per_worker

        @pl.loop(0, per_worker // half)
        def _(i):
            t0 = pl.multiple_of(base + i * half, half)
            for k in range(TOP_K):
                pltpu.sync_copy(dest_hbm.at[k, pl.ds(tok_off + t0, half)], idx[k])
            gathers = [None] * TOP_K
            writes = [None] * TOP_K
            gathers[0] = pltpu.async_copy(y_hbm.at[idx[0]], bufs[0], gsem[0])
            for k in range(TOP_K):
                if k + 1 < TOP_K:
                    if k >= 1:
                        writes[k - 1].wait()
                    gathers[k + 1] = pltpu.async_copy(y_hbm.at[idx[k + 1]], bufs[(k + 1) % 2], gsem[(k + 1) % 2])
                gathers[k].wait()
                writes[k] = pltpu.async_copy(bufs[k % 2], out_hbm.at[k, pl.ds(t0, half)], wsem[k % 2])
            writes[TOP_K - 2].wait()
            writes[TOP_K - 1].wait()

    return gather_rows(y, destT)


def _ffn_kernel(be_ref, bv_ref, nu_ref, xs_ref, wg_ref, wu_ref, wd_ref, y_ref, wgu_sc, wd_sc):
    b = pl.program_id(0)
    changed = jnp.logical_or(b == 0, be_ref[b] != be_ref[jnp.maximum(b - 1, 0)])

    @pl.when(changed)
    def _():
        wgu_sc[:, 0:D_EXPERT] = wg_ref[0].astype(BF16)
        wgu_sc[:, D_EXPERT:2 * D_EXPERT] = wu_ref[0].astype(BF16)
        wd_sc[...] = wd_ref[0].astype(BF16)

    @pl.when(b < nu_ref[0])
    def _():
        row = lax.broadcasted_iota(I32, (BM, 1), 0)
        hi, lo = _unpack_halves(jnp.where(row < bv_ref[b], xs_ref[...], 0))
        x = jnp.concatenate([hi, lo], axis=1).astype(BF16)
        gu = jnp.dot(x, wgu_sc[...], preferred_element_type=F32)
        a = (_silu(gu[:, 0:D_EXPERT]) * gu[:, D_EXPERT:2 * D_EXPERT]).astype(BF16)
        y_ref[...] = _pack_row(jnp.dot(a, wd_sc[...], preferred_element_type=F32))

    @pl.when(b >= nu_ref[0])
    def _():
        y_ref[...] = jnp.zeros_like(y_ref)


def _ffn(blk_e, blk_valid, n_used, xs, w_eg, w_eu, w_ed):
    n_slots = xs.shape[0]
    nblocks = n_slots // BM
    return pl.pallas_call(
        _ffn_kernel,
        grid_spec=pltpu.PrefetchScalarGridSpec(
            num_scalar_prefetch=3,
            grid=(nblocks,),
            in_specs=[
                pl.BlockSpec((BM, HALF), lambda b, be, bv, nu: (jnp.minimum(b, nu[0] - 1), 0)),
                pl.BlockSpec((1, D_MODEL, D_EXPERT), lambda b, be, bv, nu: (be[b], 0, 0)),
                pl.BlockSpec((1, D_MODEL, D_EXPERT), lambda b, be, bv, nu: (be[b], 0, 0)),
                pl.BlockSpec((1, D_EXPERT, D_MODEL), lambda b, be, bv, nu: (be[b], 0, 0)),
            ],
            out_specs=pl.BlockSpec((BM, HALF), lambda b, be, bv, nu: (b, 0)),
            scratch_shapes=[pltpu.VMEM((D_MODEL, 2 * D_EXPERT), BF16),
                            pltpu.VMEM((D_EXPERT, D_MODEL), BF16)],
        ),
        out_shape=jax.ShapeDtypeStruct((n_slots, HALF), I32),
        compiler_params=_cparams(("arbitrary",), 32),
        name="ffn",
    )(blk_e, blk_valid, n_used, xs, w_eg, w_eu, w_ed)


def _final_kernel(x1_ref, h2_ref, yk_ref, gate_ref, mod_ref, wsgu_ref, wsd_ref, gf_ref, o_ref):
    hi, lo = _unpack_halves(h2_ref[...])
    hb = jnp.concatenate([hi, lo], axis=1).astype(BF16)
    gu = jnp.dot(hb, wsgu_ref[...], preferred_element_type=F32)
    a = (_silu(gu[:, 0:D_SHARED]) * gu[:, D_SHARED:2 * D_SHARED]).astype(BF16)
    shared = jnp.dot(a, wsd_ref[...], preferred_element_type=F32)

    gate = gate_ref[...]
    acc_hi = acc_lo = None
    for k in range(TOP_K):
        hi, lo = _unpack_halves(yk_ref[k])
        g = gate[:, k:k + 1]
        acc_hi = g * hi if acc_hi is None else acc_hi + g * hi
        acc_lo = g * lo if acc_lo is None else acc_lo + g * lo
    routed = jnp.concatenate([acc_hi, acc_lo], axis=1)
    x2 = x1_ref[...] + mod_ref[0, 5:6, :] * (routed + shared)
    o_ref[...] = _rms(x2, -1) * gf_ref[...]


def _final(x1, h2p, yk, gates, mod, wsgu, wsd, gf, tok_off, nb, s, b_off):
    n_tiles = nb * s // TF
    t_off = tok_off // TF
    tpb = s // TF
    const = lambda i: (0, 0)
    return pl.pallas_call(
        _final_kernel,
        grid=(n_tiles,),
        in_specs=[
            pl.BlockSpec((TF, D_MODEL), lambda i: (t_off + i, 0)),
            pl.BlockSpec((TF, HALF), lambda i: (t_off + i, 0)),
            pl.BlockSpec((TOP_K, TF, HALF), lambda i: (0, i, 0)),
            pl.BlockSpec((TF, TOP_K), lambda i: (t_off + i, 0)),
            pl.BlockSpec((1, 6, D_MODEL), lambda i: (b_off + i // tpb, 0, 0)),
            pl.BlockSpec((D_MODEL, 2 * D_SHARED), const),
            pl.BlockSpec((D_SHARED, D_MODEL), const),
            pl.BlockSpec((1, D_MODEL), const),
        ],
        out_specs=pl.BlockSpec((TF, D_MODEL), lambda i: (i, 0)),
        out_shape=jax.ShapeDtypeStruct((nb * s, D_MODEL), F32),
        compiler_params=_cparams(("parallel",), 40),
        name="final",
    )(x1, h2p, yk, gates, mod, wsgu, wsd, gf)


def _t5_bucket(rel):
    nb = N_BUCKETS // 2
    max_exact = nb // 2
    ret = jnp.where(rel > 0, nb, 0)
    n = jnp.abs(rel)
    nf = jnp.maximum(n, 1).astype(F32)
    large = max_exact + (jnp.log(nf / max_exact) / math.log(MAX_DISTANCE / max_exact)
                         * (nb - max_exact)).astype(I32)
    large = jnp.minimum(large, nb - 1)
    return ret + jnp.where(n < max_exact, n, large)


def _bias_tables(rel_bias):
    n_far = np.arange(TK + 1, 1 << 20, dtype=np.float64)
    assert np.all(N_BUCKETS // 4 + np.log(n_far / (N_BUCKETS // 4)) / math.log(MAX_DISTANCE / (N_BUCKETS // 4))
                  * (N_BUCKETS // 4) >= N_BUCKETS // 2), "relative-position buckets must saturate beyond one chunk"
    kk = jnp.arange(TK, dtype=I32)[:, None]
    qq = jnp.arange(TQ, dtype=I32)[None, :]
    rel = jnp.stack([(dd - 2) * TK + kk - qq for dd in range(5)])
    onehot = jax.nn.one_hot(_t5_bucket(rel), N_BUCKETS, dtype=F32)
    band = jnp.einsum("dkqb,bh->hdkq", onehot, rel_bias.astype(F32), precision=lax.Precision.HIGHEST)
    return band * LOG2E


def _rope_tables(s):
    rows = s // GRID_W
    row_id = jnp.repeat(jnp.arange(rows, dtype=F32), GRID_W)
    col_id = jnp.tile(jnp.arange(GRID_W, dtype=F32), rows)
    half = HD_B // 2
    inv = ROPE_THETA ** (-jnp.arange(0, half, 2, dtype=F32) / half)
    ang_r = row_id[:, None] * inv[None, :]
    ang_c = col_id[:, None] * inv[None, :]
    cr, sr, cc, sc = jnp.cos(ang_r), jnp.sin(ang_r), jnp.cos(ang_c), jnp.sin(ang_c)
    c64 = jnp.concatenate([cr, cr, cc, cc], axis=1)
    s64 = jnp.concatenate([-sr, sr, -sc, sc], axis=1)
    return c64.T, s64.T, jnp.concatenate([c64, c64], axis=1), jnp.concatenate([s64, s64], axis=1)


def kernel(x_prompt, x_sample, c_prompt, c_sample, rel_bias, w_ada, b_ada, g_norm1, w_in, lambda_q1, lambda_k1, lambda_q2, lambda_k2, g_subln, g_qnorm, g_knorm, w_out, g_norm2, w_router, router_bias, w_exp_gate, w_exp_up, w_exp_down, w_sh_gate, w_sh_up, w_sh_down, g_final):
    bp, sp, d = x_prompt.shape
    bs, ss, _ = x_sample.shape
    assert d == D_MODEL and sp % TM == 0 and ss % TM == 0 and TM % TK == 0 and TQ == TK
    assert sp % (KSUB * TK) == 0 and ss % (KSUB * TK) == 0
    tp, ts = bp * sp, bs * ss
    t = tp + ts
    assert tp % TF == 0 and ts % TF == 0

    mod = _ada(jnp.concatenate([c_prompt, c_sample], axis=0), w_ada[LAYER], b_ada[LAYER])
    mod = mod.reshape(bp + bs, 6, d)

    w = w_in[LAYER]
    o1, o2, o3 = A_Q, A_Q + A_K, A_Q + A_K + A_V
    o4, o5 = o3 + B_Q, o3 + B_Q + B_K
    wqT = jnp.concatenate([w[:, :o1], w[:, o3:o4]], axis=1).T.astype(BF16)
    wk = jnp.concatenate([w[:, o1:o2], w[:, o4:o5]], axis=1).astype(BF16)
    wvT = jnp.concatenate([w[:, o2:o3], w[:, o5:]], axis=1).T.astype(BF16)
    g1 = g_norm1[LAYER].reshape(1, d)
    gqT = jnp.broadcast_to(g_qnorm[LAYER][:, None], (HD_B, TM))
    gk = jnp.tile(g_knorm[LAYER], KV_B).reshape(1, LANES)
    band = _bias_tables(rel_bias)
    lam_rows = jnp.stack([lambda_q1[LAYER], lambda_k1[LAYER], lambda_q2[LAYER], lambda_k2[LAYER]])
    gsub = jnp.broadcast_to(g_subln[LAYER][:, None], (2 * HD_A, TQ))
    wo = w_out[LAYER].astype(BF16)
    woa, wob = wo[:A_V], wo[A_V:]

    def mixers(x, nb, s, b_off):
        cT, sT, cN, sN = _rope_tables(s)
        qT, k2, vT4 = _inproj(x.reshape(nb * s, d), nb, s, b_off, mod, g1, wqT, wk, wvT, gqT, gk, cT, sT, cN, sN)
        k3 = k2.reshape(nb, s, K_COLS)
        oa = _attention(qT, k3, vT4, True, (band, lam_rows, gsub))
        ob = _attention(qT, k3, vT4, False, ())
        return oa.reshape(nb * s, A_V), ob.reshape(nb * s, B_Q)

    aap, abp = mixers(x_prompt, bp, sp, 0)
    aas, abs_ = mixers(x_sample, bs, ss, bp)

    x1, h2p, logitsT = _outproj(
        x_prompt.reshape(tp, d), aap, abp, sp, x_sample.reshape(ts, d), aas, abs_, ss, mod, woa, wob,
        g_norm2[LAYER].reshape(1, d), w_router[LAYER].T.astype(BF16))

    idxT, gateT, rankT, cnt = _route(logitsT, router_bias[LAYER].reshape(N_EXPERTS, 1))

    counts = cnt[:, 0].astype(I32)
    nblk = (counts + BM - 1) // BM
    blk_end = jnp.cumsum(nblk)
    blk_start = blk_end - nblk
    n_blocks = t * TOP_K // BM + N_EXPERTS
    n_slots = n_blocks * BM
    n_used = blk_end[-1:]
    bidx = jnp.arange(n_blocks, dtype=I32)[:, None]
    eidx = jnp.arange(N_EXPERTS, dtype=I32)[None, :]
    owns = jnp.logical_and(bidx >= blk_start[None, :], bidx < blk_end[None, :])
    last_e = jnp.max(jnp.where(nblk > 0, eidx[0], 0))
    blk_e = jnp.where(bidx[:, 0] < n_used[0], jnp.sum(jnp.where(owns, eidx, 0), axis=1), last_e).astype(I32)
    pad_start = (blk_start * BM).astype(F32).reshape(N_EXPERTS, 1)
    in_blk = jnp.clip(counts[None, :] - (bidx - blk_start[None, :]) * BM, 0, BM)
    blk_valid = jnp.sum(jnp.where(owns, in_blk, 0), axis=1).astype(I32)

    destT = _dest(idxT, rankT, pad_start)
    gates = gateT.T

    n_used = n_used.astype(I32)
    xs = _sc_dispatch(h2p, destT, n_slots)
    y = _ffn(blk_e, blk_valid, n_used, xs, w_exp_gate[LAYER], w_exp_up[LAYER], w_exp_down[LAYER])
    yk_p = _sc_combine(y, destT, 0, tp)
    yk_s = _sc_combine(y, destT, tp, ts)

    wsgu = jnp.concatenate([w_sh_gate[LAYER], w_sh_up[LAYER]], axis=1).astype(BF16)
    wsd = w_sh_down[LAYER].astype(BF16)
    gf = g_final.reshape(1, d)
    y_prompt = _final(x1, h2p, yk_p, gates, mod, wsgu, wsd, gf, 0, bp, sp, 0)
    y_sample = _final(x1, h2p, yk_s, gates, mod, wsgu, wsd, gf, tp, bs, ss, bp)
    return (y_prompt.reshape(bp, sp, d), y_sample.reshape(bs, ss, d))
```

```python
import functools
import math

import jax
import jax.numpy as jnp
import numpy as np
from jax import lax
from jax.experimental import pallas as pl
from jax.experimental.pallas import tpu as pltpu
from jax.experimental.pallas import tpu_sc as plsc

F32 = jnp.float32
BF16 = jnp.bfloat16
I32 = jnp.int32

D_MODEL = 1024
GRID_W = 64
HA = 4
HD_A = 64
HB = 8
KV_B = 2
HD_B = 64
ROPE_THETA = 10000.0
N_BUCKETS = 32
MAX_DISTANCE = 128
A_Q = HA * 2 * HD_A
A_K = HA * 2 * HD_A
A_V = HA * 2 * HD_A
B_Q = HB * HD_B
B_K = KV_B * HD_B
B_V = KV_B * HD_B
N_EXPERTS = 256
TOP_K = 8
N_GROUPS = 8
TOPK_GROUPS = 4
D_EXPERT = 256
D_SHARED = 256
ROUTE_SCALE = 2.5
EPS = 1e-6
LAYER = 0
LAM_INIT = 0.8 - 0.6 * math.exp(-0.3 * LAYER)

LANES = 128
SUBLANES = 8
V7X_VMEM_BYTES = 64 * 1024 * 1024

TM = 512
TQ = 256
TK = 256
KSUB = 4
MAP_ROWS = 2 * HD_A
ONES_ROWS = 16
N_MAPS = 2 * HA + HB
K_COLS = A_K + B_K
V_ROWS = A_V + B_V
BM = 512
TF = 256
HALF = D_MODEL // 2
SC_CHUNK = 128
LOG2E = math.log2(math.e)
Q_SCALE = (HD_A ** -0.5) * LOG2E
NT_DIMS = (((1,), (1,)), ((), ()))
NEG_BIG = -1e30


def _cparams(semantics, vmem_mb):
    assert vmem_mb * 1024 * 1024 <= V7X_VMEM_BYTES
    return pltpu.CompilerParams(dimension_semantics=semantics, vmem_limit_bytes=vmem_mb * 1024 * 1024)


def _rms(x, axis):
    return x * lax.rsqrt(jnp.mean(x * x, axis=axis, keepdims=True) + EPS)


def _silu(x):
    return x * jax.nn.sigmoid(x)


def _pack_row(x):
    hi = pltpu.bitcast(x[:, :HALF].astype(BF16).astype(F32), I32)
    lo = pltpu.bitcast(x[:, HALF:].astype(BF16).astype(F32), I32)
    return hi | lax.shift_right_logical(lo, 16)


def _unpack_halves(p):
    hi = pltpu.bitcast(p & jnp.int32(-65536), F32)
    lo = pltpu.bitcast(lax.shift_left(p, 16), F32)
    return hi, lo


def _ada_kernel(c_ref, w_ref, b_ref, o_ref):
    a = _silu(c_ref[...])
    o_ref[...] = jnp.dot(a, w_ref[...], preferred_element_type=F32) + b_ref[...]


def _ada(c_all, w, b):
    nb, d = c_all.shape
    n = w.shape[1]
    tn = 768
    return pl.pallas_call(
        _ada_kernel,
        grid=(n // tn,),
        in_specs=[pl.BlockSpec((nb, d), lambda j: (0, 0)),
                  pl.BlockSpec((d, tn), lambda j: (0, j)),
                  pl.BlockSpec((1, tn), lambda j: (0, j))],
        out_specs=pl.BlockSpec((nb, tn), lambda j: (0, j)),
        out_shape=jax.ShapeDtypeStruct((nb, n), F32),
        compiler_params=_cparams(("parallel",), 24),
        name="ada",
    )(c_all, w, b.reshape(1, n))


def _inproj_kernel(x_ref, mod_ref, g1_ref, wqT_ref, wk_ref, wvT_ref, gqT_ref, gk_ref,
                   cT_ref, sT_ref, cN_ref, sN_ref, qT_ref, k_ref, vT_ref):
    x = x_ref[...]
    tm = x.shape[0]
    shift = mod_ref[0, 0:1, :]
    scale = mod_ref[0, 1:2, :]
    h = (_rms(x, -1) * g1_ref[...]) * (1.0 + scale) + shift
    hb = h.astype(BF16)
    qT = lax.dot_general(wqT_ref[...], hb, NT_DIMS, preferred_element_type=F32)
    kn = jnp.dot(hb, wk_ref[...], preferred_element_type=F32)
    vT = lax.dot_general(wvT_ref[...], hb, NT_DIMS, preferred_element_type=F32)

    vTb = vT.astype(BF16)
    for c in range(tm // TK):
        vT_ref[0, c] = vTb[:, c * TK:(c + 1) * TK]

    zeros = jnp.zeros((HD_A, tm), BF16)

    def put_map(m, q, upper):
        r0 = m * MAP_ROWS
        if upper:
            qT_ref[0, r0:r0 + HD_A, :] = zeros
            qT_ref[0, r0 + HD_A:r0 + MAP_ROWS, :] = q
        else:
            qT_ref[0, r0:r0 + HD_A, :] = q
            qT_ref[0, r0 + HD_A:r0 + MAP_ROWS, :] = zeros

    for m in range(2 * HA):
        q = (qT[m * HD_A:(m + 1) * HD_A, :] * Q_SCALE).astype(BF16)
        put_map(m, q, upper=(m % 2 == 1))

    cT = cT_ref[...]
    sT = sT_ref[...]
    gq = gqT_ref[...]
    for j in range(HB):
        xh = qT[A_Q + j * HD_B:A_Q + (j + 1) * HD_B, :]
        y = _rms(xh, 0) * gq
        yp = jnp.concatenate([y[16:32], y[0:16], y[48:64], y[32:48]], axis=0)
        r = (y * cT + yp * sT) * Q_SCALE
        put_map(2 * HA + j, r.astype(BF16), upper=(j // (HB // KV_B) == 1))

    kb = kn[:, A_K:K_COLS]
    lane = lax.broadcasted_iota(I32, kb.shape, 1)
    first = lane < HD_B
    x2 = kb * kb
    s0 = jnp.sum(jnp.where(first, x2, 0.0), axis=-1, keepdims=True)
    s1 = jnp.sum(jnp.where(first, 0.0, x2), axis=-1, keepdims=True)
    ms = jnp.where(first, s0, s1) * (1.0 / HD_B)
    y = kb * lax.rsqrt(ms + EPS) * gk_ref[...]
    partner = jnp.where((lane % 32) < 16, pltpu.roll(y, LANES - 16, 1), pltpu.roll(y, 16, 1))
    r = y * cN_ref[...] + partner * sN_ref[...]
    k_ref[:, 0:A_K] = kn[:, 0:A_K].astype(BF16)
    k_ref[:, A_K:K_COLS] = r.astype(BF16)


def _inproj(x2d, nb, s, b_off, mod, g1, wqT, wk, wvT, gqT, gk, cT, sT, cN, sN):
    nps = s // TM
    const = lambda i: (0, 0)
    return pl.pallas_call(
        _inproj_kernel,
        grid=(nb * nps,),
        in_specs=[
            pl.BlockSpec((TM, D_MODEL), lambda i: (i, 0)),
            pl.BlockSpec((1, 6, D_MODEL), lambda i: (b_off + i // nps, 0, 0)),
            pl.BlockSpec((1, D_MODEL), const),
            pl.BlockSpec((A_Q + B_Q, D_MODEL), const),
            pl.BlockSpec((D_MODEL, K_COLS), const),
            pl.BlockSpec((V_ROWS, D_MODEL), const),
            pl.BlockSpec((HD_B, TM), const),
            pl.BlockSpec((1, LANES), const),
            pl.BlockSpec((HD_B, TM), lambda i: (0, i % nps)),
            pl.BlockSpec((HD_B, TM), lambda i: (0, i % nps)),
            pl.BlockSpec((TM, LANES), lambda i: (i % nps, 0)),
            pl.BlockSpec((TM, LANES), lambda i: (i % nps, 0)),
        ],
        out_specs=[
            pl.BlockSpec((1, N_MAPS * MAP_ROWS, TM), lambda i: (i // nps, 0, i % nps)),
            pl.BlockSpec((TM, K_COLS), lambda i: (i, 0)),
            pl.BlockSpec((1, TM // TK, V_ROWS, TK), lambda i: (i // nps, i % nps, 0, 0)),
        ],
        out_shape=[
            jax.ShapeDtypeStruct((nb, N_MAPS * MAP_ROWS, s), BF16),
            jax.ShapeDtypeStruct((nb * s, K_COLS), BF16),
            jax.ShapeDtypeStruct((nb, s // TK, V_ROWS, TK), BF16),
        ],
        compiler_params=_cparams(("parallel",), 48),
        name="inproj",
    )(x2d, mod, g1, wqT, wk, wvT, gqT, gk, cT, sT, cN, sN)


def _attn_kernel(*refs, is_diff, n_big, ksub):
    if is_diff:
        qT_ref, k_ref, vT_ref, band_ref, lam_ref, gsub_ref, o_ref, s_a, s_b = refs
    else:
        qT_ref, k_ref, vT_ref, o_ref, s_a, s_b = refs
    qi = pl.program_id(2)
    tq = qT_ref.shape[2]
    dv = vT_ref.shape[2]
    q2 = jnp.concatenate([qT_ref[0, 0:MAP_ROWS, :], qT_ref[0, MAP_ROWS:2 * MAP_ROWS, :]], axis=1)

    def score_stage(g, s_buf):
        cm = None
        for j in range(ksub):
            c = g * ksub + j
            kc = k_ref[0, pl.ds(pl.multiple_of(c * TK, TK), TK), :]
            sj = jnp.dot(kc, q2, preferred_element_type=F32)
            if is_diff:
                bias = band_ref[0, jnp.clip(c - qi + 2, 0, 4)]
                sj = sj + jnp.concatenate([bias, bias], axis=1)
            s_buf[j * TK:(j + 1) * TK, :] = sj
            cj = jnp.max(sj, axis=0, keepdims=True)
            cm = cj if cm is None else jnp.maximum(cm, cj)
        return cm

    ones = jnp.ones((ONES_ROWS, TK), BF16)

    def softmax_stage(g, s_buf, cm, carry):
        m, acc = carry
        m_new = jnp.maximum(m, cm)
        acc = jnp.exp2(m - m_new) * acc
        for j in range(ksub):
            d = s_buf[j * TK:(j + 1) * TK, :] - m_new
            vT = vT_ref[0, g * ksub + j]
            if is_diff:
                p = jnp.exp2(d.astype(BF16))
                acc = acc + jnp.dot(jnp.concatenate([vT, ones], axis=0), p, preferred_element_type=F32)
            else:
                p = jnp.exp2(d)
                pv = jnp.dot(vT, p.astype(BF16), preferred_element_type=F32)
                lsum = jnp.broadcast_to(jnp.sum(p, axis=0, keepdims=True), (ONES_ROWS, p.shape[1]))
                acc = acc + jnp.concatenate([pv, lsum], axis=0)
        return m_new, acc

    def pair(i, state):
        carry, cm_a = state
        g = 2 * i
        cm_b = score_stage(g + 1, s_b)
        carry = softmax_stage(g, s_a, cm_a, carry)
        cm_a = score_stage(g + 2, s_a)
        carry = softmax_stage(g + 1, s_b, cm_b, carry)
        return carry, cm_a

    init = (jnp.full((1, 2 * tq), NEG_BIG, F32), jnp.zeros((dv + ONES_ROWS, 2 * tq), F32))
    carry, cm_a = lax.fori_loop(0, n_big // 2 - 1, pair, (init, score_stage(0, s_a)))
    cm_b = score_stage(n_big - 1, s_b)
    carry = softmax_stage(n_big - 2, s_a, cm_a, carry)
    _, acc = softmax_stage(n_big - 1, s_b, cm_b, carry)
    o2 = acc[0:dv, :] * (1.0 / acc[dv:dv + 1, :])
    outs = [o2[:, 0:tq], o2[:, tq:2 * tq]]

    if is_diff:
        lv = lam_ref[...]
        lam = (jnp.exp(jnp.sum(lv[0:1] * lv[1:2], axis=-1, keepdims=True))
               - jnp.exp(jnp.sum(lv[2:3] * lv[3:4], axis=-1, keepdims=True)) + LAM_INIT)
        o = outs[0] - lam * outs[1]
        o = _rms(o, 0) * gsub_ref[...] * (1.0 - LAM_INIT)
    else:
        o = jnp.concatenate(outs, axis=0)
    o_ref[0] = o.T.astype(BF16)


def _attention(qT, k3, vT4, is_diff, extra):
    nb, s, _ = k3.shape
    nk = s // TK
    nq = s // TQ
    if is_diff:
        q_map = lambda b, u, qi: (b, u, qi)
        k_map = lambda b, u, qi: (b, 0, u)
        v_spec = pl.BlockSpec((1, nk, 2 * HD_A, TK), lambda b, u, qi: (b, 0, u, 0))
        extra_specs = [
            pl.BlockSpec((1, 5, TK, TQ), lambda b, u, qi: (u, 0, 0, 0)),
            pl.BlockSpec((4, HD_A), lambda b, u, qi: (0, 0)),
            pl.BlockSpec((2 * HD_A, TQ), lambda b, u, qi: (0, 0)),
        ]
        n_units = HA
    else:
        first = (2 * HA * MAP_ROWS) // (2 * MAP_ROWS)
        q_map = lambda b, u, qi: (b, first + u, qi)
        k_map = lambda b, u, qi: (b, 0, A_K // LANES)
        v0 = A_V // HD_B
        per_kv = (HB // KV_B) // 2
        v_spec = pl.BlockSpec((1, nk, HD_B, TK), lambda b, u, qi: (b, 0, v0 + u // per_kv, 0))
        extra_specs = []
        n_units = HB // 2
    ksub = KSUB if nk >= 4 * KSUB else KSUB // 2
    n_big = nk // ksub
    assert n_big * ksub == nk and n_big % 2 == 0
    return pl.pallas_call(
        functools.partial(_attn_kernel, is_diff=is_diff, n_big=n_big, ksub=ksub),
        grid=(nb, n_units, nq),
        in_specs=[pl.BlockSpec((1, 2 * MAP_ROWS, TQ), q_map),
                  pl.BlockSpec((1, s, LANES), k_map),
                  v_spec] + extra_specs,
        out_specs=pl.BlockSpec((1, TQ, LANES), lambda b, u, qi: (b, qi, u)),
        out_shape=jax.ShapeDtypeStruct((nb, s, n_units * LANES), BF16),
        scratch_shapes=[pltpu.VMEM((ksub * TK, 2 * TQ), F32)] * 2,
        compiler_params=_cparams(("parallel", "parallel", "parallel"), 40),
        name="attn_diff" if is_diff else "attn_gqa",
    )(qT, k3, vT4, *extra)


def _outproj_kernel(x_ref, aa_ref, ab_ref, mod_ref, woa_ref, wob_ref, g2_ref, wrT_ref, x1_ref, h2_ref, lg_ref):
    att = (jnp.dot(aa_ref[...], woa_ref[...], preferred_element_type=F32)
           + jnp.dot(ab_ref[...], wob_ref[...], preferred_element_type=F32))
    x1 = x_ref[...] + mod_ref[0, 2:3, :] * att
    h2 = (_rms(x1, -1) * g2_ref[...]) * (1.0 + mod_ref[0, 4:5, :]) + mod_ref[0, 3:4, :]
    x1_ref[...] = x1
    h2_ref[...] = _pack_row(h2)
    lg_ref[...] = lax.dot_general(wrT_ref[...], h2.astype(BF16), NT_DIMS, preferred_element_type=F32)


def _outproj(x2d, aa, ab, s, b_off, mod, woa, wob, g2, wrT):
    t = x2d.shape[0]
    tpb = s // TM
    const = lambda i: (0, 0)
    row = lambda i: (i, 0)
    return pl.pallas_call(
        _outproj_kernel,
        grid=(t // TM,),
        in_specs=[
            pl.BlockSpec((TM, D_MODEL), row),
            pl.BlockSpec((TM, A_V), row),
            pl.BlockSpec((TM, B_Q), row),
            pl.BlockSpec((1, 6, D_MODEL), lambda i: (b_off + i // tpb, 0, 0)),
            pl.BlockSpec((A_V, D_MODEL), const),
            pl.BlockSpec((B_Q, D_MODEL), const),
            pl.BlockSpec((1, D_MODEL), const),
            pl.BlockSpec((N_EXPERTS, D_MODEL), const),
        ],
        out_specs=[
            pl.BlockSpec((TM, D_MODEL), lambda i: (i, 0)),
            pl.BlockSpec((TM, HALF), lambda i: (i, 0)),
            pl.BlockSpec((N_EXPERTS, TM), lambda i: (0, i)),
        ],
        out_shape=[
            jax.ShapeDtypeStruct((t, D_MODEL), F32),
            jax.ShapeDtypeStruct((t, HALF), I32),
            jax.ShapeDtypeStruct((N_EXPERTS, t), F32),
        ],
        compiler_params=_cparams(("parallel",), 48),
        name="outproj",
    )(x2d, aa, ab, mod, woa, wob, g2, wrT)


def _route_kernel(lg_ref, rb_ref, idx_ref, gate_ref, rank_ref, cnt_ref):
    i = pl.program_id(0)

    @pl.when(i == 0)
    def _():
        cnt_ref[...] = jnp.zeros_like(cnt_ref)

    scores = jax.nn.sigmoid(lg_ref[...])
    tr = scores.shape[1]
    sel = scores + rb_ref[...]
    gsz = N_EXPERTS // N_GROUPS

    rows = []
    for g in range(N_GROUPS):
        blk = sel[g * gsz:(g + 1) * gsz, :]
        m1 = jnp.max(blk, axis=0, keepdims=True)
        eq = blk == m1
        n1 = jnp.sum(jnp.where(eq, 1.0, 0.0), axis=0, keepdims=True)
        m2 = jnp.max(jnp.where(eq, -jnp.inf, blk), axis=0, keepdims=True)
        rows.append(m1 + jnp.where(n1 >= 2.0, m1, m2))
    gs = jnp.concatenate(rows, axis=0)
    gi = lax.broadcasted_iota(I32, gs.shape, 0)
    beaten = jnp.zeros(gs.shape, F32)
    for gp in range(N_GROUPS):
        row = gs[gp:gp + 1, :]
        ahead = jnp.where(row > gs, 1.0, jnp.where(jnp.logical_and(row == gs, gp < gi), 1.0, 0.0))
        beaten = beaten + ahead
    keep = jnp.where(beaten < float(TOPK_GROUPS), 1.0, 0.0)
    emask = jnp.concatenate(
        [jnp.broadcast_to(keep[g:g + 1, :], (gsz, tr)) for g in range(N_GROUPS)], axis=0) > 0.5
    masked = jnp.where(emask, sel, -jnp.inf)

    ei = lax.broadcasted_iota(I32, masked.shape, 0).astype(F32)
    idxs, ws = [], []
    for _ in range(TOP_K):
        mk = jnp.max(masked, axis=0, keepdims=True)
        ik = jnp.min(jnp.where(masked == mk, ei, float(N_EXPERTS)), axis=0, keepdims=True)
        hit = ei == ik
        ws.append(jnp.sum(jnp.where(hit, scores, 0.0), axis=0, keepdims=True))
        masked = jnp.where(hit, -jnp.inf, masked)
        idxs.append(ik)
    wsum = ws[0]
    for w in ws[1:]:
        wsum = wsum + w
    gate_ref[...] = jnp.concatenate([w / wsum * ROUTE_SCALE for w in ws], axis=0)
    idx_ref[...] = jnp.concatenate(idxs, axis=0)

    chosen = jnp.logical_and(emask, masked == -jnp.inf)
    onehot = jnp.where(chosen, 1.0, 0.0)
    before = (lax.broadcasted_iota(I32, (tr, tr), 0) < lax.broadcasted_iota(I32, (tr, tr), 1))
    upper = jnp.where(before, 1.0, 0.0).astype(BF16)
    prior = jnp.dot(onehot.astype(BF16), upper, preferred_element_type=F32) + cnt_ref[:, 0:1]
    rank_ref[...] = jnp.concatenate(
        [jnp.sum(jnp.where(ei == ik, prior, 0.0), axis=0, keepdims=True) for ik in idxs], axis=0)
    cnt_ref[...] = cnt_ref[...] + jnp.sum(onehot, axis=1, keepdims=True)


def _route(logitsT, rbias):
    e, t = logitsT.shape
    row = pl.BlockSpec((TOP_K, TM), lambda i: (0, i))
    return pl.pallas_call(
        _route_kernel,
        grid=(t // TM,),
        in_specs=[pl.BlockSpec((e, TM), lambda i: (0, i)),
                  pl.BlockSpec((e, 1), lambda i: (0, 0))],
        out_specs=[row, row, row, pl.BlockSpec((e, LANES), lambda i: (0, 0))],
        out_shape=[jax.ShapeDtypeStruct((TOP_K, t), F32),
                   jax.ShapeDtypeStruct((TOP_K, t), F32),
                   jax.ShapeDtypeStruct((TOP_K, t), F32),
                   jax.ShapeDtypeStruct((e, LANES), F32)],
        compiler_params=_cparams(("arbitrary",), 32),
        name="route",
    )(logitsT, rbias)


def _dest_kernel(idx_ref, rank_ref, ps_ref, dest_ref):
    ei = lax.broadcasted_iota(I32, (N_EXPERTS, idx_ref.shape[1]), 0).astype(F32)
    ps = ps_ref[...]
    rows = []
    for k in range(TOP_K):
        hit = ei == idx_ref[k:k + 1, :]
        rows.append(jnp.sum(jnp.where(hit, ps, 0.0), axis=0, keepdims=True) + rank_ref[k:k + 1, :])
    dest_ref[...] = jnp.concatenate(rows, axis=0).astype(I32)


def _dest(idxT, rankT, pad_start):
    t = idxT.shape[1]
    row = pl.BlockSpec((TOP_K, TM), lambda i: (0, i))
    return pl.pallas_call(
        _dest_kernel,
        grid=(t // TM,),
        in_specs=[row, row, pl.BlockSpec((N_EXPERTS, 1), lambda i: (0, 0))],
        out_specs=row,
        out_shape=jax.ShapeDtypeStruct((TOP_K, t), I32),
        compiler_params=_cparams(("parallel",), 24),
        name="dest",
    )(idxT, rankT, pad_start)


def _sc_workers(mesh, n_tokens):
    n_workers = mesh.num_cores * mesh.num_subcores
    per_worker = n_tokens // n_workers
    assert per_worker * n_workers == n_tokens and per_worker % SC_CHUNK == 0
    return per_worker


def _sc_scratch(width):
    return ([pltpu.VMEM((SC_CHUNK,), I32)] * TOP_K
            + [pltpu.VMEM((SC_CHUNK, width), I32), pltpu.SemaphoreType.DMA])


def _sc_dispatch(rows, destT, n_slots):
    t, width = rows.shape
    mesh = plsc.VectorSubcoreMesh(core_axis_name="c", subcore_axis_name="s")
    per_worker = _sc_workers(mesh, t)

    @functools.partial(pl.kernel, mesh=mesh, out_type=jax.ShapeDtypeStruct((n_slots, width), I32),
                       scratch_types=_sc_scratch(width))
    def scatter_rows(rows_hbm, dest_hbm, xs_hbm, *scratch):
        idx, buf, sem = scratch[:TOP_K], scratch[TOP_K], scratch[TOP_K + 1]
        base = (lax.axis_index("s") * mesh.num_cores + lax.axis_index("c")) * per_worker

        @pl.loop(0, per_worker // SC_CHUNK)
        def _(i):
            t0 = pl.multiple_of(base + i * SC_CHUNK, SC_CHUNK)
            for k in range(TOP_K):
                pltpu.sync_copy(dest_hbm.at[k, pl.ds(t0, SC_CHUNK)], idx[k])
            pltpu.sync_copy(rows_hbm.at[pl.ds(t0, SC_CHUNK)], buf)
            copies = [pltpu.async_copy(buf, xs_hbm.at[idx[k]], sem) for k in range(TOP_K)]
            for c in copies:
                c.wait()

    return scatter_rows(rows, destT)


def _sc_combine(y, destT, tok_off, n_tok):
    width = y.shape[1]
    mesh = plsc.VectorSubcoreMesh(core_axis_name="c", subcore_axis_name="s")
    per_worker = _sc_workers(mesh, n_tok)
    half = SC_CHUNK // 2

    @functools.partial(pl.kernel, mesh=mesh, out_type=jax.ShapeDtypeStruct((TOP_K, n_tok, width), I32),
                       scratch_types=([pltpu.VMEM((half,), I32)] * TOP_K + [pltpu.VMEM((half, width), I32)] * 2
                                      + [pltpu.SemaphoreType.DMA] * 4))
    def gather_rows(y_hbm, dest_hbm, out_hbm, *scratch):
        idx, bufs = scratch[:TOP_K], scratch[TOP_K:TOP_K + 2]
        gsem, wsem = scratch[TOP_K + 2:TOP_K + 4], scratch[TOP_K + 4:TOP_K + 6]
        base = (lax.axis_index("s") * mesh.num_cores + lax.axis_index("c")) * per_worker

        @pl.loop(0, per_worker // half)
        def _(i):
            t0 = pl.multiple_of(base + i * half, half)
            for k in range(TOP_K):
                pltpu.sync_copy(dest_hbm.at[k, pl.ds(tok_off + t0, half)], idx[k])
            gathers = [None] * TOP_K
            writes = [None] * TOP_K
            gathers[0] = pltpu.async_copy(y_hbm.at[idx[0]], bufs[0], gsem[0])
            for k in range(TOP_K):
                if k + 1 < TOP_K:
                    if k >= 1:
                        writes[k - 1].wait()
                    gathers[k + 1] = pltpu.async_copy(y_hbm.at[idx[k + 1]], bufs[(k + 1) % 2], gsem[(k + 1) % 2])
                gathers[k].wait()
                writes[k] = pltpu.async_copy(bufs[k % 2], out_hbm.at[k, pl.ds(t0, half)], wsem[k % 2])
            writes[TOP_K - 2].wait()
            writes[TOP_K - 1].wait()

    return gather_rows(y, destT)


def _ffn_kernel(be_ref, bv_ref, nu_ref, xs_ref, wg_ref, wu_ref, wd_ref, y_ref, wgu_sc, wd_sc):
    b = pl.program_id(0)
    changed = jnp.logical_or(b == 0, be_ref[b] != be_ref[jnp.maximum(b - 1, 0)])

    @pl.when(changed)
    def _():
        wgu_sc[:, 0:D_EXPERT] = wg_ref[0].astype(BF16)
        wgu_sc[:, D_EXPERT:2 * D_EXPERT] = wu_ref[0].astype(BF16)
        wd_sc[...] = wd_ref[0].astype(BF16)

    @pl.when(b < nu_ref[0])
    def _():
        row = lax.broadcasted_iota(I32, (BM, 1), 0)
        hi, lo = _unpack_halves(jnp.where(row < bv_ref[b], xs_ref[...], 0))
        x = jnp.concatenate([hi, lo], axis=1).astype(BF16)
        gu = jnp.dot(x, wgu_sc[...], preferred_element_type=F32)
        a = (_silu(gu[:, 0:D_EXPERT]) * gu[:, D_EXPERT:2 * D_EXPERT]).astype(BF16)
        y_ref[...] = _pack_row(jnp.dot(a, wd_sc[...], preferred_element_type=F32))

    @pl.when(b >= nu_ref[0])
    def _():
        y_ref[...] = jnp.zeros_like(y_ref)


def _ffn(blk_e, blk_valid, n_used, xs, w_eg, w_eu, w_ed):
    n_slots = xs.shape[0]
    nblocks = n_slots // BM
    return pl.pallas_call(
        _ffn_kernel,
        grid_spec=pltpu.PrefetchScalarGridSpec(
            num_scalar_prefetch=3,
            grid=(nblocks,),
            in_specs=[
                pl.BlockSpec((BM, HALF), lambda b, be, bv, nu: (jnp.minimum(b, nu[0] - 1), 0)),
                pl.BlockSpec((1, D_MODEL, D_EXPERT), lambda b, be, bv, nu: (be[b], 0, 0)),
                pl.BlockSpec((1, D_MODEL, D_EXPERT), lambda b, be, bv, nu: (be[b], 0, 0)),
                pl.BlockSpec((1, D_EXPERT, D_MODEL), lambda b, be, bv, nu: (be[b], 0, 0)),
            ],
            out_specs=pl.BlockSpec((BM, HALF), lambda b, be, bv, nu: (b, 0)),
            scratch_shapes=[pltpu.VMEM((D_MODEL, 2 * D_EXPERT), BF16),
                            pltpu.VMEM((D_EXPERT, D_MODEL), BF16)],
        ),
        out_shape=jax.ShapeDtypeStruct((n_slots, HALF), I32),
        compiler_params=_cparams(("arbitrary",), 32),
        name="ffn",
    )(blk_e, blk_valid, n_used, xs, w_eg, w_eu, w_ed)


def _final_kernel(x1_ref, h2_ref, yk_ref, gate_ref, mod_ref, wsgu_ref, wsd_ref, gf_ref, o_ref):
    hi, lo = _unpack_halves(h2_ref[...])
    hb = jnp.concatenate([hi, lo], axis=1).astype(BF16)
    gu = jnp.dot(hb, wsgu_ref[...], preferred_element_type=F32)
    a = (_silu(gu[:, 0:D_SHARED]) * gu[:, D_SHARED:2 * D_SHARED]).astype(BF16)
    shared = jnp.dot(a, wsd_ref[...], preferred_element_type=F32)

    gate = gate_ref[...]
    acc_hi = acc_lo = None
    for k in range(TOP_K):
        hi, lo = _unpack_halves(yk_ref[k])
        g = gate[:, k:k + 1]
        acc_hi = g * hi if acc_hi is None else acc_hi + g * hi
        acc_lo = g * lo if acc_lo is None else acc_lo + g * lo
    routed = jnp.concatenate([acc_hi, acc_lo], axis=1)
    x2 = x1_ref[...] + mod_ref[0, 5:6, :] * (routed + shared)
    o_ref[...] = _rms(x2, -1) * gf_ref[...]


def _final(x1, h2p, yk, gates, mod, wsgu, wsd, gf, tok_off, nb, s, b_off):
    n_tiles = nb * s // TF
    t_off = tok_off // TF
    tpb = s // TF
    const = lambda i: (0, 0)
    return pl.pallas_call(
        _final_kernel,
        grid=(n_tiles,),
        in_specs=[
            pl.BlockSpec((TF, D_MODEL), lambda i: (t_off + i, 0)),
            pl.BlockSpec((TF, HALF), lambda i: (t_off + i, 0)),
            pl.BlockSpec((TOP_K, TF, HALF), lambda i: (0, i, 0)),
            pl.BlockSpec((TF, TOP_K), lambda i: (t_off + i, 0)),
            pl.BlockSpec((1, 6, D_MODEL), lambda i: (b_off + i // tpb, 0, 0)),
            pl.BlockSpec((D_MODEL, 2 * D_SHARED), const),
            pl.BlockSpec((D_SHARED, D_MODEL), const),
            pl.BlockSpec((1, D_MODEL), const),
        ],
        out_specs=pl.BlockSpec((TF, D_MODEL), lambda i: (i, 0)),
        out_shape=jax.ShapeDtypeStruct((nb * s, D_MODEL), F32),
        compiler_params=_cparams(("parallel",), 40),
        name="final",
    )(x1, h2p, yk, gates, mod, wsgu, wsd, gf)


def _t5_bucket(rel):
    nb = N_BUCKETS // 2
    max_exact = nb // 2
    ret = jnp.where(rel > 0, nb, 0)
    n = jnp.abs(rel)
    nf = jnp.maximum(n, 1).astype(F32)
    large = max_exact + (jnp.log(nf / max_exact) / math.log(MAX_DISTANCE / max_exact)
                         * (nb - max_exact)).astype(I32)
    large = jnp.minimum(large, nb - 1)
    return ret + jnp.where(n < max_exact, n, large)


def _bias_tables(rel_bias):
    n_far = np.arange(TK + 1, 1 << 20, dtype=np.float64)
    assert np.all(N_BUCKETS // 4 + np.log(n_far / (N_BUCKETS // 4)) / math.log(MAX_DISTANCE / (N_BUCKETS // 4))
                  * (N_BUCKETS // 4) >= N_BUCKETS // 2), "relative-position buckets must saturate beyond one chunk"
    kk = jnp.arange(TK, dtype=I32)[:, None]
    qq = jnp.arange(TQ, dtype=I32)[None, :]
    rel = jnp.stack([(dd - 2) * TK + kk - qq for dd in range(5)])
    onehot = jax.nn.one_hot(_t5_bucket(rel), N_BUCKETS, dtype=F32)
    band = jnp.einsum("dkqb,bh->hdkq", onehot, rel_bias.astype(F32), precision=lax.Precision.HIGHEST)
    return band * LOG2E


def _rope_tables(s):
    rows = s // GRID_W
    row_id = jnp.repeat(jnp.arange(rows, dtype=F32), GRID_W)
    col_id = jnp.tile(jnp.arange(GRID_W, dtype=F32), rows)
    half = HD_B // 2
    inv = ROPE_THETA ** (-jnp.arange(0, half, 2, dtype=F32) / half)
    ang_r = row_id[:, None] * inv[None, :]
    ang_c = col_id[:, None] * inv[None, :]
    cr, sr, cc, sc = jnp.cos(ang_r), jnp.sin(ang_r), jnp.cos(ang_c), jnp.sin(ang_c)
    c64 = jnp.concatenate([cr, cr, cc, cc], axis=1)
    s64 = jnp.concatenate([-sr, sr, -sc, sc], axis=1)
    return c64.T, s64.T, jnp.concatenate([c64, c64], axis=1), jnp.concatenate([s64, s64], axis=1)


def kernel(x_prompt, x_sample, c_prompt, c_sample, rel_bias, w_ada, b_ada, g_norm1, w_in, lambda_q1, lambda_k1, lambda_q2, lambda_k2, g_subln, g_qnorm, g_knorm, w_out, g_norm2, w_router, router_bias, w_exp_gate, w_exp_up, w_exp_down, w_sh_gate, w_sh_up, w_sh_down, g_final):
    bp, sp, d = x_prompt.shape
    bs, ss, _ = x_sample.shape
    assert d == D_MODEL and sp % TM == 0 and ss % TM == 0 and TM % TK == 0 and TQ == TK
    assert sp % (KSUB * TK) == 0 and ss % (KSUB * TK) == 0

    mod = _ada(jnp.concatenate([c_prompt, c_sample], axis=0), w_ada[LAYER], b_ada[LAYER])
    mod = mod.reshape(bp + bs, 6, d)

    w = w_in[LAYER]
    o1, o2, o3 = A_Q, A_Q + A_K, A_Q + A_K + A_V
    o4, o5 = o3 + B_Q, o3 + B_Q + B_K
    wqT = jnp.concatenate([w[:, :o1], w[:, o3:o4]], axis=1).T.astype(BF16)
    wk = jnp.concatenate([w[:, o1:o2], w[:, o4:o5]], axis=1).astype(BF16)
    wvT = jnp.concatenate([w[:, o2:o3], w[:, o5:]], axis=1).T.astype(BF16)
    g1 = g_norm1[LAYER].reshape(1, d)
    gqT = jnp.broadcast_to(g_qnorm[LAYER][:, None], (HD_B, TM))
    gk = jnp.tile(g_knorm[LAYER], KV_B).reshape(1, LANES)
    band = _bias_tables(rel_bias)
    lam_rows = jnp.stack([lambda_q1[LAYER], lambda_k1[LAYER], lambda_q2[LAYER], lambda_k2[LAYER]])
    gsub = jnp.broadcast_to(g_subln[LAYER][:, None], (2 * HD_A, TQ))
    wo = w_out[LAYER].astype(BF16)
    woa, wob = wo[:A_V], wo[A_V:]

    g2 = g_norm2[LAYER].reshape(1, d)
    wrT = w_router[LAYER].T.astype(BF16)
    rbias = router_bias[LAYER].reshape(N_EXPERTS, 1)
    wsgu = jnp.concatenate([w_sh_gate[LAYER], w_sh_up[LAYER]], axis=1).astype(BF16)
    wsd = w_sh_down[LAYER].astype(BF16)
    gf = g_final.reshape(1, d)

    def layer(x, nb, s, b_off):
        t = nb * s
        x2d = x.reshape(t, d)
        cT, sT, cN, sN = _rope_tables(s)
        qT, k2, vT4 = _inproj(x2d, nb, s, b_off, mod, g1, wqT, wk, wvT, gqT, gk, cT, sT, cN, sN)
        k3 = k2.reshape(nb, s, K_COLS)
        oa = _attention(qT, k3, vT4, True, (band, lam_rows, gsub)).reshape(t, A_V)
        ob = _attention(qT, k3, vT4, False, ()).reshape(t, B_Q)
        x1, h2p, logitsT = _outproj(x2d, oa, ob, s, b_off, mod, woa, wob, g2, wrT)
        idxT, gateT, rankT, cnt = _route(logitsT, rbias)

        counts = cnt[:, 0].astype(I32)
        nblk = (counts + BM - 1) // BM
        blk_end = jnp.cumsum(nblk)
        blk_start = blk_end - nblk
        n_blocks = t * TOP_K // BM + N_EXPERTS
        n_used = blk_end[-1:].astype(I32)
        bidx = jnp.arange(n_blocks, dtype=I32)[:, None]
        eidx = jnp.arange(N_EXPERTS, dtype=I32)[None, :]
        owns = jnp.logical_and(bidx >= blk_start[None, :], bidx < blk_end[None, :])
        last_e = jnp.max(jnp.where(nblk > 0, eidx[0], 0))
        blk_e = jnp.where(bidx[:, 0] < n_used[0], jnp.sum(jnp.where(owns, eidx, 0), axis=1), last_e).astype(I32)
        pad_start = (blk_start * BM).astype(F32).reshape(N_EXPERTS, 1)
        in_blk = jnp.clip(counts[None, :] - (bidx - blk_start[None, :]) * BM, 0, BM)
        blk_valid = jnp.sum(jnp.where(owns, in_blk, 0), axis=1).astype(I32)

        destT = _dest(idxT, rankT, pad_start)
        xs = _sc_dispatch(h2p, destT, n_blocks * BM)
        y = _ffn(blk_e, blk_valid, n_used, xs, w_exp_gate[LAYER], w_exp_up[LAYER], w_exp_down[LAYER])
        yk = _sc_combine(y, destT, 0, t)
        out = _final(x1, h2p, yk, gateT.T, mod, wsgu, wsd, gf, 0, nb, s, b_off)
        return out.reshape(nb, s, d)

    return (layer(x_prompt, bp, sp, 0), layer(x_sample, bs, ss, bp))
```

```python
import functools
import math

import jax
import jax.numpy as jnp
import numpy as np
from jax import lax
from jax.experimental import pallas as pl
from jax.experimental.pallas import tpu as pltpu
from jax.experimental.pallas import tpu_sc as plsc

F32 = jnp.float32
BF16 = jnp.bfloat16
I32 = jnp.int32

D_MODEL = 1024
GRID_W = 64
HA = 4
HD_A = 64
HB = 8
KV_B = 2
HD_B = 64
ROPE_THETA = 10000.0
N_BUCKETS = 32
MAX_DISTANCE = 128
A_Q = HA * 2 * HD_A
A_K = HA * 2 * HD_A
A_V = HA * 2 * HD_A
B_Q = HB * HD_B
B_K = KV_B * HD_B
B_V = KV_B * HD_B
N_EXPERTS = 256
TOP_K = 8
N_GROUPS = 8
TOPK_GROUPS = 4
D_EXPERT = 256
D_SHARED = 256
ROUTE_SCALE = 2.5
EPS = 1e-6
LAYER = 0
LAM_INIT = 0.8 - 0.6 * math.exp(-0.3 * LAYER)

LANES = 128
SUBLANES = 8
V7X_VMEM_BYTES = 64 * 1024 * 1024

TM = 512
TQ = 256
TK = 256
KSUB = 4
MAP_ROWS = 2 * HD_A
ONES_ROWS = 16
N_MAPS = 2 * HA + HB
K_COLS = A_K + B_K
V_ROWS = A_V + B_V
BM = 256
TF = 256
HALF = D_MODEL // 2
SC_CHUNK = 128
LOG2E = math.log2(math.e)
Q_SCALE = (HD_A ** -0.5) * LOG2E
NT_DIMS = (((1,), (1,)), ((), ()))
NEG_BIG = -1e30


def _cparams(semantics, vmem_mb):
    assert vmem_mb * 1024 * 1024 <= V7X_VMEM_BYTES
    return pltpu.CompilerParams(dimension_semantics=semantics, vmem_limit_bytes=vmem_mb * 1024 * 1024)


def _rms(x, axis):
    return x * lax.rsqrt(jnp.mean(x * x, axis=axis, keepdims=True) + EPS)


def _silu(x):
    return x * jax.nn.sigmoid(x)


def _pack_row(x):
    hi = pltpu.bitcast(x[:, :HALF].astype(BF16).astype(F32), I32)
    lo = pltpu.bitcast(x[:, HALF:].astype(BF16).astype(F32), I32)
    return hi | lax.shift_right_logical(lo, 16)


def _unpack_halves(p):
    hi = pltpu.bitcast(p & jnp.int32(-65536), F32)
    lo = pltpu.bitcast(lax.shift_left(p, 16), F32)
    return hi, lo


def _ada_kernel(c_ref, w_ref, b_ref, o_ref):
    a = _silu(c_ref[...])
    o_ref[...] = jnp.dot(a, w_ref[...], preferred_element_type=F32) + b_ref[...]


def _ada(c_all, w, b):
    nb, d = c_all.shape
    n = w.shape[1]
    tn = 768
    return pl.pallas_call(
        _ada_kernel,
        grid=(n // tn,),
        in_specs=[pl.BlockSpec((nb, d), lambda j: (0, 0)),
                  pl.BlockSpec((d, tn), lambda j: (0, j)),
                  pl.BlockSpec((1, tn), lambda j: (0, j))],
        out_specs=pl.BlockSpec((nb, tn), lambda j: (0, j)),
        out_shape=jax.ShapeDtypeStruct((nb, n), F32),
        compiler_params=_cparams(("parallel",), 24),
        name="ada",
    )(c_all, w, b.reshape(1, n))


def _inproj_kernel(x_ref, mod_ref, g1_ref, wqT_ref, wk_ref, wvT_ref, gqT_ref, gk_ref,
                   cT_ref, sT_ref, cN_ref, sN_ref, qT_ref, k_ref, vT_ref):
    x = x_ref[...]
    tm = x.shape[0]
    shift = mod_ref[0, 0:1, :]
    scale = mod_ref[0, 1:2, :]
    h = (_rms(x, -1) * g1_ref[...]) * (1.0 + scale) + shift
    hb = h.astype(BF16)
    qT = lax.dot_general(wqT_ref[...], hb, NT_DIMS, preferred_element_type=F32)
    kn = jnp.dot(hb, wk_ref[...], preferred_element_type=F32)
    vT = lax.dot_general(wvT_ref[...], hb, NT_DIMS, preferred_element_type=F32)

    vTb = vT.astype(BF16)
    for c in range(tm // TK):
        vT_ref[0, c] = vTb[:, c * TK:(c + 1) * TK]

    zeros = jnp.zeros((HD_A, tm), BF16)

    def put_map(m, q, upper):
        r0 = m * MAP_ROWS
        if upper:
            qT_ref[0, r0:r0 + HD_A, :] = zeros
            qT_ref[0, r0 + HD_A:r0 + MAP_ROWS, :] = q
        else:
            qT_ref[0, r0:r0 + HD_A, :] = q
            qT_ref[0, r0 + HD_A:r0 + MAP_ROWS, :] = zeros

    for m in range(2 * HA):
        q = (qT[m * HD_A:(m + 1) * HD_A, :] * Q_SCALE).astype(BF16)
        put_map(m, q, upper=(m % 2 == 1))

    cT = cT_ref[...]
    sT = sT_ref[...]
    gq = gqT_ref[...]
    for j in range(HB):
        xh = qT[A_Q + j * HD_B:A_Q + (j + 1) * HD_B, :]
        y = _rms(xh, 0) * gq
        yp = jnp.concatenate([y[16:32], y[0:16], y[48:64], y[32:48]], axis=0)
        r = (y * cT + yp * sT) * Q_SCALE
        put_map(2 * HA + j, r.astype(BF16), upper=(j // (HB // KV_B) == 1))

    kb = kn[:, A_K:K_COLS]
    lane = lax.broadcasted_iota(I32, kb.shape, 1)
    first = lane < HD_B
    x2 = kb * kb
    s0 = jnp.sum(jnp.where(first, x2, 0.0), axis=-1, keepdims=True)
    s1 = jnp.sum(jnp.where(first, 0.0, x2), axis=-1, keepdims=True)
    ms = jnp.where(first, s0, s1) * (1.0 / HD_B)
    y = kb * lax.rsqrt(ms + EPS) * gk_ref[...]
    partner = jnp.where((lane % 32) < 16, pltpu.roll(y, LANES - 16, 1), pltpu.roll(y, 16, 1))
    r = y * cN_ref[...] + partner * sN_ref[...]
    k_ref[:, 0:A_K] = kn[:, 0:A_K].astype(BF16)
    k_ref[:, A_K:K_COLS] = r.astype(BF16)


def _inproj(x2d, nb, s, b_off, mod, g1, wqT, wk, wvT, gqT, gk, cT, sT, cN, sN):
    nps = s // TM
    const = lambda i: (0, 0)
    return pl.pallas_call(
        _inproj_kernel,
        grid=(nb * nps,),
        in_specs=[
            pl.BlockSpec((TM, D_MODEL), lambda i: (i, 0)),
            pl.BlockSpec((1, 6, D_MODEL), lambda i: (b_off + i // nps, 0, 0)),
            pl.BlockSpec((1, D_MODEL), const),
            pl.BlockSpec((A_Q + B_Q, D_MODEL), const),
            pl.BlockSpec((D_MODEL, K_COLS), const),
            pl.BlockSpec((V_ROWS, D_MODEL), const),
            pl.BlockSpec((HD_B, TM), const),
            pl.BlockSpec((1, LANES), const),
            pl.BlockSpec((HD_B, TM), lambda i: (0, i % nps)),
            pl.BlockSpec((HD_B, TM), lambda i: (0, i % nps)),
            pl.BlockSpec((TM, LANES), lambda i: (i % nps, 0)),
            pl.BlockSpec((TM, LANES), lambda i: (i % nps, 0)),
        ],
        out_specs=[
            pl.BlockSpec((1, N_MAPS * MAP_ROWS, TM), lambda i: (i // nps, 0, i % nps)),
            pl.BlockSpec((TM, K_COLS), lambda i: (i, 0)),
            pl.BlockSpec((1, TM // TK, V_ROWS, TK), lambda i: (i // nps, i % nps, 0, 0)),
        ],
        out_shape=[
            jax.ShapeDtypeStruct((nb, N_MAPS * MAP_ROWS, s), BF16),
            jax.ShapeDtypeStruct((nb * s, K_COLS), BF16),
            jax.ShapeDtypeStruct((nb, s // TK, V_ROWS, TK), BF16),
        ],
        compiler_params=_cparams(("parallel",), 48),
        name="inproj",
    )(x2d, mod, g1, wqT, wk, wvT, gqT, gk, cT, sT, cN, sN)


def _attn_kernel(*refs, is_diff, n_big, ksub):
    if is_diff:
        qT_ref, k_ref, vT_ref, band_ref, lam_ref, gsub_ref, o_ref, s_a, s_b = refs
    else:
        qT_ref, k_ref, vT_ref, o_ref, s_a, s_b = refs
    qi = pl.program_id(2)
    tq = qT_ref.shape[2]
    dv = vT_ref.shape[2]
    q2 = jnp.concatenate([qT_ref[0, 0:MAP_ROWS, :], qT_ref[0, MAP_ROWS:2 * MAP_ROWS, :]], axis=1)

    def score_stage(g, s_buf):
        cm = None
        for j in range(ksub):
            c = g * ksub + j
            kc = k_ref[0, pl.ds(pl.multiple_of(c * TK, TK), TK), :]
            sj = jnp.dot(kc, q2, preferred_element_type=F32)
            if is_diff:
                bias = band_ref[0, jnp.clip(c - qi + 2, 0, 4)]
                sj = sj + jnp.concatenate([bias, bias], axis=1)
            s_buf[j * TK:(j + 1) * TK, :] = sj
            cj = jnp.max(sj, axis=0, keepdims=True)
            cm = cj if cm is None else jnp.maximum(cm, cj)
        return cm

    ones = jnp.ones((ONES_ROWS, TK), BF16)

    def softmax_stage(g, s_buf, cm, carry):
        m, acc = carry
        m_new = jnp.maximum(m, cm)
        acc = jnp.exp2(m - m_new) * acc
        for j in range(ksub):
            d = s_buf[j * TK:(j + 1) * TK, :] - m_new
            vT = vT_ref[0, g * ksub + j]
            if is_diff:
                p = jnp.exp2(d.astype(BF16))
                acc = acc + jnp.dot(jnp.concatenate([vT, ones], axis=0), p, preferred_element_type=F32)
            else:
                p = jnp.exp2(d)
                pv = jnp.dot(vT, p.astype(BF16), preferred_element_type=F32)
                lsum = jnp.broadcast_to(jnp.sum(p, axis=0, keepdims=True), (ONES_ROWS, p.shape[1]))
                acc = acc + jnp.concatenate([pv, lsum], axis=0)
        return m_new, acc

    def pair(i, state):
        carry, cm_a = state
        g = 2 * i
        cm_b = score_stage(g + 1, s_b)
        carry = softmax_stage(g, s_a, cm_a, carry)
        cm_a = score_stage(g + 2, s_a)
        carry = softmax_stage(g + 1, s_b, cm_b, carry)
        return carry, cm_a

    init = (jnp.full((1, 2 * tq), NEG_BIG, F32), jnp.zeros((dv + ONES_ROWS, 2 * tq), F32))
    carry, cm_a = lax.fori_loop(0, n_big // 2 - 1, pair, (init, score_stage(0, s_a)))
    cm_b = score_stage(n_big - 1, s_b)
    carry = softmax_stage(n_big - 2, s_a, cm_a, carry)
    _, acc = softmax_stage(n_big - 1, s_b, cm_b, carry)
    o2 = acc[0:dv, :] * (1.0 / acc[dv:dv + 1, :])
    outs = [o2[:, 0:tq], o2[:, tq:2 * tq]]

    if is_diff:
        lv = lam_ref[...]
        lam = (jnp.exp(jnp.sum(lv[0:1] * lv[1:2], axis=-1, keepdims=True))
               - jnp.exp(jnp.sum(lv[2:3] * lv[3:4], axis=-1, keepdims=True)) + LAM_INIT)
        o = outs[0] - lam * outs[1]
        o = _rms(o, 0) * gsub_ref[...] * (1.0 - LAM_INIT)
    else:
        o = jnp.concatenate(outs, axis=0)
    o_ref[0] = o.T.astype(BF16)


def _attention(qT, k3, vT4, is_diff, extra):
    nb, s, _ = k3.shape
    nk = s // TK
    nq = s // TQ
    if is_diff:
        q_map = lambda b, u, qi: (b, u, qi)
        k_map = lambda b, u, qi: (b, 0, u)
        v_spec = pl.BlockSpec((1, nk, 2 * HD_A, TK), lambda b, u, qi: (b, 0, u, 0))
        extra_specs = [
            pl.BlockSpec((1, 5, TK, TQ), lambda b, u, qi: (u, 0, 0, 0)),
            pl.BlockSpec((4, HD_A), lambda b, u, qi: (0, 0)),
            pl.BlockSpec((2 * HD_A, TQ), lambda b, u, qi: (0, 0)),
        ]
        n_units = HA
    else:
        first = (2 * HA * MAP_ROWS) // (2 * MAP_ROWS)
        q_map = lambda b, u, qi: (b, first + u, qi)
        k_map = lambda b, u, qi: (b, 0, A_K // LANES)
        v0 = A_V // HD_B
        per_kv = (HB // KV_B) // 2
        v_spec = pl.BlockSpec((1, nk, HD_B, TK), lambda b, u, qi: (b, 0, v0 + u // per_kv, 0))
        extra_specs = []
        n_units = HB // 2
    ksub = KSUB if nk >= 4 * KSUB else KSUB // 2
    n_big = nk // ksub
    assert n_big * ksub == nk and n_big % 2 == 0
    return pl.pallas_call(
        functools.partial(_attn_kernel, is_diff=is_diff, n_big=n_big, ksub=ksub),
        grid=(nb, n_units, nq),
        in_specs=[pl.BlockSpec((1, 2 * MAP_ROWS, TQ), q_map),
                  pl.BlockSpec((1, s, LANES), k_map),
                  v_spec] + extra_specs,
        out_specs=pl.BlockSpec((1, TQ, LANES), lambda b, u, qi: (b, qi, u)),
        out_shape=jax.ShapeDtypeStruct((nb, s, n_units * LANES), BF16),
        scratch_shapes=[pltpu.VMEM((ksub * TK, 2 * TQ), F32)] * 2,
        compiler_params=_cparams(("parallel", "parallel", "parallel"), 40),
        name="attn_diff" if is_diff else "attn_gqa",
    )(qT, k3, vT4, *extra)


def _outproj_kernel(x_ref, aa_ref, ab_ref, mod_ref, woa_ref, wob_ref, g2_ref, wrT_ref, x1_ref, h2_ref, lg_ref):
    att = (jnp.dot(aa_ref[...], woa_ref[...], preferred_element_type=F32)
           + jnp.dot(ab_ref[...], wob_ref[...], preferred_element_type=F32))
    x1 = x_ref[...] + mod_ref[0, 2:3, :] * att
    h2 = (_rms(x1, -1) * g2_ref[...]) * (1.0 + mod_ref[0, 4:5, :]) + mod_ref[0, 3:4, :]
    x1_ref[...] = x1
    h2_ref[...] = _pack_row(h2)
    lg_ref[...] = lax.dot_general(wrT_ref[...], h2.astype(BF16), NT_DIMS, preferred_element_type=F32)


def _outproj(x2d, aa, ab, s, b_off, mod, woa, wob, g2, wrT):
    t = x2d.shape[0]
    tpb = s // TM
    const = lambda i: (0, 0)
    row = lambda i: (i, 0)
    return pl.pallas_call(
        _outproj_kernel,
        grid=(t // TM,),
        in_specs=[
            pl.BlockSpec((TM, D_MODEL), row),
            pl.BlockSpec((TM, A_V), row),
            pl.BlockSpec((TM, B_Q), row),
            pl.BlockSpec((1, 6, D_MODEL), lambda i: (b_off + i // tpb, 0, 0)),
            pl.BlockSpec((A_V, D_MODEL), const),
            pl.BlockSpec((B_Q, D_MODEL), const),
            pl.BlockSpec((1, D_MODEL), const),
            pl.BlockSpec((N_EXPERTS, D_MODEL), const),
        ],
        out_specs=[
            pl.BlockSpec((TM, D_MODEL), lambda i: (i, 0)),
            pl.BlockSpec((TM, HALF), lambda i: (i, 0)),
            pl.BlockSpec((N_EXPERTS, TM), lambda i: (0, i)),
        ],
        out_shape=[
            jax.ShapeDtypeStruct((t, D_MODEL), F32),
            jax.ShapeDtypeStruct((t, HALF), I32),
            jax.ShapeDtypeStruct((N_EXPERTS, t), F32),
        ],
        compiler_params=_cparams(("parallel",), 48),
        name="outproj",
    )(x2d, aa, ab, mod, woa, wob, g2, wrT)


def _route_kernel(lg_ref, rb_ref, idx_ref, gate_ref, rank_ref, cnt_ref):
    i = pl.program_id(0)

    @pl.when(i == 0)
    def _():
        cnt_ref[...] = jnp.zeros_like(cnt_ref)

    scores = jax.nn.sigmoid(lg_ref[...])
    tr = scores.shape[1]
    sel = scores + rb_ref[...]
    gsz = N_EXPERTS // N_GROUPS

    rows = []
    for g in range(N_GROUPS):
        blk = sel[g * gsz:(g + 1) * gsz, :]
        m1 = jnp.max(blk, axis=0, keepdims=True)
        eq = blk == m1
        n1 = jnp.sum(jnp.where(eq, 1.0, 0.0), axis=0, keepdims=True)
        m2 = jnp.max(jnp.where(eq, -jnp.inf, blk), axis=0, keepdims=True)
        rows.append(m1 + jnp.where(n1 >= 2.0, m1, m2))
    gs = jnp.concatenate(rows, axis=0)
    gi = lax.broadcasted_iota(I32, gs.shape, 0)
    beaten = jnp.zeros(gs.shape, F32)
    for gp in range(N_GROUPS):
        row = gs[gp:gp + 1, :]
        ahead = jnp.where(row > gs, 1.0, jnp.where(jnp.logical_and(row == gs, gp < gi), 1.0, 0.0))
        beaten = beaten + ahead
    keep = jnp.where(beaten < float(TOPK_GROUPS), 1.0, 0.0)
    emask = jnp.concatenate(
        [jnp.broadcast_to(keep[g:g + 1, :], (gsz, tr)) for g in range(N_GROUPS)], axis=0) > 0.5
    masked = jnp.where(emask, sel, -jnp.inf)

    ei = lax.broadcasted_iota(I32, masked.shape, 0).astype(F32)
    idxs, ws = [], []
    for _ in range(TOP_K):
        mk = jnp.max(masked, axis=0, keepdims=True)
        ik = jnp.min(jnp.where(masked == mk, ei, float(N_EXPERTS)), axis=0, keepdims=True)
        hit = ei == ik
        ws.append(jnp.sum(jnp.where(hit, scores, 0.0), axis=0, keepdims=True))
        masked = jnp.where(hit, -jnp.inf, masked)
        idxs.append(ik)
    wsum = ws[0]
    for w in ws[1:]:
        wsum = wsum + w
    gate_ref[...] = jnp.concatenate([w / wsum * ROUTE_SCALE for w in ws], axis=0)
    idx_ref[...] = jnp.concatenate(idxs, axis=0)

    chosen = jnp.logical_and(emask, masked == -jnp.inf)
    onehot = jnp.where(chosen, 1.0, 0.0)
    before = (lax.broadcasted_iota(I32, (tr, tr), 0) < lax.broadcasted_iota(I32, (tr, tr), 1))
    upper = jnp.where(before, 1.0, 0.0).astype(BF16)
    prior = jnp.dot(onehot.astype(BF16), upper, preferred_element_type=F32) + cnt_ref[:, 0:1]
    rank_ref[...] = jnp.concatenate(
        [jnp.sum(jnp.where(ei == ik, prior, 0.0), axis=0, keepdims=True) for ik in idxs], axis=0)
    cnt_ref[...] = cnt_ref[...] + jnp.sum(onehot, axis=1, keepdims=True)


def _route(logitsT, rbias):
    e, t = logitsT.shape
    row = pl.BlockSpec((TOP_K, TM), lambda i: (0, i))
    return pl.pallas_call(
        _route_kernel,
        grid=(t // TM,),
        in_specs=[pl.BlockSpec((e, TM), lambda i: (0, i)),
                  pl.BlockSpec((e, 1), lambda i: (0, 0))],
        out_specs=[row, row, row, pl.BlockSpec((e, LANES), lambda i: (0, 0))],
        out_shape=[jax.ShapeDtypeStruct((TOP_K, t), F32),
                   jax.ShapeDtypeStruct((TOP_K, t), F32),
                   jax.ShapeDtypeStruct((TOP_K, t), F32),
                   jax.ShapeDtypeStruct((e, LANES), F32)],
        compiler_params=_cparams(("arbitrary",), 32),
        name="route",
    )(logitsT, rbias)


def _dest_kernel(idx_ref, rank_ref, ps_ref, dest_ref):
    ei = lax.broadcasted_iota(I32, (N_EXPERTS, idx_ref.shape[1]), 0).astype(F32)
    ps = ps_ref[...]
    rows = []
    for k in range(TOP_K):
        hit = ei == idx_ref[k:k + 1, :]
        rows.append(jnp.sum(jnp.where(hit, ps, 0.0), axis=0, keepdims=True) + rank_ref[k:k + 1, :])
    dest_ref[...] = jnp.concatenate(rows, axis=0).astype(I32)


def _dest(idxT, rankT, pad_start):
    t = idxT.shape[1]
    row = pl.BlockSpec((TOP_K, TM), lambda i: (0, i))
    return pl.pallas_call(
        _dest_kernel,
        grid=(t // TM,),
        in_specs=[row, row, pl.BlockSpec((N_EXPERTS, 1), lambda i: (0, 0))],
        out_specs=row,
        out_shape=jax.ShapeDtypeStruct((TOP_K, t), I32),
        compiler_params=_cparams(("parallel",), 24),
        name="dest",
    )(idxT, rankT, pad_start)


def _sc_workers(mesh, n_tokens):
    n_workers = mesh.num_cores * mesh.num_subcores
    per_worker = n_tokens // n_workers
    assert per_worker * n_workers == n_tokens and per_worker % SC_CHUNK == 0
    return per_worker


def _sc_scratch(width):
    return ([pltpu.VMEM((SC_CHUNK,), I32)] * TOP_K
            + [pltpu.VMEM((SC_CHUNK, width), I32), pltpu.SemaphoreType.DMA])


def _sc_dispatch(rows, destT, n_slots):
    t, width = rows.shape
    mesh = plsc.VectorSubcoreMesh(core_axis_name="c", subcore_axis_name="s")
    per_worker = _sc_workers(mesh, t)

    @functools.partial(pl.kernel, mesh=mesh, out_type=jax.ShapeDtypeStruct((n_slots, width), I32),
                       scratch_types=_sc_scratch(width))
    def scatter_rows(rows_hbm, dest_hbm, xs_hbm, *scratch):
        idx, buf, sem = scratch[:TOP_K], scratch[TOP_K], scratch[TOP_K + 1]
        base = (lax.axis_index("s") * mesh.num_cores + lax.axis_index("c")) * per_worker

        @pl.loop(0, per_worker // SC_CHUNK)
        def _(i):
            t0 = pl.multiple_of(base + i * SC_CHUNK, SC_CHUNK)
            for k in range(TOP_K):
                pltpu.sync_copy(dest_hbm.at[k, pl.ds(t0, SC_CHUNK)], idx[k])
            pltpu.sync_copy(rows_hbm.at[pl.ds(t0, SC_CHUNK)], buf)
            copies = [pltpu.async_copy(buf, xs_hbm.at[idx[k]], sem) for k in range(TOP_K)]
            for c in copies:
                c.wait()

    return scatter_rows(rows, destT)


def _sc_combine(y, destT, tok_off, n_tok):
    width = y.shape[1]
    mesh = plsc.VectorSubcoreMesh(core_axis_name="c", subcore_axis_name="s")
    per_worker = _sc_workers(mesh, n_tok)
    half = SC_CHUNK // 2

    @functools.partial(pl.kernel, mesh=mesh, out_type=jax.ShapeDtypeStruct((TOP_K, n_tok, width), I32),
                       scratch_types=([pltpu.VMEM((half,), I32)] * TOP_K + [pltpu.VMEM((half, width), I32)] * 2
                                      + [pltpu.SemaphoreType.DMA] * 4))
    def gather_rows(y_hbm, dest_hbm, out_hbm, *scratch):
        idx, bufs = scratch[:TOP_K], scratch[TOP_K:TOP_K + 2]
        gsem, wsem = scratch[TOP_K + 2:TOP_K + 4], scratch[TOP_K + 4:TOP_K + 6]
        base = (lax.axis_index("s") * mesh.num_cores + lax.axis_index("c")) * per_worker

        @pl.loop(0, per_worker // half)
        def _(i):
            t0 = pl.multiple_of(base + i * half, half)
            for k in range(TOP_K):
                pltpu.sync_copy(dest_hbm.at[k, pl.ds(tok_off + t0, half)], idx[k])
            gathers = [None] * TOP_K
            writes = [None] * TOP_K
            gathers[0] = pltpu.async_copy(y_hbm.at[idx[0]], bufs[0], gsem[0])
            for k in range(TOP_K):
                if k + 1 < TOP_K:
                    if k >= 1:
                        writes[k - 1].wait()
                    gathers[k + 1] = pltpu.async_copy(y_hbm.at[idx[k + 1]], bufs[(k + 1) % 2], gsem[(k + 1) % 2])
                gathers[k].wait()
                writes[k] = pltpu.async_copy(bufs[k % 2], out_hbm.at[k, pl.ds(t0, half)], wsem[k % 2])
            writes[TOP_K - 2].wait()
            writes[TOP_K - 1].wait()

    return gather_rows(y, destT)


def _ffn_kernel(bs_ref, be_ref, bv_ref, nu_ref, wg_ref, wu_ref, wd_ref, xs_hbm, y_hbm,
                wgu_sc, wd_sc, xbuf, ybuf, lsem, ssem):
    e = pl.program_id(0)
    n_used = nu_ref[0]

    def load(g):
        return pltpu.make_async_copy(xs_hbm.at[pl.ds(pl.multiple_of(g * BM, BM), BM)], xbuf.at[g % 2], lsem.at[g % 2])

    def store(g):
        return pltpu.make_async_copy(ybuf.at[g % 2], y_hbm.at[pl.ds(pl.multiple_of(g * BM, BM), BM)], ssem.at[g % 2])

    @pl.when(e == 0)
    def _():
        load(0).start()

    wgu_sc[:, 0:D_EXPERT] = wg_ref[0].astype(BF16)
    wgu_sc[:, D_EXPERT:2 * D_EXPERT] = wu_ref[0].astype(BF16)
    wd_sc[...] = wd_ref[0].astype(BF16)

    def block(g, c):
        load(g).wait()

        @pl.when(g + 1 < n_used)
        def _():
            load(g + 1).start()

        row = lax.broadcasted_iota(I32, (BM, 1), 0)
        hi, lo = _unpack_halves(jnp.where(row < bv_ref[g], xbuf[g % 2], 0))
        x = jnp.concatenate([hi, lo], axis=1).astype(BF16)
        gu = jnp.dot(x, wgu_sc[...], preferred_element_type=F32)
        a = (_silu(gu[:, 0:D_EXPERT]) * gu[:, D_EXPERT:2 * D_EXPERT]).astype(BF16)
        y = _pack_row(jnp.dot(a, wd_sc[...], preferred_element_type=F32))

        @pl.when(g >= 2)
        def _():
            store(g - 2).wait()

        ybuf[g % 2] = y
        store(g).start()
        return c

    lax.fori_loop(bs_ref[e], be_ref[e], block, 0)

    @pl.when(e == pl.num_programs(0) - 1)
    def _():
        @pl.when(n_used >= 2)
        def _():
            store(n_used - 2).wait()

        store(n_used - 1).wait()


def _ffn(blk_start, blk_end, blk_valid, n_used, xs, w_eg, w_eu, w_ed):
    n_slots = xs.shape[0]
    weights = lambda e, bs, be, bv, nu: (e, 0, 0)
    return pl.pallas_call(
        _ffn_kernel,
        grid_spec=pltpu.PrefetchScalarGridSpec(
            num_scalar_prefetch=4,
            grid=(N_EXPERTS,),
            in_specs=[
                pl.BlockSpec((1, D_MODEL, D_EXPERT), weights),
                pl.BlockSpec((1, D_MODEL, D_EXPERT), weights),
                pl.BlockSpec((1, D_EXPERT, D_MODEL), weights),
                pl.BlockSpec(memory_space=pl.ANY),
            ],
            out_specs=pl.BlockSpec(memory_space=pl.ANY),
            scratch_shapes=[pltpu.VMEM((D_MODEL, 2 * D_EXPERT), BF16),
                            pltpu.VMEM((D_EXPERT, D_MODEL), BF16),
                            pltpu.VMEM((2, BM, HALF), I32),
                            pltpu.VMEM((2, BM, HALF), I32),
                            pltpu.SemaphoreType.DMA((2,)),
                            pltpu.SemaphoreType.DMA((2,))],
        ),
        out_shape=jax.ShapeDtypeStruct((n_slots, HALF), I32),
        compiler_params=_cparams(("arbitrary",), 32),
        name="ffn",
    )(blk_start, blk_end, blk_valid, n_used, w_eg, w_eu, w_ed, xs)


def _final_kernel(x1_ref, h2_ref, yk_ref, gate_ref, mod_ref, wsgu_ref, wsd_ref, gf_ref, o_ref):
    hi, lo = _unpack_halves(h2_ref[...])
    hb = jnp.concatenate([hi, lo], axis=1).astype(BF16)
    gu = jnp.dot(hb, wsgu_ref[...], preferred_element_type=F32)
    a = (_silu(gu[:, 0:D_SHARED]) * gu[:, D_SHARED:2 * D_SHARED]).astype(BF16)
    shared = jnp.dot(a, wsd_ref[...], preferred_element_type=F32)

    gate = gate_ref[...]
    acc_hi = acc_lo = None
    for k in range(TOP_K):
        hi, lo = _unpack_halves(yk_ref[k])
        g = gate[:, k:k + 1]
        acc_hi = g * hi if acc_hi is None else acc_hi + g * hi
        acc_lo = g * lo if acc_lo is None else acc_lo + g * lo
    routed = jnp.concatenate([acc_hi, acc_lo], axis=1)
    x2 = x1_ref[...] + mod_ref[0, 5:6, :] * (routed + shared)
    o_ref[...] = _rms(x2, -1) * gf_ref[...]


def _final(x1, h2p, yk, gates, mod, wsgu, wsd, gf, tok_off, nb, s, b_off):
    n_tiles = nb * s // TF
    t_off = tok_off // TF
    tpb = s // TF
    const = lambda i: (0, 0)
    return pl.pallas_call(
        _final_kernel,
        grid=(n_tiles,),
        in_specs=[
            pl.BlockSpec((TF, D_MODEL), lambda i: (t_off + i, 0)),
            pl.BlockSpec((TF, HALF), lambda i: (t_off + i, 0)),
            pl.BlockSpec((TOP_K, TF, HALF), lambda i: (0, i, 0)),
            pl.BlockSpec((TF, TOP_K), lambda i: (t_off + i, 0)),
            pl.BlockSpec((1, 6, D_MODEL), lambda i: (b_off + i // tpb, 0, 0)),
            pl.BlockSpec((D_MODEL, 2 * D_SHARED), const),
            pl.BlockSpec((D_SHARED, D_MODEL), const),
            pl.BlockSpec((1, D_MODEL), const),
        ],
        out_specs=pl.BlockSpec((TF, D_MODEL), lambda i: (i, 0)),
        out_shape=jax.ShapeDtypeStruct((nb * s, D_MODEL), F32),
        compiler_params=_cparams(("parallel",), 40),
        name="final",
    )(x1, h2p, yk, gates, mod, wsgu, wsd, gf)


def _t5_bucket(rel):
    nb = N_BUCKETS // 2
    max_exact = nb // 2
    ret = jnp.where(rel > 0, nb, 0)
    n = jnp.abs(rel)
    nf = jnp.maximum(n, 1).astype(F32)
    large = max_exact + (jnp.log(nf / max_exact) / math.log(MAX_DISTANCE / max_exact)
                         * (nb - max_exact)).astype(I32)
    large = jnp.minimum(large, nb - 1)
    return ret + jnp.where(n < max_exact, n, large)


def _bias_tables(rel_bias):
    n_far = np.arange(TK + 1, 1 << 20, dtype=np.float64)
    assert np.all(N_BUCKETS // 4 + np.log(n_far / (N_BUCKETS // 4)) / math.log(MAX_DISTANCE / (N_BUCKETS // 4))
                  * (N_BUCKETS // 4) >= N_BUCKETS // 2), "relative-position buckets must saturate beyond one chunk"
    kk = jnp.arange(TK, dtype=I32)[:, None]
    qq = jnp.arange(TQ, dtype=I32)[None, :]
    rel = jnp.stack([(dd - 2) * TK + kk - qq for dd in range(5)])
    onehot = jax.nn.one_hot(_t5_bucket(rel), N_BUCKETS, dtype=F32)
    band = jnp.einsum("dkqb,bh->hdkq", onehot, rel_bias.astype(F32), precision=lax.Precision.HIGHEST)
    return band * LOG2E


def _rope_tables(s):
    rows = s // GRID_W
    row_id = jnp.repeat(jnp.arange(rows, dtype=F32), GRID_W)
    col_id = jnp.tile(jnp.arange(GRID_W, dtype=F32), rows)
    half = HD_B // 2
    inv = ROPE_THETA ** (-jnp.arange(0, half, 2, dtype=F32) / half)
    ang_r = row_id[:, None] * inv[None, :]
    ang_c = col_id[:, None] * inv[None, :]
    cr, sr, cc, sc = jnp.cos(ang_r), jnp.sin(ang_r), jnp.cos(ang_c), jnp.sin(ang_c)
    c64 = jnp.concatenate([cr, cr, cc, cc], axis=1)
    s64 = jnp.concatenate([-sr, sr, -sc, sc], axis=1)
    return c64.T, s64.T, jnp.concatenate([c64, c64], axis=1), jnp.concatenate([s64, s64], axis=1)


def kernel(x_prompt, x_sample, c_prompt, c_sample, rel_bias, w_ada, b_ada, g_norm1, w_in, lambda_q1, lambda_k1, lambda_q2, lambda_k2, g_subln, g_qnorm, g_knorm, w_out, g_norm2, w_router, router_bias, w_exp_gate, w_exp_up, w_exp_down, w_sh_gate, w_sh_up, w_sh_down, g_final):
    bp, sp, d = x_prompt.shape
    bs, ss, _ = x_sample.shape
    assert d == D_MODEL and sp % TM == 0 and ss % TM == 0 and TM % TK == 0 and TQ == TK
    assert sp % (KSUB * TK) == 0 and ss % (KSUB * TK) == 0

    mod = _ada(jnp.concatenate([c_prompt, c_sample], axis=0), w_ada[LAYER], b_ada[LAYER])
    mod = mod.reshape(bp + bs, 6, d)

    w = w_in[LAYER]
    o1, o2, o3 = A_Q, A_Q + A_K, A_Q + A_K + A_V
    o4, o5 = o3 + B_Q, o3 + B_Q + B_K
    wqT = jnp.concatenate([w[:, :o1], w[:, o3:o4]], axis=1).T.astype(BF16)
    wk = jnp.concatenate([w[:, o1:o2], w[:, o4:o5]], axis=1).astype(BF16)
    wvT = jnp.concatenate([w[:, o2:o3], w[:, o5:]], axis=1).T.astype(BF16)
    g1 = g_norm1[LAYER].reshape(1, d)
    gqT = jnp.broadcast_to(g_qnorm[LAYER][:, None], (HD_B, TM))
    gk = jnp.tile(g_knorm[LAYER], KV_B).reshape(1, LANES)
    band = _bias_tables(rel_bias)
    lam_rows = jnp.stack([lambda_q1[LAYER], lambda_k1[LAYER], lambda_q2[LAYER], lambda_k2[LAYER]])
    gsub = jnp.broadcast_to(g_subln[LAYER][:, None], (2 * HD_A, TQ))
    wo = w_out[LAYER].astype(BF16)
    woa, wob = wo[:A_V], wo[A_V:]

    g2 = g_norm2[LAYER].reshape(1, d)
    wrT = w_router[LAYER].T.astype(BF16)
    rbias = router_bias[LAYER].reshape(N_EXPERTS, 1)
    wsgu = jnp.concatenate([w_sh_gate[LAYER], w_sh_up[LAYER]], axis=1).astype(BF16)
    wsd = w_sh_down[LAYER].astype(BF16)
    gf = g_final.reshape(1, d)

    def layer(x, nb, s, b_off):
        t = nb * s
        x2d = x.reshape(t, d)
        cT, sT, cN, sN = _rope_tables(s)
        qT, k2, vT4 = _inproj(x2d, nb, s, b_off, mod, g1, wqT, wk, wvT, gqT, gk, cT, sT, cN, sN)
        k3 = k2.reshape(nb, s, K_COLS)
        oa = _attention(qT, k3, vT4, True, (band, lam_rows, gsub)).reshape(t, A_V)
        ob = _attention(qT, k3, vT4, False, ()).reshape(t, B_Q)
        x1, h2p, logitsT = _outproj(x2d, oa, ob, s, b_off, mod, woa, wob, g2, wrT)
        idxT, gateT, rankT, cnt = _route(logitsT, rbias)

        counts = cnt[:, 0].astype(I32)
        nblk = (counts + BM - 1) // BM
        blk_end = jnp.cumsum(nblk)
        blk_start = blk_end - nblk
        n_blocks = t * TOP_K // BM + N_EXPERTS
        n_used = blk_end[-1:].astype(I32)
        bidx = jnp.arange(n_blocks, dtype=I32)[:, None]
        owns = jnp.logical_and(bidx >= blk_start[None, :], bidx < blk_end[None, :])
        pad_start = (blk_start * BM).astype(F32).reshape(N_EXPERTS, 1)
        in_blk = jnp.clip(counts[None, :] - (bidx - blk_start[None, :]) * BM, 0, BM)
        blk_valid = jnp.sum(jnp.where(owns, in_blk, 0), axis=1).astype(I32)

        destT = _dest(idxT, rankT, pad_start)
        xs = _sc_dispatch(h2p, destT, n_blocks * BM)
        y = _ffn(blk_start.astype(I32), blk_end.astype(I32), blk_valid, n_used, xs,
                 w_exp_gate[LAYER], w_exp_up[LAYER], w_exp_down[LAYER])
        yk = _sc_combine(y, destT, 0, t)
        out = _final(x1, h2p, yk, gateT.T, mod, wsgu, wsd, gf, 0, nb, s, b_off)
        return out.reshape(nb, s, d)

    return (layer(x_prompt, bp, sp, 0), layer(x_sample, bs, ss, bp))
```

```python
import functools
import math

import jax
import jax.numpy as jnp
import numpy as np
from jax import lax
from jax.experimental import pallas as pl
from jax.experimental.pallas import tpu as pltpu
from jax.experimental.pallas import tpu_sc as plsc

F32 = jnp.float32
BF16 = jnp.bfloat16
I32 = jnp.int32

D_MODEL = 1024
GRID_W = 64
HA = 4
HD_A = 64
HB = 8
KV_B = 2
HD_B = 64
ROPE_THETA = 10000.0
N_BUCKETS = 32
MAX_DISTANCE = 128
A_Q = HA * 2 * HD_A
A_K = HA * 2 * HD_A
A_V = HA * 2 * HD_A
B_Q = HB * HD_B
B_K = KV_B * HD_B
B_V = KV_B * HD_B
N_EXPERTS = 256
TOP_K = 8
N_GROUPS = 8
TOPK_GROUPS = 4
D_EXPERT = 256
D_SHARED = 256
ROUTE_SCALE = 2.5
EPS = 1e-6
LAYER = 0
LAM_INIT = 0.8 - 0.6 * math.exp(-0.3 * LAYER)

LANES = 128
SUBLANES = 8
V7X_VMEM_BYTES = 64 * 1024 * 1024

TM = 512
TQ = 256
TK = 256
KSUB = 4
MAP_ROWS = 2 * HD_A
ONES_ROWS = 16
N_MAPS = 2 * HA + HB
K_COLS = A_K + B_K
V_ROWS = A_V + B_V
BM = 256
FFN_AHEAD = 6
FFN_OUT = 4
TF = 256
HALF = D_MODEL // 2
SC_CHUNK = 128
LOG2E = math.log2(math.e)
Q_SCALE = (HD_A ** -0.5) * LOG2E
NT_DIMS = (((1,), (1,)), ((), ()))
NEG_BIG = -1e30


def _cparams(semantics, vmem_mb):
    assert vmem_mb * 1024 * 1024 <= V7X_VMEM_BYTES
    return pltpu.CompilerParams(dimension_semantics=semantics, vmem_limit_bytes=vmem_mb * 1024 * 1024)


def _rms(x, axis):
    return x * lax.rsqrt(jnp.mean(x * x, axis=axis, keepdims=True) + EPS)


def _silu(x):
    return x * jax.nn.sigmoid(x)


def _pack_row(x):
    hi = pltpu.bitcast(x[:, :HALF].astype(BF16).astype(F32), I32)
    lo = pltpu.bitcast(x[:, HALF:].astype(BF16).astype(F32), I32)
    return hi | lax.shift_right_logical(lo, 16)


def _unpack_halves(p):
    hi = pltpu.bitcast(p & jnp.int32(-65536), F32)
    lo = pltpu.bitcast(lax.shift_left(p, 16), F32)
    return hi, lo


def _ada_kernel(c_ref, w_ref, b_ref, o_ref):
    a = _silu(c_ref[...])
    o_ref[...] = jnp.dot(a, w_ref[...], preferred_element_type=F32) + b_ref[...]


def _ada(c_all, w, b):
    nb, d = c_all.shape
    n = w.shape[1]
    tn = 768
    return pl.pallas_call(
        _ada_kernel,
        grid=(n // tn,),
        in_specs=[pl.BlockSpec((nb, d), lambda j: (0, 0)),
                  pl.BlockSpec((d, tn), lambda j: (0, j)),
                  pl.BlockSpec((1, tn), lambda j: (0, j))],
        out_specs=pl.BlockSpec((nb, tn), lambda j: (0, j)),
        out_shape=jax.ShapeDtypeStruct((nb, n), F32),
        compiler_params=_cparams(("parallel",), 24),
        name="ada",
    )(c_all, w, b.reshape(1, n))


def _inproj_kernel(x_ref, mod_ref, g1_ref, wqT_ref, wk_ref, wvT_ref, gqT_ref, gk_ref,
                   cT_ref, sT_ref, cN_ref, sN_ref, qT_ref, k_ref, vT_ref):
    x = x_ref[...]
    tm = x.shape[0]
    shift = mod_ref[0, 0:1, :]
    scale = mod_ref[0, 1:2, :]
    h = (_rms(x, -1) * g1_ref[...]) * (1.0 + scale) + shift
    hb = h.astype(BF16)
    qT = lax.dot_general(wqT_ref[...], hb, NT_DIMS, preferred_element_type=F32)
    kn = jnp.dot(hb, wk_ref[...], preferred_element_type=F32)
    vT = lax.dot_general(wvT_ref[...], hb, NT_DIMS, preferred_element_type=F32)

    vTb = vT.astype(BF16)
    for c in range(tm // TK):
        vT_ref[0, c] = vTb[:, c * TK:(c + 1) * TK]

    zeros = jnp.zeros((HD_A, tm), BF16)

    def put_map(m, q, upper):
        r0 = m * MAP_ROWS
        if upper:
            qT_ref[0, r0:r0 + HD_A, :] = zeros
            qT_ref[0, r0 + HD_A:r0 + MAP_ROWS, :] = q
        else:
            qT_ref[0, r0:r0 + HD_A, :] = q
            qT_ref[0, r0 + HD_A:r0 + MAP_ROWS, :] = zeros

    for m in range(2 * HA):
        q = (qT[m * HD_A:(m + 1) * HD_A, :] * Q_SCALE).astype(BF16)
        put_map(m, q, upper=(m % 2 == 1))

    cT = cT_ref[...]
    sT = sT_ref[...]
    gq = gqT_ref[...]
    for j in range(HB):
        xh = qT[A_Q + j * HD_B:A_Q + (j + 1) * HD_B, :]
        y = _rms(xh, 0) * gq
        yp = jnp.concatenate([y[16:32], y[0:16], y[48:64], y[32:48]], axis=0)
        r = (y * cT + yp * sT) * Q_SCALE
        put_map(2 * HA + j, r.astype(BF16), upper=(j // (HB // KV_B) == 1))

    kb = kn[:, A_K:K_COLS]
    lane = lax.broadcasted_iota(I32, kb.shape, 1)
    first = lane < HD_B
    x2 = kb * kb
    s0 = jnp.sum(jnp.where(first, x2, 0.0), axis=-1, keepdims=True)
    s1 = jnp.sum(jnp.where(first, 0.0, x2), axis=-1, keepdims=True)
    ms = jnp.where(first, s0, s1) * (1.0 / HD_B)
    y = kb * lax.rsqrt(ms + EPS) * gk_ref[...]
    partner = jnp.where((lane % 32) < 16, pltpu.roll(y, LANES - 16, 1), pltpu.roll(y, 16, 1))
    r = y * cN_ref[...] + partner * sN_ref[...]
    k_ref[:, 0:A_K] = kn[:, 0:A_K].astype(BF16)
    k_ref[:, A_K:K_COLS] = r.astype(BF16)


def _inproj(x2d, nb, s, b_off, mod, g1, wqT, wk, wvT, gqT, gk, cT, sT, cN, sN):
    nps = s // TM
    const = lambda i: (0, 0)
    return pl.pallas_call(
        _inproj_kernel,
        grid=(nb * nps,),
        in_specs=[
            pl.BlockSpec((TM, D_MODEL), lambda i: (i, 0)),
            pl.BlockSpec((1, 6, D_MODEL), lambda i: (b_off + i // nps, 0, 0)),
            pl.BlockSpec((1, D_MODEL), const),
            pl.BlockSpec((A_Q + B_Q, D_MODEL), const),
            pl.BlockSpec((D_MODEL, K_COLS), const),
            pl.BlockSpec((V_ROWS, D_MODEL), const),
            pl.BlockSpec((HD_B, TM), const),
            pl.BlockSpec((1, LANES), const),
            pl.BlockSpec((HD_B, TM), lambda i: (0, i % nps)),
            pl.BlockSpec((HD_B, TM), lambda i: (0, i % nps)),
            pl.BlockSpec((TM, LANES), lambda i: (i % nps, 0)),
            pl.BlockSpec((TM, LANES), lambda i: (i % nps, 0)),
        ],
        out_specs=[
            pl.BlockSpec((1, N_MAPS * MAP_ROWS, TM), lambda i: (i // nps, 0, i % nps)),
            pl.BlockSpec((TM, K_COLS), lambda i: (i, 0)),
            pl.BlockSpec((1, TM // TK, V_ROWS, TK), lambda i: (i // nps, i % nps, 0, 0)),
        ],
        out_shape=[
            jax.ShapeDtypeStruct((nb, N_MAPS * MAP_ROWS, s), BF16),
            jax.ShapeDtypeStruct((nb * s, K_COLS), BF16),
            jax.ShapeDtypeStruct((nb, s // TK, V_ROWS, TK), BF16),
        ],
        compiler_params=_cparams(("parallel",), 48),
        name="inproj",
    )(x2d, mod, g1, wqT, wk, wvT, gqT, gk, cT, sT, cN, sN)


def _attn_kernel(*refs, is_diff, n_big, ksub):
    if is_diff:
        qT_ref, k_ref, vT_ref, band_ref, lam_ref, gsub_ref, o_ref, s_a, s_b = refs
    else:
        qT_ref, k_ref, vT_ref, o_ref, s_a, s_b = refs
    qi = pl.program_id(2)
    tq = qT_ref.shape[2]
    dv = vT_ref.shape[2]
    q2 = jnp.concatenate([qT_ref[0, 0:MAP_ROWS, :], qT_ref[0, MAP_ROWS:2 * MAP_ROWS, :]], axis=1)

    def score_stage(g, s_buf):
        cm = None
        for j in range(ksub):
            c = g * ksub + j
            kc = k_ref[0, pl.ds(pl.multiple_of(c * TK, TK), TK), :]
            sj = jnp.dot(kc, q2, preferred_element_type=F32)
            if is_diff:
                bias = band_ref[0, jnp.clip(c - qi + 2, 0, 4)]
                sj = sj + jnp.concatenate([bias, bias], axis=1)
            s_buf[j * TK:(j + 1) * TK, :] = sj
            cj = jnp.max(sj, axis=0, keepdims=True)
            cm = cj if cm is None else jnp.maximum(cm, cj)
        return cm

    ones = jnp.ones((ONES_ROWS, TK), BF16)

    def softmax_stage(g, s_buf, cm, carry):
        m, acc = carry
        m_new = jnp.maximum(m, cm)
        acc = jnp.exp2(m - m_new) * acc
        for j in range(ksub):
            d = s_buf[j * TK:(j + 1) * TK, :] - m_new
            vT = vT_ref[0, g * ksub + j]
            if is_diff:
                p = jnp.exp2(d.astype(BF16))
                acc = acc + jnp.dot(jnp.concatenate([vT, ones], axis=0), p, preferred_element_type=F32)
            else:
                p = jnp.exp2(d)
                pv = jnp.dot(vT, p.astype(BF16), preferred_element_type=F32)
                lsum = jnp.broadcast_to(jnp.sum(p, axis=0, keepdims=True), (ONES_ROWS, p.shape[1]))
                acc = acc + jnp.concatenate([pv, lsum], axis=0)
        return m_new, acc

    def pair(i, state):
        carry, cm_a = state
        g = 2 * i
        cm_b = score_stage(g + 1, s_b)
        carry = softmax_stage(g, s_a, cm_a, carry)
        cm_a = score_stage(g + 2, s_a)
        carry = softmax_stage(g + 1, s_b, cm_b, carry)
        return carry, cm_a

    init = (jnp.full((1, 2 * tq), NEG_BIG, F32), jnp.zeros((dv + ONES_ROWS, 2 * tq), F32))
    carry, cm_a = lax.fori_loop(0, n_big // 2 - 1, pair, (init, score_stage(0, s_a)))
    cm_b = score_stage(n_big - 1, s_b)
    carry = softmax_stage(n_big - 2, s_a, cm_a, carry)
    _, acc = softmax_stage(n_big - 1, s_b, cm_b, carry)
    o2 = acc[0:dv, :] * (1.0 / acc[dv:dv + 1, :])
    outs = [o2[:, 0:tq], o2[:, tq:2 * tq]]

    if is_diff:
        lv = lam_ref[...]
        lam = (jnp.exp(jnp.sum(lv[0:1] * lv[1:2], axis=-1, keepdims=True))
               - jnp.exp(jnp.sum(lv[2:3] * lv[3:4], axis=-1, keepdims=True)) + LAM_INIT)
        o = outs[0] - lam * outs[1]
        o = _rms(o, 0) * gsub_ref[...] * (1.0 - LAM_INIT)
    else:
        o = jnp.concatenate(outs, axis=0)
    o_ref[0] = o.T.astype(BF16)


def _attention(qT, k3, vT4, is_diff, extra):
    nb, s, _ = k3.shape
    nk = s // TK
    nq = s // TQ
    if is_diff:
        q_map = lambda b, u, qi: (b, u, qi)
        k_map = lambda b, u, qi: (b, 0, u)
        v_spec = pl.BlockSpec((1, nk, 2 * HD_A, TK), lambda b, u, qi: (b, 0, u, 0))
        extra_specs = [
            pl.BlockSpec((1, 5, TK, TQ), lambda b, u, qi: (u, 0, 0, 0)),
            pl.BlockSpec((4, HD_A), lambda b, u, qi: (0, 0)),
            pl.BlockSpec((2 * HD_A, TQ), lambda b, u, qi: (0, 0)),
        ]
        n_units = HA
    else:
        first = (2 * HA * MAP_ROWS) // (2 * MAP_ROWS)
        q_map = lambda b, u, qi: (b, first + u, qi)
        k_map = lambda b, u, qi: (b, 0, A_K // LANES)
        v0 = A_V // HD_B
        per_kv = (HB // KV_B) // 2
        v_spec = pl.BlockSpec((1, nk, HD_B, TK), lambda b, u, qi: (b, 0, v0 + u // per_kv, 0))
        extra_specs = []
        n_units = HB // 2
    ksub = KSUB if nk >= 4 * KSUB else KSUB // 2
    n_big = nk // ksub
    assert n_big * ksub == nk and n_big % 2 == 0
    return pl.pallas_call(
        functools.partial(_attn_kernel, is_diff=is_diff, n_big=n_big, ksub=ksub),
        grid=(nb, n_units, nq),
        in_specs=[pl.BlockSpec((1, 2 * MAP_ROWS, TQ), q_map),
                  pl.BlockSpec((1, s, LANES), k_map),
                  v_spec] + extra_specs,
        out_specs=pl.BlockSpec((1, TQ, LANES), lambda b, u, qi: (b, qi, u)),
        out_shape=jax.ShapeDtypeStruct((nb, s, n_units * LANES), BF16),
        scratch_shapes=[pltpu.VMEM((ksub * TK, 2 * TQ), F32)] * 2,
        compiler_params=_cparams(("parallel", "parallel", "parallel"), 40),
        name="attn_diff" if is_diff else "attn_gqa",
    )(qT, k3, vT4, *extra)


def _outproj_kernel(x_ref, aa_ref, ab_ref, mod_ref, woa_ref, wob_ref, g2_ref, wrT_ref, x1_ref, h2_ref, lg_ref):
    att = (jnp.dot(aa_ref[...], woa_ref[...], preferred_element_type=F32)
           + jnp.dot(ab_ref[...], wob_ref[...], preferred_element_type=F32))
    x1 = x_ref[...] + mod_ref[0, 2:3, :] * att
    h2 = (_rms(x1, -1) * g2_ref[...]) * (1.0 + mod_ref[0, 4:5, :]) + mod_ref[0, 3:4, :]
    x1_ref[...] = x1
    h2_ref[...] = _pack_row(h2)
    lg_ref[...] = lax.dot_general(wrT_ref[...], h2.astype(BF16), NT_DIMS, preferred_element_type=F32)


def _outproj(x2d, aa, ab, s, b_off, mod, woa, wob, g2, wrT):
    t = x2d.shape[0]
    tpb = s // TM
    const = lambda i: (0, 0)
    row = lambda i: (i, 0)
    return pl.pallas_call(
        _outproj_kernel,
        grid=(t // TM,),
        in_specs=[
            pl.BlockSpec((TM, D_MODEL), row),
            pl.BlockSpec((TM, A_V), row),
            pl.BlockSpec((TM, B_Q), row),
            pl.BlockSpec((1, 6, D_MODEL), lambda i: (b_off + i // tpb, 0, 0)),
            pl.BlockSpec((A_V, D_MODEL), const),
            pl.BlockSpec((B_Q, D_MODEL), const),
            pl.BlockSpec((1, D_MODEL), const),
            pl.BlockSpec((N_EXPERTS, D_MODEL), const),
        ],
        out_specs=[
            pl.BlockSpec((TM, D_MODEL), lambda i: (i, 0)),
            pl.BlockSpec((TM, HALF), lambda i: (i, 0)),
            pl.BlockSpec((N_EXPERTS, TM), lambda i: (0, i)),
        ],
        out_shape=[
            jax.ShapeDtypeStruct((t, D_MODEL), F32),
            jax.ShapeDtypeStruct((t, HALF), I32),
            jax.ShapeDtypeStruct((N_EXPERTS, t), F32),
        ],
        compiler_params=_cparams(("parallel",), 48),
        name="outproj",
    )(x2d, aa, ab, mod, woa, wob, g2, wrT)


def _route_kernel(lg_ref, rb_ref, idx_ref, gate_ref, rank_ref, cnt_ref):
    i = pl.program_id(0)

    @pl.when(i == 0)
    def _():
        cnt_ref[...] = jnp.zeros_like(cnt_ref)

    scores = jax.nn.sigmoid(lg_ref[...])
    tr = scores.shape[1]
    sel = scores + rb_ref[...]
    gsz = N_EXPERTS // N_GROUPS

    rows = []
    for g in range(N_GROUPS):
        blk = sel[g * gsz:(g + 1) * gsz, :]
        m1 = jnp.max(blk, axis=0, keepdims=True)
        eq = blk == m1
        n1 = jnp.sum(jnp.where(eq, 1.0, 0.0), axis=0, keepdims=True)
        m2 = jnp.max(jnp.where(eq, -jnp.inf, blk), axis=0, keepdims=True)
        rows.append(m1 + jnp.where(n1 >= 2.0, m1, m2))
    gs = jnp.concatenate(rows, axis=0)
    gi = lax.broadcasted_iota(I32, gs.shape, 0)
    beaten = jnp.zeros(gs.shape, F32)
    for gp in range(N_GROUPS):
        row = gs[gp:gp + 1, :]
        ahead = jnp.where(row > gs, 1.0, jnp.where(jnp.logical_and(row == gs, gp < gi), 1.0, 0.0))
        beaten = beaten + ahead
    keep = jnp.where(beaten < float(TOPK_GROUPS), 1.0, 0.0)
    emask = jnp.concatenate(
        [jnp.broadcast_to(keep[g:g + 1, :], (gsz, tr)) for g in range(N_GROUPS)], axis=0) > 0.5
    masked = jnp.where(emask, sel, -jnp.inf)

    ei = lax.broadcasted_iota(I32, masked.shape, 0).astype(F32)
    idxs, ws = [], []
    for _ in range(TOP_K):
        mk = jnp.max(masked, axis=0, keepdims=True)
        ik = jnp.min(jnp.where(masked == mk, ei, float(N_EXPERTS)), axis=0, keepdims=True)
        hit = ei == ik
        ws.append(jnp.sum(jnp.where(hit, scores, 0.0), axis=0, keepdims=True))
        masked = jnp.where(hit, -jnp.inf, masked)
        idxs.append(ik)
    wsum = ws[0]
    for w in ws[1:]:
        wsum = wsum + w
    gate_ref[...] = jnp.concatenate([w / wsum * ROUTE_SCALE for w in ws], axis=0)
    idx_ref[...] = jnp.concatenate(idxs, axis=0)

    chosen = jnp.logical_and(emask, masked == -jnp.inf)
    onehot = jnp.where(chosen, 1.0, 0.0)
    before = (lax.broadcasted_iota(I32, (tr, tr), 0) < lax.broadcasted_iota(I32, (tr, tr), 1))
    upper = jnp.where(before, 1.0, 0.0).astype(BF16)
    prior = jnp.dot(onehot.astype(BF16), upper, preferred_element_type=F32) + cnt_ref[:, 0:1]
    rank_ref[...] = jnp.concatenate(
        [jnp.sum(jnp.where(ei == ik, prior, 0.0), axis=0, keepdims=True) for ik in idxs], axis=0)
    cnt_ref[...] = cnt_ref[...] + jnp.sum(onehot, axis=1, keepdims=True)


def _route(logitsT, rbias):
    e, t = logitsT.shape
    row = pl.BlockSpec((TOP_K, TM), lambda i: (0, i))
    return pl.pallas_call(
        _route_kernel,
        grid=(t // TM,),
        in_specs=[pl.BlockSpec((e, TM), lambda i: (0, i)),
                  pl.BlockSpec((e, 1), lambda i: (0, 0))],
        out_specs=[row, row, row, pl.BlockSpec((e, LANES), lambda i: (0, 0))],
        out_shape=[jax.ShapeDtypeStruct((TOP_K, t), F32),
                   jax.ShapeDtypeStruct((TOP_K, t), F32),
                   jax.ShapeDtypeStruct((TOP_K, t), F32),
                   jax.ShapeDtypeStruct((e, LANES), F32)],
        compiler_params=_cparams(("arbitrary",), 32),
        name="route",
    )(logitsT, rbias)


def _dest_kernel(idx_ref, rank_ref, ps_ref, dest_ref):
    ei = lax.broadcasted_iota(I32, (N_EXPERTS, idx_ref.shape[1]), 0).astype(F32)
    ps = ps_ref[...]
    rows = []
    for k in range(TOP_K):
        hit = ei == idx_ref[k:k + 1, :]
        rows.append(jnp.sum(jnp.where(hit, ps, 0.0), axis=0, keepdims=True) + rank_ref[k:k + 1, :])
    dest_ref[...] = jnp.concatenate(rows, axis=0).astype(I32)


def _dest(idxT, rankT, pad_start):
    t = idxT.shape[1]
    row = pl.BlockSpec((TOP_K, TM), lambda i: (0, i))
    return pl.pallas_call(
        _dest_kernel,
        grid=(t // TM,),
        in_specs=[row, row, pl.BlockSpec((N_EXPERTS, 1), lambda i: (0, 0))],
        out_specs=row,
        out_shape=jax.ShapeDtypeStruct((TOP_K, t), I32),
        compiler_params=_cparams(("parallel",), 24),
        name="dest",
    )(idxT, rankT, pad_start)


def _sc_workers(mesh, n_tokens):
    n_workers = mesh.num_cores * mesh.num_subcores
    per_worker = n_tokens // n_workers
    assert per_worker * n_workers == n_tokens and per_worker % SC_CHUNK == 0
    return per_worker


def _sc_scratch(width):
    return ([pltpu.VMEM((SC_CHUNK,), I32)] * TOP_K
            + [pltpu.VMEM((SC_CHUNK, width), I32), pltpu.SemaphoreType.DMA])


def _sc_dispatch(rows, destT, n_slots):
    t, width = rows.shape
    mesh = plsc.VectorSubcoreMesh(core_axis_name="c", subcore_axis_name="s")
    per_worker = _sc_workers(mesh, t)

    @functools.partial(pl.kernel, mesh=mesh, out_type=jax.ShapeDtypeStruct((n_slots, width), I32),
                       scratch_types=_sc_scratch(width))
    def scatter_rows(rows_hbm, dest_hbm, xs_hbm, *scratch):
        idx, buf, sem = scratch[:TOP_K], scratch[TOP_K], scratch[TOP_K + 1]
        base = (lax.axis_index("s") * mesh.num_cores + lax.axis_index("c")) * per_worker

        @pl.loop(0, per_worker // SC_CHUNK)
        def _(i):
            t0 = pl.multiple_of(base + i * SC_CHUNK, SC_CHUNK)
            for k in range(TOP_K):
                pltpu.sync_copy(dest_hbm.at[k, pl.ds(t0, SC_CHUNK)], idx[k])
            pltpu.sync_copy(rows_hbm.at[pl.ds(t0, SC_CHUNK)], buf)
            copies = [pltpu.async_copy(buf, xs_hbm.at[idx[k]], sem) for k in range(TOP_K)]
            for c in copies:
                c.wait()

    return scatter_rows(rows, destT)


def _sc_combine(y, destT, tok_off, n_tok):
    width = y.shape[1]
    mesh = plsc.VectorSubcoreMesh(core_axis_name="c", subcore_axis_name="s")
    per_worker = _sc_workers(mesh, n_tok)
    half = SC_CHUNK // 2

    @functools.partial(pl.kernel, mesh=mesh, out_type=jax.ShapeDtypeStruct((TOP_K, n_tok, width), I32),
                       scratch_types=([pltpu.VMEM((half,), I32)] * TOP_K + [pltpu.VMEM((half, width), I32)] * 2
                                      + [pltpu.SemaphoreType.DMA] * 4))
    def gather_rows(y_hbm, dest_hbm, out_hbm, *scratch):
        idx, bufs = scratch[:TOP_K], scratch[TOP_K:TOP_K + 2]
        gsem, wsem = scratch[TOP_K + 2:TOP_K + 4], scratch[TOP_K + 4:TOP_K + 6]
        base = (lax.axis_index("s") * mesh.num_cores + lax.axis_index("c")) * per_worker

        @pl.loop(0, per_worker // half)
        def _(i):
            t0 = pl.multiple_of(base + i * half, half)
            for k in range(TOP_K):
                pltpu.sync_copy(dest_hbm.at[k, pl.ds(tok_off + t0, half)], idx[k])
            gathers = [None] * TOP_K
            writes = [None] * TOP_K
            gathers[0] = pltpu.async_copy(y_hbm.at[idx[0]], bufs[0], gsem[0])
            for k in range(TOP_K):
                if k + 1 < TOP_K:
                    if k >= 1:
                        writes[k - 1].wait()
                    gathers[k + 1] = pltpu.async_copy(y_hbm.at[idx[k + 1]], bufs[(k + 1) % 2], gsem[(k + 1) % 2])
                gathers[k].wait()
                writes[k] = pltpu.async_copy(bufs[k % 2], out_hbm.at[k, pl.ds(t0, half)], wsem[k % 2])
            writes[TOP_K - 2].wait()
            writes[TOP_K - 1].wait()

    return gather_rows(y, destT)


def _ffn_kernel(bs_ref, be_ref, bv_ref, nu_ref, wg_ref, wu_ref, wd_ref, xs_hbm, y_hbm,
                wgu_sc, wd_sc, xbuf, ybuf, lsem, ssem):
    e = pl.program_id(0)
    n_used = nu_ref[0]
    ring = FFN_AHEAD + 1

    def load(g):
        return pltpu.make_async_copy(xs_hbm.at[pl.ds(pl.multiple_of(g * BM, BM), BM)], xbuf.at[g % ring],
                                     lsem.at[g % ring])

    def store(g):
        return pltpu.make_async_copy(ybuf.at[g % FFN_OUT], y_hbm.at[pl.ds(pl.multiple_of(g * BM, BM), BM)],
                                     ssem.at[g % FFN_OUT])

    @pl.when(e == 0)
    def _():
        for g0 in range(FFN_AHEAD):
            @pl.when(g0 < n_used)
            def _():
                load(g0).start()

    wgu_sc[:, 0:D_EXPERT] = wg_ref[0].astype(BF16)
    wgu_sc[:, D_EXPERT:2 * D_EXPERT] = wu_ref[0].astype(BF16)
    wd_sc[...] = wd_ref[0].astype(BF16)

    def block(g, c):
        load(g).wait()

        @pl.when(g + FFN_AHEAD < n_used)
        def _():
            load(g + FFN_AHEAD).start()

        row = lax.broadcasted_iota(I32, (BM, 1), 0)
        hi, lo = _unpack_halves(jnp.where(row < bv_ref[g], xbuf[g % ring], 0))
        x = jnp.concatenate([hi, lo], axis=1).astype(BF16)
        gu = jnp.dot(x, wgu_sc[...], preferred_element_type=F32)
        a = (_silu(gu[:, 0:D_EXPERT]) * gu[:, D_EXPERT:2 * D_EXPERT]).astype(BF16)
        y = _pack_row(jnp.dot(a, wd_sc[...], preferred_element_type=F32))

        @pl.when(g >= FFN_OUT)
        def _():
            store(g - FFN_OUT).wait()

        ybuf[g % FFN_OUT] = y
        store(g).start()
        return c

    lax.fori_loop(bs_ref[e], be_ref[e], block, 0)

    @pl.when(e == pl.num_programs(0) - 1)
    def _():
        for back in range(FFN_OUT, 0, -1):
            @pl.when(n_used >= back)
            def _():
                store(n_used - back).wait()


def _ffn(blk_start, blk_end, blk_valid, n_used, xs, w_eg, w_eu, w_ed):
    n_slots = xs.shape[0]
    weights = lambda e, bs, be, bv, nu: (e, 0, 0)
    return pl.pallas_call(
        _ffn_kernel,
        grid_spec=pltpu.PrefetchScalarGridSpec(
            num_scalar_prefetch=4,
            grid=(N_EXPERTS,),
            in_specs=[
                pl.BlockSpec((1, D_MODEL, D_EXPERT), weights),
                pl.BlockSpec((1, D_MODEL, D_EXPERT), weights),
                pl.BlockSpec((1, D_EXPERT, D_MODEL), weights),
                pl.BlockSpec(memory_space=pl.ANY),
            ],
            out_specs=pl.BlockSpec(memory_space=pl.ANY),
            scratch_shapes=[pltpu.VMEM((D_MODEL, 2 * D_EXPERT), BF16),
                            pltpu.VMEM((D_EXPERT, D_MODEL), BF16),
                            pltpu.VMEM((FFN_AHEAD + 1, BM, HALF), I32),
                            pltpu.VMEM((FFN_OUT, BM, HALF), I32),
                            pltpu.SemaphoreType.DMA((FFN_AHEAD + 1,)),
                            pltpu.SemaphoreType.DMA((FFN_OUT,))],
        ),
        out_shape=jax.ShapeDtypeStruct((n_slots, HALF), I32),
        compiler_params=_cparams(("arbitrary",), 32),
        name="ffn",
    )(blk_start, blk_end, blk_valid, n_used, w_eg, w_eu, w_ed, xs)


def _final_kernel(x1_ref, h2_ref, yk_ref, gate_ref, mod_ref, wsgu_ref, wsd_ref, gf_ref, o_ref):
    hi, lo = _unpack_halves(h2_ref[...])
    hb = jnp.concatenate([hi, lo], axis=1).astype(BF16)
    gu = jnp.dot(hb, wsgu_ref[...], preferred_element_type=F32)
    a = (_silu(gu[:, 0:D_SHARED]) * gu[:, D_SHARED:2 * D_SHARED]).astype(BF16)
    shared = jnp.dot(a, wsd_ref[...], preferred_element_type=F32)

    gate = gate_ref[...]
    acc_hi = acc_lo = None
    for k in range(TOP_K):
        hi, lo = _unpack_halves(yk_ref[k])
        g = gate[:, k:k + 1]
        acc_hi = g * hi if acc_hi is None else acc_hi + g * hi
        acc_lo = g * lo if acc_lo is None else acc_lo + g * lo
    routed = jnp.concatenate([acc_hi, acc_lo], axis=1)
    x2 = x1_ref[...] + mod_ref[0, 5:6, :] * (routed + shared)
    o_ref[...] = _rms(x2, -1) * gf_ref[...]


def _final(x1, h2p, yk, gates, mod, wsgu, wsd, gf, tok_off, nb, s, b_off):
    n_tiles = nb * s // TF
    t_off = tok_off // TF
    tpb = s // TF
    const = lambda i: (0, 0)
    return pl.pallas_call(
        _final_kernel,
        grid=(n_tiles,),
        in_specs=[
            pl.BlockSpec((TF, D_MODEL), lambda i: (t_off + i, 0)),
            pl.BlockSpec((TF, HALF), lambda i: (t_off + i, 0)),
            pl.BlockSpec((TOP_K, TF, HALF), lambda i: (0, i, 0)),
            pl.BlockSpec((TF, TOP_K), lambda i: (t_off + i, 0)),
            pl.BlockSpec((1, 6, D_MODEL), lambda i: (b_off + i // tpb, 0, 0)),
            pl.BlockSpec((D_MODEL, 2 * D_SHARED), const),
            pl.BlockSpec((D_SHARED, D_MODEL), const),
            pl.BlockSpec((1, D_MODEL), const),
        ],
        out_specs=pl.BlockSpec((TF, D_MODEL), lambda i: (i, 0)),
        out_shape=jax.ShapeDtypeStruct((nb * s, D_MODEL), F32),
        compiler_params=_cparams(("parallel",), 40),
        name="final",
    )(x1, h2p, yk, gates, mod, wsgu, wsd, gf)


def _t5_bucket(rel):
    nb = N_BUCKETS // 2
    max_exact = nb // 2
    ret = jnp.where(rel > 0, nb, 0)
    n = jnp.abs(rel)
    nf = jnp.maximum(n, 1).astype(F32)
    large = max_exact + (jnp.log(nf / max_exact) / math.log(MAX_DISTANCE / max_exact)
                         * (nb - max_exact)).astype(I32)
    large = jnp.minimum(large, nb - 1)
    return ret + jnp.where(n < max_exact, n, large)


def _bias_tables(rel_bias):
    n_far = np.arange(TK + 1, 1 << 20, dtype=np.float64)
    assert np.all(N_BUCKETS // 4 + np.log(n_far / (N_BUCKETS // 4)) / math.log(MAX_DISTANCE / (N_BUCKETS // 4))
                  * (N_BUCKETS // 4) >= N_BUCKETS // 2), "relative-position buckets must saturate beyond one chunk"
    kk = jnp.arange(TK, dtype=I32)[:, None]
    qq = jnp.arange(TQ, dtype=I32)[None, :]
    rel = jnp.stack([(dd - 2) * TK + kk - qq for dd in range(5)])
    onehot = jax.nn.one_hot(_t5_bucket(rel), N_BUCKETS, dtype=F32)
    band = jnp.einsum("dkqb,bh->hdkq", onehot, rel_bias.astype(F32), precision=lax.Precision.HIGHEST)
    return band * LOG2E


def _rope_tables(s):
    rows = s // GRID_W
    row_id = jnp.repeat(jnp.arange(rows, dtype=F32), GRID_W)
    col_id = jnp.tile(jnp.arange(GRID_W, dtype=F32), rows)
    half = HD_B // 2
    inv = ROPE_THETA ** (-jnp.arange(0, half, 2, dtype=F32) / half)
    ang_r = row_id[:, None] * inv[None, :]
    ang_c = col_id[:, None] * inv[None, :]
    cr, sr, cc, sc = jnp.cos(ang_r), jnp.sin(ang_r), jnp.cos(ang_c), jnp.sin(ang_c)
    c64 = jnp.concatenate([cr, cr, cc, cc], axis=1)
    s64 = jnp.concatenate([-sr, sr, -sc, sc], axis=1)
    return c64.T, s64.T, jnp.concatenate([c64, c64], axis=1), jnp.concatenate([s64, s64], axis=1)


def kernel(x_prompt, x_sample, c_prompt, c_sample, rel_bias, w_ada, b_ada, g_norm1, w_in, lambda_q1, lambda_k1, lambda_q2, lambda_k2, g_subln, g_qnorm, g_knorm, w_out, g_norm2, w_router, router_bias, w_exp_gate, w_exp_up, w_exp_down, w_sh_gate, w_sh_up, w_sh_down, g_final):
    bp, sp, d = x_prompt.shape
    bs, ss, _ = x_sample.shape
    assert d == D_MODEL and sp % TM == 0 and ss % TM == 0 and TM % TK == 0 and TQ == TK
    assert sp % (KSUB * TK) == 0 and ss % (KSUB * TK) == 0

    mod = _ada(jnp.concatenate([c_prompt, c_sample], axis=0), w_ada[LAYER], b_ada[LAYER])
    mod = mod.reshape(bp + bs, 6, d)

    w = w_in[LAYER]
    o1, o2, o3 = A_Q, A_Q + A_K, A_Q + A_K + A_V
    o4, o5 = o3 + B_Q, o3 + B_Q + B_K
    wqT = jnp.concatenate([w[:, :o1], w[:, o3:o4]], axis=1).T.astype(BF16)
    wk = jnp.concatenate([w[:, o1:o2], w[:, o4:o5]], axis=1).astype(BF16)
    wvT = jnp.concatenate([w[:, o2:o3], w[:, o5:]], axis=1).T.astype(BF16)
    g1 = g_norm1[LAYER].reshape(1, d)
    gqT = jnp.broadcast_to(g_qnorm[LAYER][:, None], (HD_B, TM))
    gk = jnp.tile(g_knorm[LAYER], KV_B).reshape(1, LANES)
    band = _bias_tables(rel_bias)
    lam_rows = jnp.stack([lambda_q1[LAYER], lambda_k1[LAYER], lambda_q2[LAYER], lambda_k2[LAYER]])
    gsub = jnp.broadcast_to(g_subln[LAYER][:, None], (2 * HD_A, TQ))
    wo = w_out[LAYER].astype(BF16)
    woa, wob = wo[:A_V], wo[A_V:]

    g2 = g_norm2[LAYER].reshape(1, d)
    wrT = w_router[LAYER].T.astype(BF16)
    rbias = router_bias[LAYER].reshape(N_EXPERTS, 1)
    wsgu = jnp.concatenate([w_sh_gate[LAYER], w_sh_up[LAYER]], axis=1).astype(BF16)
    wsd = w_sh_down[LAYER].astype(BF16)
    gf = g_final.reshape(1, d)

    def layer(x, nb, s, b_off):
        t = nb * s
        x2d = x.reshape(t, d)
        cT, sT, cN, sN = _rope_tables(s)
        qT, k2, vT4 = _inproj(x2d, nb, s, b_off, mod, g1, wqT, wk, wvT, gqT, gk, cT, sT, cN, sN)
        k3 = k2.reshape(nb, s, K_COLS)
        oa = _attention(qT, k3, vT4, True, (band, lam_rows, gsub)).reshape(t, A_V)
        ob = _attention(qT, k3, vT4, False, ()).reshape(t, B_Q)
        x1, h2p, logitsT = _outproj(x2d, oa, ob, s, b_off, mod, woa, wob, g2, wrT)
        idxT, gateT, rankT, cnt = _route(logitsT, rbias)

        counts = cnt[:, 0].astype(I32)
        nblk = (counts + BM - 1) // BM
        blk_end = jnp.cumsum(nblk)
        blk_start = blk_end - nblk
        n_blocks = t * TOP_K // BM + N_EXPERTS
        n_used = blk_end[-1:].astype(I32)
        bidx = jnp.arange(n_blocks, dtype=I32)[:, None]
        owns = jnp.logical_and(bidx >= blk_start[None, :], bidx < blk_end[None, :])
        pad_start = (blk_start * BM).astype(F32).reshape(N_EXPERTS, 1)
        in_blk = jnp.clip(counts[None, :] - (bidx - blk_start[None, :]) * BM, 0, BM)
        blk_valid = jnp.sum(jnp.where(owns, in_blk, 0), axis=1).astype(I32)

        destT = _dest(idxT, rankT, pad_start)
        xs = _sc_dispatch(h2p, destT, n_blocks * BM)
        y = _ffn(blk_start.astype(I32), blk_end.astype(I32), blk_valid, n_used, xs,
                 w_exp_gate[LAYER], w_exp_up[LAYER], w_exp_down[LAYER])
        yk = _sc_combine(y, destT, 0, t)
        out = _final(x1, h2p, yk, gateT.T, mod, wsgu, wsd, gf, 0, nb, s, b_off)
        return out.reshape(nb, s, d)

    return (layer(x_prompt, bp, sp, 0), layer(x_sample, bs, ss, bp))
```

```python
import functools
import math

import jax
import jax.numpy as jnp
import numpy as np
from jax import lax
from jax.experimental import pallas as pl
from jax.experimental.pallas import tpu as pltpu
from jax.experimental.pallas import tpu_sc as plsc

F32 = jnp.float32
BF16 = jnp.bfloat16
I32 = jnp.int32

D_MODEL = 1024
GRID_W = 64
HA = 4
HD_A = 64
HB = 8
KV_B = 2
HD_B = 64
ROPE_THETA = 10000.0
N_BUCKETS = 32
MAX_DISTANCE = 128
A_Q = HA * 2 * HD_A
A_K = HA * 2 * HD_A
A_V = HA * 2 * HD_A
B_Q = HB * HD_B
B_K = KV_B * HD_B
B_V = KV_B * HD_B
N_EXPERTS = 256
TOP_K = 8
N_GROUPS = 8
TOPK_GROUPS = 4
D_EXPERT = 256
D_SHARED = 256
ROUTE_SCALE = 2.5
EPS = 1e-6
LAYER = 0
LAM_INIT = 0.8 - 0.6 * math.exp(-0.3 * LAYER)

LANES = 128
SUBLANES = 8
V7X_VMEM_BYTES = 64 * 1024 * 1024

TM = 512
TQ = 512
TQ_GQA = 1024
TK = 256
BAND_TILES = TQ // TK + 4
KSUB = 4
KSUB_SHORT = 2
MAP_ROWS = 2 * HD_A
ONES_ROWS = 16
N_MAPS = 2 * HA + HB
K_COLS = A_K + B_K
V_ROWS = A_V + B_V
BM = 256
FFN_AHEAD = 6
FFN_OUT = 4
TF = 256
HALF = D_MODEL // 2
SC_CHUNK = 128
LOG2E = math.log2(math.e)
Q_SCALE = (HD_A ** -0.5) * LOG2E
NT_DIMS = (((1,), (1,)), ((), ()))
NEG_BIG = -1e30


def _cparams(semantics, vmem_mb):
    assert vmem_mb * 1024 * 1024 <= V7X_VMEM_BYTES
    return pltpu.CompilerParams(dimension_semantics=semantics, vmem_limit_bytes=vmem_mb * 1024 * 1024)


def _rms(x, axis):
    return x * lax.rsqrt(jnp.mean(x * x, axis=axis, keepdims=True) + EPS)


def _silu(x):
    return x * jax.nn.sigmoid(x)


def _pack_row(x):
    hi = pltpu.bitcast(x[:, :HALF].astype(BF16).astype(F32), I32)
    lo = pltpu.bitcast(x[:, HALF:].astype(BF16).astype(F32), I32)
    return hi | lax.shift_right_logical(lo, 16)


def _unpack_halves(p):
    hi = pltpu.bitcast(p & jnp.int32(-65536), F32)
    lo = pltpu.bitcast(lax.shift_left(p, 16), F32)
    return hi, lo


def _ada_kernel(c_ref, w_ref, b_ref, o_ref):
    a = _silu(c_ref[...])
    o_ref[...] = jnp.dot(a, w_ref[...], preferred_element_type=F32) + b_ref[...]


def _ada(c_all, w, b):
    nb, d = c_all.shape
    n = w.shape[1]
    tn = 768
    return pl.pallas_call(
        _ada_kernel,
        grid=(n // tn,),
        in_specs=[pl.BlockSpec((nb, d), lambda j: (0, 0)),
                  pl.BlockSpec((d, tn), lambda j: (0, j)),
                  pl.BlockSpec((1, tn), lambda j: (0, j))],
        out_specs=pl.BlockSpec((nb, tn), lambda j: (0, j)),
        out_shape=jax.ShapeDtypeStruct((nb, n), F32),
        compiler_params=_cparams(("parallel",), 24),
        name="ada",
    )(c_all, w, b.reshape(1, n))


def _inproj_kernel(x_ref, mod_ref, g1_ref, wqT_ref, wk_ref, wvT_ref, gqT_ref, gk_ref,
                   cT_ref, sT_ref, cN_ref, sN_ref, qT_ref, k_ref, vT_ref):
    x = x_ref[...]
    tm = x.shape[0]
    shift = mod_ref[0, 0:1, :]
    scale = mod_ref[0, 1:2, :]
    h = (_rms(x, -1) * g1_ref[...]) * (1.0 + scale) + shift
    hb = h.astype(BF16)
    qT = lax.dot_general(wqT_ref[...], hb, NT_DIMS, preferred_element_type=F32)
    kn = jnp.dot(hb, wk_ref[...], preferred_element_type=F32)
    vT = lax.dot_general(wvT_ref[...], hb, NT_DIMS, preferred_element_type=F32)

    vTb = vT.astype(BF16)
    for c in range(tm // TK):
        vT_ref[0, c] = vTb[:, c * TK:(c + 1) * TK]

    zeros = jnp.zeros((HD_A, tm), BF16)

    def put_map(m, q, upper):
        r0 = m * MAP_ROWS
        if upper:
            qT_ref[0, r0:r0 + HD_A, :] = zeros
            qT_ref[0, r0 + HD_A:r0 + MAP_ROWS, :] = q
        else:
            qT_ref[0, r0:r0 + HD_A, :] = q
            qT_ref[0, r0 + HD_A:r0 + MAP_ROWS, :] = zeros

    for m in range(2 * HA):
        q = (qT[m * HD_A:(m + 1) * HD_A, :] * Q_SCALE).astype(BF16)
        put_map(m, q, upper=(m % 2 == 1))

    cT = cT_ref[...]
    sT = sT_ref[...]
    gq = gqT_ref[...]
    for j in range(HB):
        xh = qT[A_Q + j * HD_B:A_Q + (j + 1) * HD_B, :]
        y = _rms(xh, 0) * gq
        yp = jnp.concatenate([y[16:32], y[0:16], y[48:64], y[32:48]], axis=0)
        r = (y * cT + yp * sT) * Q_SCALE
        put_map(2 * HA + j, r.astype(BF16), upper=(j // (HB // KV_B) == 1))

    kb = kn[:, A_K:K_COLS]
    lane = lax.broadcasted_iota(I32, kb.shape, 1)
    first = lane < HD_B
    x2 = kb * kb
    s0 = jnp.sum(jnp.where(first, x2, 0.0), axis=-1, keepdims=True)
    s1 = jnp.sum(jnp.where(first, 0.0, x2), axis=-1, keepdims=True)
    ms = jnp.where(first, s0, s1) * (1.0 / HD_B)
    y = kb * lax.rsqrt(ms + EPS) * gk_ref[...]
    partner = jnp.where((lane % 32) < 16, pltpu.roll(y, LANES - 16, 1), pltpu.roll(y, 16, 1))
    r = y * cN_ref[...] + partner * sN_ref[...]
    k_ref[:, 0:A_K] = kn[:, 0:A_K].astype(BF16)
    k_ref[:, A_K:K_COLS] = r.astype(BF16)


def _inproj(x2d, nb, s, b_off, mod, g1, wqT, wk, wvT, gqT, gk, cT, sT, cN, sN):
    nps = s // TM
    const = lambda i: (0, 0)
    return pl.pallas_call(
        _inproj_kernel,
        grid=(nb * nps,),
        in_specs=[
            pl.BlockSpec((TM, D_MODEL), lambda i: (i, 0)),
            pl.BlockSpec((1, 6, D_MODEL), lambda i: (b_off + i // nps, 0, 0)),
            pl.BlockSpec((1, D_MODEL), const),
            pl.BlockSpec((A_Q + B_Q, D_MODEL), const),
            pl.BlockSpec((D_MODEL, K_COLS), const),
            pl.BlockSpec((V_ROWS, D_MODEL), const),
            pl.BlockSpec((HD_B, TM), const),
            pl.BlockSpec((1, LANES), const),
            pl.BlockSpec((HD_B, TM), lambda i: (0, i % nps)),
            pl.BlockSpec((HD_B, TM), lambda i: (0, i % nps)),
            pl.BlockSpec((TM, LANES), lambda i: (i % nps, 0)),
            pl.BlockSpec((TM, LANES), lambda i: (i % nps, 0)),
        ],
        out_specs=[
            pl.BlockSpec((1, N_MAPS * MAP_ROWS, TM), lambda i: (i // nps, 0, i % nps)),
            pl.BlockSpec((TM, K_COLS), lambda i: (i, 0)),
            pl.BlockSpec((1, TM // TK, V_ROWS, TK), lambda i: (i // nps, i % nps, 0, 0)),
        ],
        out_shape=[
            jax.ShapeDtypeStruct((nb, N_MAPS * MAP_ROWS, s), BF16),
            jax.ShapeDtypeStruct((nb * s, K_COLS), BF16),
            jax.ShapeDtypeStruct((nb, s // TK, V_ROWS, TK), BF16),
        ],
        compiler_params=_cparams(("parallel",), 48),
        name="inproj",
    )(x2d, mod, g1, wqT, wk, wvT, gqT, gk, cT, sT, cN, sN)


def _attn_kernel(*refs, is_diff, n_big, ksub):
    if is_diff:
        qT_ref, k_ref, vT_ref, band_ref, lam_ref, gsub_ref, o_ref, s_a, s_b = refs
    else:
        qT_ref, k_ref, vT_ref, o_ref, s_a, s_b = refs
    qi = pl.program_id(2)
    tq = qT_ref.shape[2]
    dv = vT_ref.shape[2]
    q2 = jnp.concatenate([qT_ref[0, 0:MAP_ROWS, :], qT_ref[0, MAP_ROWS:2 * MAP_ROWS, :]], axis=1)

    def score_stage(g, s_buf):
        cm = None
        for j in range(ksub):
            c = g * ksub + j
            kc = k_ref[0, pl.ds(pl.multiple_of(c * TK, TK), TK), :]
            sj = jnp.dot(kc, q2, preferred_element_type=F32)
            if is_diff:
                bias = band_ref[0, jnp.clip(c - (tq // TK) * qi + 2, 0, band_ref.shape[1] - 1)]
                sj = sj + jnp.concatenate([bias, bias], axis=1)
            s_buf[j * TK:(j + 1) * TK, :] = sj
            cj = jnp.max(sj, axis=0, keepdims=True)
            cm = cj if cm is None else jnp.maximum(cm, cj)
        return cm

    ones = jnp.ones((ONES_ROWS, TK), BF16)

    def softmax_stage(g, s_buf, cm, carry):
        m, acc = carry
        m_new = jnp.maximum(m, cm)
        acc = jnp.exp2(m - m_new) * acc
        for j in range(ksub):
            d = s_buf[j * TK:(j + 1) * TK, :] - m_new
            vT = vT_ref[0, g * ksub + j]
            if is_diff:
                p = jnp.exp2(d.astype(BF16))
                acc = acc + jnp.dot(jnp.concatenate([vT, ones], axis=0), p, preferred_element_type=F32)
            else:
                p = jnp.exp2(d)
                pv = jnp.dot(vT, p.astype(BF16), preferred_element_type=F32)
                lsum = jnp.broadcast_to(jnp.sum(p, axis=0, keepdims=True), (ONES_ROWS, p.shape[1]))
                acc = acc + jnp.concatenate([pv, lsum], axis=0)
        return m_new, acc

    def pair(i, state):
        carry, cm_a = state
        g = 2 * i
        cm_b = score_stage(g + 1, s_b)
        carry = softmax_stage(g, s_a, cm_a, carry)
        cm_a = score_stage(g + 2, s_a)
        carry = softmax_stage(g + 1, s_b, cm_b, carry)
        return carry, cm_a

    init = (jnp.full((1, 2 * tq), NEG_BIG, F32), jnp.zeros((dv + ONES_ROWS, 2 * tq), F32))
    carry, cm_a = lax.fori_loop(0, n_big // 2 - 1, pair, (init, score_stage(0, s_a)))
    cm_b = score_stage(n_big - 1, s_b)
    carry = softmax_stage(n_big - 2, s_a, cm_a, carry)
    _, acc = softmax_stage(n_big - 1, s_b, cm_b, carry)
    o2 = acc[0:dv, :] * (1.0 / acc[dv:dv + 1, :])
    outs = [o2[:, 0:tq], o2[:, tq:2 * tq]]

    if is_diff:
        lv = lam_ref[...]
        lam = (jnp.exp(jnp.sum(lv[0:1] * lv[1:2], axis=-1, keepdims=True))
               - jnp.exp(jnp.sum(lv[2:3] * lv[3:4], axis=-1, keepdims=True)) + LAM_INIT)
        o = outs[0] - lam * outs[1]
        o = _rms(o, 0) * gsub_ref[...] * (1.0 - LAM_INIT)
    else:
        o = jnp.concatenate(outs, axis=0)
    o_ref[0] = o.T.astype(BF16)


def _attention(qT, k3, vT4, is_diff, extra):
    nb, s, _ = k3.shape
    nk = s // TK
    tq = TQ if is_diff else TQ_GQA
    nq = s // tq
    if is_diff:
        q_map = lambda b, u, qi: (b, u, qi)
        k_map = lambda b, u, qi: (b, 0, u)
        v_spec = pl.BlockSpec((1, nk, 2 * HD_A, TK), lambda b, u, qi: (b, 0, u, 0))
        extra_specs = [
            pl.BlockSpec((1, BAND_TILES, TK, TQ), lambda b, u, qi: (u, 0, 0, 0)),
            pl.BlockSpec((4, HD_A), lambda b, u, qi: (0, 0)),
            pl.BlockSpec((2 * HD_A, TQ), lambda b, u, qi: (0, 0)),
        ]
        n_units = HA
    else:
        first = (2 * HA * MAP_ROWS) // (2 * MAP_ROWS)
        q_map = lambda b, u, qi: (b, first + u, qi)
        k_map = lambda b, u, qi: (b, 0, A_K // LANES)
        v0 = A_V // HD_B
        per_kv = (HB // KV_B) // 2
        v_spec = pl.BlockSpec((1, nk, HD_B, TK), lambda b, u, qi: (b, 0, v0 + u // per_kv, 0))
        extra_specs = []
        n_units = HB // 2
    ksub = KSUB if nk >= 4 * KSUB else KSUB_SHORT
    n_big = nk // ksub
    assert n_big * ksub == nk and n_big % 2 == 0
    return pl.pallas_call(
        functools.partial(_attn_kernel, is_diff=is_diff, n_big=n_big, ksub=ksub),
        grid=(nb, n_units, nq),
        in_specs=[pl.BlockSpec((1, 2 * MAP_ROWS, tq), q_map),
                  pl.BlockSpec((1, s, LANES), k_map),
                  v_spec] + extra_specs,
        out_specs=pl.BlockSpec((1, tq, LANES), lambda b, u, qi: (b, qi, u)),
        out_shape=jax.ShapeDtypeStruct((nb, s, n_units * LANES), BF16),
        scratch_shapes=[pltpu.VMEM((ksub * TK, 2 * tq), F32)] * 2,
        compiler_params=_cparams(("parallel", "parallel", "parallel"), 40),
        name="attn_diff" if is_diff else "attn_gqa",
    )(qT, k3, vT4, *extra)


def _outproj_kernel(x_ref, aa_ref, ab_ref, mod_ref, woa_ref, wob_ref, g2_ref, wrT_ref, x1_ref, h2_ref, lg_ref):
    att = (jnp.dot(aa_ref[...], woa_ref[...], preferred_element_type=F32)
           + jnp.dot(ab_ref[...], wob_ref[...], preferred_element_type=F32))
    x1 = x_ref[...] + mod_ref[0, 2:3, :] * att
    h2 = (_rms(x1, -1) * g2_ref[...]) * (1.0 + mod_ref[0, 4:5, :]) + mod_ref[0, 3:4, :]
    x1_ref[...] = x1
    h2_ref[...] = _pack_row(h2)
    lg_ref[...] = lax.dot_general(wrT_ref[...], h2.astype(BF16), NT_DIMS, preferred_element_type=F32)


def _outproj(x2d, aa, ab, s, b_off, mod, woa, wob, g2, wrT):
    t = x2d.shape[0]
    tpb = s // TM
    const = lambda i: (0, 0)
    row = lambda i: (i, 0)
    return pl.pallas_call(
        _outproj_kernel,
        grid=(t // TM,),
        in_specs=[
            pl.BlockSpec((TM, D_MODEL), row),
            pl.BlockSpec((TM, A_V), row),
            pl.BlockSpec((TM, B_Q), row),
            pl.BlockSpec((1, 6, D_MODEL), lambda i: (b_off + i // tpb, 0, 0)),
            pl.BlockSpec((A_V, D_MODEL), const),
            pl.BlockSpec((B_Q, D_MODEL), const),
            pl.BlockSpec((1, D_MODEL), const),
            pl.BlockSpec((N_EXPERTS, D_MODEL), const),
        ],
        out_specs=[
            pl.BlockSpec((TM, D_MODEL), lambda i: (i, 0)),
            pl.BlockSpec((TM, HALF), lambda i: (i, 0)),
            pl.BlockSpec((N_EXPERTS, TM), lambda i: (0, i)),
        ],
        out_shape=[
            jax.ShapeDtypeStruct((t, D_MODEL), F32),
            jax.ShapeDtypeStruct((t, HALF), I32),
            jax.ShapeDtypeStruct((N_EXPERTS, t), F32),
        ],
        compiler_params=_cparams(("parallel",), 48),
        name="outproj",
    )(x2d, aa, ab, mod, woa, wob, g2, wrT)


def _route_kernel(lg_ref, rb_ref, idx_ref, gate_ref, rank_ref, cnt_ref):
    i = pl.program_id(0)

    @pl.when(i == 0)
    def _():
        cnt_ref[...] = jnp.zeros_like(cnt_ref)

    scores = jax.nn.sigmoid(lg_ref[...])
    tr = scores.shape[1]
    sel = scores + rb_ref[...]
    gsz = N_EXPERTS // N_GROUPS

    rows = []
    for g in range(N_GROUPS):
        blk = sel[g * gsz:(g + 1) * gsz, :]
        m1 = jnp.max(blk, axis=0, keepdims=True)
        eq = blk == m1
        n1 = jnp.sum(jnp.where(eq, 1.0, 0.0), axis=0, keepdims=True)
        m2 = jnp.max(jnp.where(eq, -jnp.inf, blk), axis=0, keepdims=True)
        rows.append(m1 + jnp.where(n1 >= 2.0, m1, m2))
    gs = jnp.concatenate(rows, axis=0)
    gi = lax.broadcasted_iota(I32, gs.shape, 0)
    beaten = jnp.zeros(gs.shape, F32)
    for gp in range(N_GROUPS):
        row = gs[gp:gp + 1, :]
        ahead = jnp.where(row > gs, 1.0, jnp.where(jnp.logical_and(row == gs, gp < gi), 1.0, 0.0))
        beaten = beaten + ahead
    keep = jnp.where(beaten < float(TOPK_GROUPS), 1.0, 0.0)
    emask = jnp.concatenate(
        [jnp.broadcast_to(keep[g:g + 1, :], (gsz, tr)) for g in range(N_GROUPS)], axis=0) > 0.5
    masked = jnp.where(emask, sel, -jnp.inf)

    ei = lax.broadcasted_iota(I32, masked.shape, 0).astype(F32)
    idxs, ws = [], []
    for _ in range(TOP_K):
        mk = jnp.max(masked, axis=0, keepdims=True)
        ik = jnp.min(jnp.where(masked == mk, ei, float(N_EXPERTS)), axis=0, keepdims=True)
        hit = ei == ik
        ws.append(jnp.sum(jnp.where(hit, scores, 0.0), axis=0, keepdims=True))
        masked = jnp.where(hit, -jnp.inf, masked)
        idxs.append(ik)
    wsum = ws[0]
    for w in ws[1:]:
        wsum = wsum + w
    gate_ref[...] = jnp.concatenate([w / wsum * ROUTE_SCALE for w in ws], axis=0)
    idx_ref[...] = jnp.concatenate(idxs, axis=0)

    chosen = jnp.logical_and(emask, masked == -jnp.inf)
    onehot = jnp.where(chosen, 1.0, 0.0)
    before = (lax.broadcasted_iota(I32, (tr, tr), 0) < lax.broadcasted_iota(I32, (tr, tr), 1))
    upper = jnp.where(before, 1.0, 0.0).astype(BF16)
    prior = jnp.dot(onehot.astype(BF16), upper, preferred_element_type=F32) + cnt_ref[:, 0:1]
    rank_ref[...] = jnp.concatenate(
        [jnp.sum(jnp.where(ei == ik, prior, 0.0), axis=0, keepdims=True) for ik in idxs], axis=0)
    cnt_ref[...] = cnt_ref[...] + jnp.sum(onehot, axis=1, keepdims=True)


def _route(logitsT, rbias):
    e, t = logitsT.shape
    row = pl.BlockSpec((TOP_K, TM), lambda i: (0, i))
    return pl.pallas_call(
        _route_kernel,
        grid=(t // TM,),
        in_specs=[pl.BlockSpec((e, TM), lambda i: (0, i)),
                  pl.BlockSpec((e, 1), lambda i: (0, 0))],
        out_specs=[row, row, row, pl.BlockSpec((e, LANES), lambda i: (0, 0))],
        out_shape=[jax.ShapeDtypeStruct((TOP_K, t), F32),
                   jax.ShapeDtypeStruct((TOP_K, t), F32),
                   jax.ShapeDtypeStruct((TOP_K, t), F32),
                   jax.ShapeDtypeStruct((e, LANES), F32)],
        compiler_params=_cparams(("arbitrary",), 32),
        name="route",
    )(logitsT, rbias)


def _dest_kernel(idx_ref, rank_ref, ps_ref, dest_ref):
    ei = lax.broadcasted_iota(I32, (N_EXPERTS, idx_ref.shape[1]), 0).astype(F32)
    ps = ps_ref[...]
    rows = []
    for k in range(TOP_K):
        hit = ei == idx_ref[k:k + 1, :]
        rows.append(jnp.sum(jnp.where(hit, ps, 0.0), axis=0, keepdims=True) + rank_ref[k:k + 1, :])
    dest_ref[...] = jnp.concatenate(rows, axis=0).astype(I32)


def _dest(idxT, rankT, pad_start):
    t = idxT.shape[1]
    row = pl.BlockSpec((TOP_K, TM), lambda i: (0, i))
    return pl.pallas_call(
        _dest_kernel,
        grid=(t // TM,),
        in_specs=[row, row, pl.BlockSpec((N_EXPERTS, 1), lambda i: (0, 0))],
        out_specs=row,
        out_shape=jax.ShapeDtypeStruct((TOP_K, t), I32),
        compiler_params=_cparams(("parallel",), 24),
        name="dest",
    )(idxT, rankT, pad_start)


def _sc_workers(mesh, n_tokens):
    n_workers = mesh.num_cores * mesh.num_subcores
    per_worker = n_tokens // n_workers
    assert per_worker * n_workers == n_tokens and per_worker % SC_CHUNK == 0
    return per_worker


def _sc_scratch(width):
    return ([pltpu.VMEM((SC_CHUNK,), I32)] * TOP_K
            + [pltpu.VMEM((SC_CHUNK, width), I32), pltpu.SemaphoreType.DMA])


def _sc_dispatch(rows, destT, n_slots):
    t, width = rows.shape
    mesh = plsc.VectorSubcoreMesh(core_axis_name="c", subcore_axis_name="s")
    per_worker = _sc_workers(mesh, t)

    @functools.partial(pl.kernel, mesh=mesh, out_type=jax.ShapeDtypeStruct((n_slots, width), I32),
                       scratch_types=_sc_scratch(width))
    def scatter_rows(rows_hbm, dest_hbm, xs_hbm, *scratch):
        idx, buf, sem = scratch[:TOP_K], scratch[TOP_K], scratch[TOP_K + 1]
        base = (lax.axis_index("s") * mesh.num_cores + lax.axis_index("c")) * per_worker

        @pl.loop(0, per_worker // SC_CHUNK)
        def _(i):
            t0 = pl.multiple_of(base + i * SC_CHUNK, SC_CHUNK)
            for k in range(TOP_K):
                pltpu.sync_copy(dest_hbm.at[k, pl.ds(t0, SC_CHUNK)], idx[k])
            pltpu.sync_copy(rows_hbm.at[pl.ds(t0, SC_CHUNK)], buf)
            copies = [pltpu.async_copy(buf, xs_hbm.at[idx[k]], sem) for k in range(TOP_K)]
            for c in copies:
                c.wait()

    return scatter_rows(rows, destT)


def _sc_combine(y, destT, tok_off, n_tok):
    width = y.shape[1]
    mesh = plsc.VectorSubcoreMesh(core_axis_name="c", subcore_axis_name="s")
    per_worker = _sc_workers(mesh, n_tok)
    half = SC_CHUNK // 2

    @functools.partial(pl.kernel, mesh=mesh, out_type=jax.ShapeDtypeStruct((TOP_K, n_tok, width), I32),
                       scratch_types=([pltpu.VMEM((half,), I32)] * TOP_K + [pltpu.VMEM((half, width), I32)] * 2
                                      + [pltpu.SemaphoreType.DMA] * 4))
    def gather_rows(y_hbm, dest_hbm, out_hbm, *scratch):
        idx, bufs = scratch[:TOP_K], scratch[TOP_K:TOP_K + 2]
        gsem, wsem = scratch[TOP_K + 2:TOP_K + 4], scratch[TOP_K + 4:TOP_K + 6]
        base = (lax.axis_index("s") * mesh.num_cores + lax.axis_index("c")) * per_worker

        @pl.loop(0, per_worker // half)
        def _(i):
            t0 = pl.multiple_of(base + i * half, half)
            for k in range(TOP_K):
                pltpu.sync_copy(dest_hbm.at[k, pl.ds(tok_off + t0, half)], idx[k])
            gathers = [None] * TOP_K
            writes = [None] * TOP_K
            gathers[0] = pltpu.async_copy(y_hbm.at[idx[0]], bufs[0], gsem[0])
            for k in range(TOP_K):
                if k + 1 < TOP_K:
                    if k >= 1:
                        writes[k - 1].wait()
                    gathers[k + 1] = pltpu.async_copy(y_hbm.at[idx[k + 1]], bufs[(k + 1) % 2], gsem[(k + 1) % 2])
                gathers[k].wait()
                writes[k] = pltpu.async_copy(bufs[k % 2], out_hbm.at[k, pl.ds(t0, half)], wsem[k % 2])
            writes[TOP_K - 2].wait()
            writes[TOP_K - 1].wait()

    return gather_rows(y, destT)


def _ffn_kernel(bs_ref, be_ref, bv_ref, nu_ref, wg_ref, wu_ref, wd_ref, xs_hbm, y_hbm,
                wgu_sc, wd_sc, xbuf, ybuf, lsem, ssem):
    e = pl.program_id(0)
    n_used = nu_ref[0]
    ring = FFN_AHEAD + 1

    def load(g):
        return pltpu.make_async_copy(xs_hbm.at[pl.ds(pl.multiple_of(g * BM, BM), BM)], xbuf.at[g % ring],
                                     lsem.at[g % ring])

    def store(g):
        return pltpu.make_async_copy(ybuf.at[g % FFN_OUT], y_hbm.at[pl.ds(pl.multiple_of(g * BM, BM), BM)],
                                     ssem.at[g % FFN_OUT])

    @pl.when(e == 0)
    def _():
        for g0 in range(FFN_AHEAD):
            @pl.when(g0 < n_used)
            def _():
                load(g0).start()

    wgu_sc[:, 0:D_EXPERT] = wg_ref[0].astype(BF16)
    wgu_sc[:, D_EXPERT:2 * D_EXPERT] = wu_ref[0].astype(BF16)
    wd_sc[...] = wd_ref[0].astype(BF16)

    def block(g, c):
        load(g).wait()

        @pl.when(g + FFN_AHEAD < n_used)
        def _():
            load(g + FFN_AHEAD).start()

        row = lax.broadcasted_iota(I32, (BM, 1), 0)
        hi, lo = _unpack_halves(jnp.where(row < bv_ref[g], xbuf[g % ring], 0))
        x = jnp.concatenate([hi, lo], axis=1).astype(BF16)
        gu = jnp.dot(x, wgu_sc[...], preferred_element_type=F32)
        a = (_silu(gu[:, 0:D_EXPERT]) * gu[:, D_EXPERT:2 * D_EXPERT]).astype(BF16)
        y = _pack_row(jnp.dot(a, wd_sc[...], preferred_element_type=F32))

        @pl.when(g >= FFN_OUT)
        def _():
            store(g - FFN_OUT).wait()

        ybuf[g % FFN_OUT] = y
        store(g).start()
        return c

    lax.fori_loop(bs_ref[e], be_ref[e], block, 0)

    @pl.when(e == pl.num_programs(0) - 1)
    def _():
        for back in range(FFN_OUT, 0, -1):
            @pl.when(n_used >= back)
            def _():
                store(n_used - back).wait()


def _ffn(blk_start, blk_end, blk_valid, n_used, xs, w_eg, w_eu, w_ed):
    n_slots = xs.shape[0]
    weights = lambda e, bs, be, bv, nu: (e, 0, 0)
    return pl.pallas_call(
        _ffn_kernel,
        grid_spec=pltpu.PrefetchScalarGridSpec(
            num_scalar_prefetch=4,
            grid=(N_EXPERTS,),
            in_specs=[
                pl.BlockSpec((1, D_MODEL, D_EXPERT), weights),
                pl.BlockSpec((1, D_MODEL, D_EXPERT), weights),
                pl.BlockSpec((1, D_EXPERT, D_MODEL), weights),
                pl.BlockSpec(memory_space=pl.ANY),
            ],
            out_specs=pl.BlockSpec(memory_space=pl.ANY),
            scratch_shapes=[pltpu.VMEM((D_MODEL, 2 * D_EXPERT), BF16),
                            pltpu.VMEM((D_EXPERT, D_MODEL), BF16),
                            pltpu.VMEM((FFN_AHEAD + 1, BM, HALF), I32),
                            pltpu.VMEM((FFN_OUT, BM, HALF), I32),
                            pltpu.SemaphoreType.DMA((FFN_AHEAD + 1,)),
                            pltpu.SemaphoreType.DMA((FFN_OUT,))],
        ),
        out_shape=jax.ShapeDtypeStruct((n_slots, HALF), I32),
        compiler_params=_cparams(("arbitrary",), 32),
        name="ffn",
    )(blk_start, blk_end, blk_valid, n_used, w_eg, w_eu, w_ed, xs)


def _final_kernel(x1_ref, h2_ref, yk_ref, gate_ref, mod_ref, wsgu_ref, wsd_ref, gf_ref, o_ref):
    hi, lo = _unpack_halves(h2_ref[...])
    hb = jnp.concatenate([hi, lo], axis=1).astype(BF16)
    gu = jnp.dot(hb, wsgu_ref[...], preferred_element_type=F32)
    a = (_silu(gu[:, 0:D_SHARED]) * gu[:, D_SHARED:2 * D_SHARED]).astype(BF16)
    shared = jnp.dot(a, wsd_ref[...], preferred_element_type=F32)

    gate = gate_ref[...]
    acc_hi = acc_lo = None
    for k in range(TOP_K):
        hi, lo = _unpack_halves(yk_ref[k])
        g = gate[:, k:k + 1]
        acc_hi = g * hi if acc_hi is None else acc_hi + g * hi
        acc_lo = g * lo if acc_lo is None else acc_lo + g * lo
    routed = jnp.concatenate([acc_hi, acc_lo], axis=1)
    x2 = x1_ref[...] + mod_ref[0, 5:6, :] * (routed + shared)
    o_ref[...] = _rms(x2, -1) * gf_ref[...]


def _final(x1, h2p, yk, gates, mod, wsgu, wsd, gf, tok_off, nb, s, b_off):
    n_tiles = nb * s // TF
    t_off = tok_off // TF
    tpb = s // TF
    const = lambda i: (0, 0)
    return pl.pallas_call(
        _final_kernel,
        grid=(n_tiles,),
        in_specs=[
            pl.BlockSpec((TF, D_MODEL), lambda i: (t_off + i, 0)),
            pl.BlockSpec((TF, HALF), lambda i: (t_off + i, 0)),
            pl.BlockSpec((TOP_K, TF, HALF), lambda i: (0, i, 0)),
            pl.BlockSpec((TF, TOP_K), lambda i: (t_off + i, 0)),
            pl.BlockSpec((1, 6, D_MODEL), lambda i: (b_off + i // tpb, 0, 0)),
            pl.BlockSpec((D_MODEL, 2 * D_SHARED), const),
            pl.BlockSpec((D_SHARED, D_MODEL), const),
            pl.BlockSpec((1, D_MODEL), const),
        ],
        out_specs=pl.BlockSpec((TF, D_MODEL), lambda i: (i, 0)),
        out_shape=jax.ShapeDtypeStruct((nb * s, D_MODEL), F32),
        compiler_params=_cparams(("parallel",), 40),
        name="final",
    )(x1, h2p, yk, gates, mod, wsgu, wsd, gf)


def _t5_bucket(rel):
    nb = N_BUCKETS // 2
    max_exact = nb // 2
    ret = jnp.where(rel > 0, nb, 0)
    n = jnp.abs(rel)
    nf = jnp.maximum(n, 1).astype(F32)
    large = max_exact + (jnp.log(nf / max_exact) / math.log(MAX_DISTANCE / max_exact)
                         * (nb - max_exact)).astype(I32)
    large = jnp.minimum(large, nb - 1)
    return ret + jnp.where(n < max_exact, n, large)


def _bias_tables(rel_bias):
    n_far = np.arange(TK + 1, 1 << 20, dtype=np.float64)
    assert np.all(N_BUCKETS // 4 + np.log(n_far / (N_BUCKETS // 4)) / math.log(MAX_DISTANCE / (N_BUCKETS // 4))
                  * (N_BUCKETS // 4) >= N_BUCKETS // 2), "relative-position buckets must saturate beyond one chunk"
    kk = jnp.arange(TK, dtype=I32)[:, None]
    qq = jnp.arange(TQ, dtype=I32)[None, :]
    rel = jnp.stack([(dd - 2) * TK + kk - qq for dd in range(BAND_TILES)])
    onehot = jax.nn.one_hot(_t5_bucket(rel), N_BUCKETS, dtype=F32)
    band = jnp.einsum("dkqb,bh->hdkq", onehot, rel_bias.astype(F32), precision=lax.Precision.HIGHEST)
    return band * LOG2E


def _rope_tables(s):
    rows = s // GRID_W
    row_id = jnp.repeat(jnp.arange(rows, dtype=F32), GRID_W)
    col_id = jnp.tile(jnp.arange(GRID_W, dtype=F32), rows)
    half = HD_B // 2
    inv = ROPE_THETA ** (-jnp.arange(0, half, 2, dtype=F32) / half)
    ang_r = row_id[:, None] * inv[None, :]
    ang_c = col_id[:, None] * inv[None, :]
    cr, sr, cc, sc = jnp.cos(ang_r), jnp.sin(ang_r), jnp.cos(ang_c), jnp.sin(ang_c)
    c64 = jnp.concatenate([cr, cr, cc, cc], axis=1)
    s64 = jnp.concatenate([-sr, sr, -sc, sc], axis=1)
    return c64.T, s64.T, jnp.concatenate([c64, c64], axis=1), jnp.concatenate([s64, s64], axis=1)


def kernel(x_prompt, x_sample, c_prompt, c_sample, rel_bias, w_ada, b_ada, g_norm1, w_in, lambda_q1, lambda_k1, lambda_q2, lambda_k2, g_subln, g_qnorm, g_knorm, w_out, g_norm2, w_router, router_bias, w_exp_gate, w_exp_up, w_exp_down, w_sh_gate, w_sh_up, w_sh_down, g_final):
    bp, sp, d = x_prompt.shape
    bs, ss, _ = x_sample.shape
    assert d == D_MODEL and sp % TM == 0 and ss % TM == 0 and TM % TK == 0 and TQ % TK == 0
    assert sp % (KSUB * TK) == 0 and ss % (KSUB * TK) == 0

    mod = _ada(jnp.concatenate([c_prompt, c_sample], axis=0), w_ada[LAYER], b_ada[LAYER])
    mod = mod.reshape(bp + bs, 6, d)

    w = w_in[LAYER]
    o1, o2, o3 = A_Q, A_Q + A_K, A_Q + A_K + A_V
    o4, o5 = o3 + B_Q, o3 + B_Q + B_K
    wqT = jnp.concatenate([w[:, :o1], w[:, o3:o4]], axis=1).T.astype(BF16)
    wk = jnp.concatenate([w[:, o1:o2], w[:, o4:o5]], axis=1).astype(BF16)
    wvT = jnp.concatenate([w[:, o2:o3], w[:, o5:]], axis=1).T.astype(BF16)
    g1 = g_norm1[LAYER].reshape(1, d)
    gqT = jnp.broadcast_to(g_qnorm[LAYER][:, None], (HD_B, TM))
    gk = jnp.tile(g_knorm[LAYER], KV_B).reshape(1, LANES)
    band = _bias_tables(rel_bias)
    lam_rows = jnp.stack([lambda_q1[LAYER], lambda_k1[LAYER], lambda_q2[LAYER], lambda_k2[LAYER]])
    gsub = jnp.broadcast_to(g_subln[LAYER][:, None], (2 * HD_A, TQ))
    wo = w_out[LAYER].astype(BF16)
    woa, wob = wo[:A_V], wo[A_V:]

    g2 = g_norm2[LAYER].reshape(1, d)
    wrT = w_router[LAYER].T.astype(BF16)
    rbias = router_bias[LAYER].reshape(N_EXPERTS, 1)
    wsgu = jnp.concatenate([w_sh_gate[LAYER], w_sh_up[LAYER]], axis=1).astype(BF16)
    wsd = w_sh_down[LAYER].astype(BF16)
    gf = g_final.reshape(1, d)

    def layer(x, nb, s, b_off):
        t = nb * s
        x2d = x.reshape(t, d)
        cT, sT, cN, sN = _rope_tables(s)
        qT, k2, vT4 = _inproj(x2d, nb, s, b_off, mod, g1, wqT, wk, wvT, gqT, gk, cT, sT, cN, sN)
        k3 = k2.reshape(nb, s, K_COLS)
        oa = _attention(qT, k3, vT4, True, (band, lam_rows, gsub)).reshape(t, A_V)
        ob = _attention(qT, k3, vT4, False, ()).reshape(t, B_Q)
        x1, h2p, logitsT = _outproj(x2d, oa, ob, s, b_off, mod, woa, wob, g2, wrT)
        idxT, gateT, rankT, cnt = _route(logitsT, rbias)

        counts = cnt[:, 0].astype(I32)
        nblk = (counts + BM - 1) // BM
        blk_end = jnp.cumsum(nblk)
        blk_start = blk_end - nblk
        n_blocks = t * TOP_K // BM + N_EXPERTS
        n_used = blk_end[-1:].astype(I32)
        bidx = jnp.arange(n_blocks, dtype=I32)[:, None]
        owns = jnp.logical_and(bidx >= blk_start[None, :], bidx < blk_end[None, :])
        pad_start = (blk_start * BM).astype(F32).reshape(N_EXPERTS, 1)
        in_blk = jnp.clip(counts[None, :] - (bidx - blk_start[None, :]) * BM, 0, BM)
        blk_valid = jnp.sum(jnp.where(owns, in_blk, 0), axis=1).astype(I32)

        destT = _dest(idxT, rankT, pad_start)
        xs = _sc_dispatch(h2p, destT, n_blocks * BM)
        y = _ffn(blk_start.astype(I32), blk_end.astype(I32), blk_valid, n_used, xs,
                 w_exp_gate[LAYER], w_exp_up[LAYER], w_exp_down[LAYER])
        yk = _sc_combine(y, destT, 0, t)
        out = _final(x1, h2p, yk, gateT.T, mod, wsgu, wsd, gf, 0, nb, s, b_off)
        return out.reshape(nb, s, d)

    return (layer(x_prompt, bp, sp, 0), layer(x_sample, bs, ss, bp))
```

```python
import functools
import math

import jax
import jax.numpy as jnp
import numpy as np
from jax import lax
from jax.experimental import pallas as pl
from jax.experimental.pallas import tpu as pltpu
from jax.experimental.pallas import tpu_sc as plsc

F32 = jnp.float32
BF16 = jnp.bfloat16
I32 = jnp.int32

D_MODEL = 1024
GRID_W = 64
HA = 4
HD_A = 64
HB = 8
KV_B = 2
HD_B = 64
ROPE_THETA = 10000.0
N_BUCKETS = 32
MAX_DISTANCE = 128
A_Q = HA * 2 * HD_A
A_K = HA * 2 * HD_A
A_V = HA * 2 * HD_A
B_Q = HB * HD_B
B_K = KV_B * HD_B
B_V = KV_B * HD_B
N_EXPERTS = 256
TOP_K = 8
N_GROUPS = 8
TOPK_GROUPS = 4
D_EXPERT = 256
D_SHARED = 256
ROUTE_SCALE = 2.5
EPS = 1e-6
LAYER = 0
LAM_INIT = 0.8 - 0.6 * math.exp(-0.3 * LAYER)

LANES = 128
SUBLANES = 8
V7X_VMEM_BYTES = 64 * 1024 * 1024

TM = 512
TQ = 512
TQ_GQA = 1024
TK = 256
BAND_TILES = TQ // TK + 4
KSUB = 4
KSUB_SHORT = 2
MAP_ROWS = 2 * HD_A
ONES_ROWS = 16
N_MAPS = 2 * HA + HB
K_COLS = A_K + B_K
V_ROWS = A_V + B_V
BM = 256
FFN_AHEAD = 6
FFN_OUT = 8
FFN_GROUP = 2
TF = 256
HALF = D_MODEL // 2
SC_CHUNK = 128
LOG2E = math.log2(math.e)
Q_SCALE = (HD_A ** -0.5) * LOG2E
NT_DIMS = (((1,), (1,)), ((), ()))
NEG_BIG = -1e30


def _cparams(semantics, vmem_mb):
    assert vmem_mb * 1024 * 1024 <= V7X_VMEM_BYTES
    return pltpu.CompilerParams(dimension_semantics=semantics, vmem_limit_bytes=vmem_mb * 1024 * 1024)


def _rms(x, axis):
    return x * lax.rsqrt(jnp.mean(x * x, axis=axis, keepdims=True) + EPS)


def _silu(x):
    return x * jax.nn.sigmoid(x)


def _pack_row(x):
    hi = pltpu.bitcast(x[:, :HALF].astype(BF16).astype(F32), I32)
    lo = pltpu.bitcast(x[:, HALF:].astype(BF16).astype(F32), I32)
    return hi | lax.shift_right_logical(lo, 16)


def _unpack_halves(p):
    hi = pltpu.bitcast(p & jnp.int32(-65536), F32)
    lo = pltpu.bitcast(lax.shift_left(p, 16), F32)
    return hi, lo


def _ada_kernel(c_ref, w_ref, b_ref, o_ref):
    a = _silu(c_ref[...])
    o_ref[...] = jnp.dot(a, w_ref[...], preferred_element_type=F32) + b_ref[...]


def _ada(c_all, w, b):
    nb, d = c_all.shape
    n = w.shape[1]
    tn = 768
    return pl.pallas_call(
        _ada_kernel,
        grid=(n // tn,),
        in_specs=[pl.BlockSpec((nb, d), lambda j: (0, 0)),
                  pl.BlockSpec((d, tn), lambda j: (0, j)),
                  pl.BlockSpec((1, tn), lambda j: (0, j))],
        out_specs=pl.BlockSpec((nb, tn), lambda j: (0, j)),
        out_shape=jax.ShapeDtypeStruct((nb, n), F32),
        compiler_params=_cparams(("parallel",), 24),
        name="ada",
    )(c_all, w, b.reshape(1, n))


def _inproj_kernel(x_ref, mod_ref, g1_ref, wqT_ref, wk_ref, wvT_ref, gqT_ref, gk_ref,
                   cT_ref, sT_ref, cN_ref, sN_ref, qT_ref, k_ref, vT_ref):
    x = x_ref[...]
    tm = x.shape[0]
    shift = mod_ref[0, 0:1, :]
    scale = mod_ref[0, 1:2, :]
    h = (_rms(x, -1) * g1_ref[...]) * (1.0 + scale) + shift
    hb = h.astype(BF16)
    qT = lax.dot_general(wqT_ref[...], hb, NT_DIMS, preferred_element_type=F32)
    kn = jnp.dot(hb, wk_ref[...], preferred_element_type=F32)
    vT = lax.dot_general(wvT_ref[...], hb, NT_DIMS, preferred_element_type=F32)

    vTb = vT.astype(BF16)
    for c in range(tm // TK):
        vT_ref[0, c] = vTb[:, c * TK:(c + 1) * TK]

    zeros = jnp.zeros((HD_A, tm), BF16)

    def put_map(m, q, upper):
        r0 = m * MAP_ROWS
        if upper:
            qT_ref[0, r0:r0 + HD_A, :] = zeros
            qT_ref[0, r0 + HD_A:r0 + MAP_ROWS, :] = q
        else:
            qT_ref[0, r0:r0 + HD_A, :] = q
            qT_ref[0, r0 + HD_A:r0 + MAP_ROWS, :] = zeros

    for m in range(2 * HA):
        q = (qT[m * HD_A:(m + 1) * HD_A, :] * Q_SCALE).astype(BF16)
        put_map(m, q, upper=(m % 2 == 1))

    cT = cT_ref[...]
    sT = sT_ref[...]
    gq = gqT_ref[...]
    for j in range(HB):
        xh = qT[A_Q + j * HD_B:A_Q + (j + 1) * HD_B, :]
        y = _rms(xh, 0) * gq
        yp = jnp.concatenate([y[16:32], y[0:16], y[48:64], y[32:48]], axis=0)
        r = (y * cT + yp * sT) * Q_SCALE
        put_map(2 * HA + j, r.astype(BF16), upper=(j // (HB // KV_B) == 1))

    kb = kn[:, A_K:K_COLS]
    lane = lax.broadcasted_iota(I32, kb.shape, 1)
    first = lane < HD_B
    x2 = kb * kb
    s0 = jnp.sum(jnp.where(first, x2, 0.0), axis=-1, keepdims=True)
    s1 = jnp.sum(jnp.where(first, 0.0, x2), axis=-1, keepdims=True)
    ms = jnp.where(first, s0, s1) * (1.0 / HD_B)
    y = kb * lax.rsqrt(ms + EPS) * gk_ref[...]
    partner = jnp.where((lane % 32) < 16, pltpu.roll(y, LANES - 16, 1), pltpu.roll(y, 16, 1))
    r = y * cN_ref[...] + partner * sN_ref[...]
    k_ref[:, 0:A_K] = kn[:, 0:A_K].astype(BF16)
    k_ref[:, A_K:K_COLS] = r.astype(BF16)


def _inproj(x2d, nb, s, b_off, mod, g1, wqT, wk, wvT, gqT, gk, cT, sT, cN, sN):
    nps = s // TM
    const = lambda i: (0, 0)
    return pl.pallas_call(
        _inproj_kernel,
        grid=(nb * nps,),
        in_specs=[
            pl.BlockSpec((TM, D_MODEL), lambda i: (i, 0)),
            pl.BlockSpec((1, 6, D_MODEL), lambda i: (b_off + i // nps, 0, 0)),
            pl.BlockSpec((1, D_MODEL), const),
            pl.BlockSpec((A_Q + B_Q, D_MODEL), const),
            pl.BlockSpec((D_MODEL, K_COLS), const),
            pl.BlockSpec((V_ROWS, D_MODEL), const),
            pl.BlockSpec((HD_B, TM), const),
            pl.BlockSpec((1, LANES), const),
            pl.BlockSpec((HD_B, TM), lambda i: (0, i % nps)),
            pl.BlockSpec((HD_B, TM), lambda i: (0, i % nps)),
            pl.BlockSpec((TM, LANES), lambda i: (i % nps, 0)),
            pl.BlockSpec((TM, LANES), lambda i: (i % nps, 0)),
        ],
        out_specs=[
            pl.BlockSpec((1, N_MAPS * MAP_ROWS, TM), lambda i: (i // nps, 0, i % nps)),
            pl.BlockSpec((TM, K_COLS), lambda i: (i, 0)),
            pl.BlockSpec((1, TM // TK, V_ROWS, TK), lambda i: (i // nps, i % nps, 0, 0)),
        ],
        out_shape=[
            jax.ShapeDtypeStruct((nb, N_MAPS * MAP_ROWS, s), BF16),
            jax.ShapeDtypeStruct((nb * s, K_COLS), BF16),
            jax.ShapeDtypeStruct((nb, s // TK, V_ROWS, TK), BF16),
        ],
        compiler_params=_cparams(("parallel",), 48),
        name="inproj",
    )(x2d, mod, g1, wqT, wk, wvT, gqT, gk, cT, sT, cN, sN)


def _attn_kernel(*refs, is_diff, n_big, ksub):
    if is_diff:
        qT_ref, k_ref, vT_ref, band_ref, lam_ref, gsub_ref, o_ref, s_a, s_b = refs
    else:
        qT_ref, k_ref, vT_ref, o_ref, s_a, s_b = refs
    qi = pl.program_id(2)
    tq = qT_ref.shape[2]
    dv = vT_ref.shape[2]
    q2 = jnp.concatenate([qT_ref[0, 0:MAP_ROWS, :], qT_ref[0, MAP_ROWS:2 * MAP_ROWS, :]], axis=1)

    def score_stage(g, s_buf):
        cm = None
        for j in range(ksub):
            c = g * ksub + j
            kc = k_ref[0, pl.ds(pl.multiple_of(c * TK, TK), TK), :]
            sj = jnp.dot(kc, q2, preferred_element_type=F32)
            if is_diff:
                bias = band_ref[0, jnp.clip(c - (tq // TK) * qi + 2, 0, band_ref.shape[1] - 1)]
                sj = sj + jnp.concatenate([bias, bias], axis=1)
            s_buf[j * TK:(j + 1) * TK, :] = sj
            cj = jnp.max(sj, axis=0, keepdims=True)
            cm = cj if cm is None else jnp.maximum(cm, cj)
        return cm

    ones = jnp.ones((ONES_ROWS, TK), BF16)

    def softmax_stage(g, s_buf, cm, carry):
        m, acc = carry
        m_new = jnp.maximum(m, cm)
        acc = jnp.exp2(m - m_new) * acc
        for j in range(ksub):
            d = s_buf[j * TK:(j + 1) * TK, :] - m_new
            vT = vT_ref[0, g * ksub + j]
            if is_diff:
                p = jnp.exp2(d.astype(BF16))
                acc = acc + jnp.dot(jnp.concatenate([vT, ones], axis=0), p, preferred_element_type=F32)
            else:
                p = jnp.exp2(d)
                pv = jnp.dot(vT, p.astype(BF16), preferred_element_type=F32)
                lsum = jnp.broadcast_to(jnp.sum(p, axis=0, keepdims=True), (ONES_ROWS, p.shape[1]))
                acc = acc + jnp.concatenate([pv, lsum], axis=0)
        return m_new, acc

    def pair(i, state):
        carry, cm_a = state
        g = 2 * i
        cm_b = score_stage(g + 1, s_b)
        carry = softmax_stage(g, s_a, cm_a, carry)
        cm_a = score_stage(g + 2, s_a)
        carry = softmax_stage(g + 1, s_b, cm_b, carry)
        return carry, cm_a

    init = (jnp.full((1, 2 * tq), NEG_BIG, F32), jnp.zeros((dv + ONES_ROWS, 2 * tq), F32))
    carry, cm_a = lax.fori_loop(0, n_big // 2 - 1, pair, (init, score_stage(0, s_a)))
    cm_b = score_stage(n_big - 1, s_b)
    carry = softmax_stage(n_big - 2, s_a, cm_a, carry)
    _, acc = softmax_stage(n_big - 1, s_b, cm_b, carry)
    o2 = acc[0:dv, :] * (1.0 / acc[dv:dv + 1, :])
    outs = [o2[:, 0:tq], o2[:, tq:2 * tq]]

    if is_diff:
        lv = lam_ref[...]
        lam = (jnp.exp(jnp.sum(lv[0:1] * lv[1:2], axis=-1, keepdims=True))
               - jnp.exp(jnp.sum(lv[2:3] * lv[3:4], axis=-1, keepdims=True)) + LAM_INIT)
        o = outs[0] - lam * outs[1]
        o = _rms(o, 0) * gsub_ref[...] * (1.0 - LAM_INIT)
    else:
        o = jnp.concatenate(outs, axis=0)
    o_ref[0] = o.T.astype(BF16)


def _attention(qT, k3, vT4, is_diff, extra):
    nb, s, _ = k3.shape
    nk = s // TK
    tq = TQ if is_diff else TQ_GQA
    nq = s // tq
    if is_diff:
        q_map = lambda b, u, qi: (b, u, qi)
        k_map = lambda b, u, qi: (b, 0, u)
        v_spec = pl.BlockSpec((1, nk, 2 * HD_A, TK), lambda b, u, qi: (b, 0, u, 0))
        extra_specs = [
            pl.BlockSpec((1, BAND_TILES, TK, TQ), lambda b, u, qi: (u, 0, 0, 0)),
            pl.BlockSpec((4, HD_A), lambda b, u, qi: (0, 0)),
            pl.BlockSpec((2 * HD_A, TQ), lambda b, u, qi: (0, 0)),
        ]
        n_units = HA
    else:
        first = (2 * HA * MAP_ROWS) // (2 * MAP_ROWS)
        q_map = lambda b, u, qi: (b, first + u, qi)
        k_map = lambda b, u, qi: (b, 0, A_K // LANES)
        v0 = A_V // HD_B
        per_kv = (HB // KV_B) // 2
        v_spec = pl.BlockSpec((1, nk, HD_B, TK), lambda b, u, qi: (b, 0, v0 + u // per_kv, 0))
        extra_specs = []
        n_units = HB // 2
    ksub = KSUB if nk >= 4 * KSUB else KSUB_SHORT
    n_big = nk // ksub
    assert n_big * ksub == nk and n_big % 2 == 0
    return pl.pallas_call(
        functools.partial(_attn_kernel, is_diff=is_diff, n_big=n_big, ksub=ksub),
        grid=(nb, n_units, nq),
        in_specs=[pl.BlockSpec((1, 2 * MAP_ROWS, tq), q_map),
                  pl.BlockSpec((1, s, LANES), k_map),
                  v_spec] + extra_specs,
        out_specs=pl.BlockSpec((1, tq, LANES), lambda b, u, qi: (b, qi, u)),
        out_shape=jax.ShapeDtypeStruct((nb, s, n_units * LANES), BF16),
        scratch_shapes=[pltpu.VMEM((ksub * TK, 2 * tq), F32)] * 2,
        compiler_params=_cparams(("parallel", "parallel", "parallel"), 40),
        name="attn_diff" if is_diff else "attn_gqa",
    )(qT, k3, vT4, *extra)


def _outproj_kernel(x_ref, aa_ref, ab_ref, mod_ref, woa_ref, wob_ref, g2_ref, wrT_ref, rb_ref,
                    x1_ref, h2_ref, idx_ref, gate_ref, rank_ref, cnt_ref):
    att = (jnp.dot(aa_ref[...], woa_ref[...], preferred_element_type=F32)
           + jnp.dot(ab_ref[...], wob_ref[...], preferred_element_type=F32))
    x1 = x_ref[...] + mod_ref[0, 2:3, :] * att
    h2 = (_rms(x1, -1) * g2_ref[...]) * (1.0 + mod_ref[0, 4:5, :]) + mod_ref[0, 3:4, :]
    x1_ref[...] = x1
    h2_ref[...] = _pack_row(h2)
    logits = lax.dot_general(wrT_ref[...], h2.astype(BF16), NT_DIMS, preferred_element_type=F32)
    _route_tile(logits, rb_ref, idx_ref, gate_ref, rank_ref, cnt_ref)


def _outproj(x2d, aa, ab, s, b_off, mod, woa, wob, g2, wrT, rbias):
    t = x2d.shape[0]
    tpb = s // TM
    const = lambda i: (0, 0)
    row = lambda i: (i, 0)
    choice_rows = pl.BlockSpec((TOP_K, TM), lambda i: (0, i))
    return pl.pallas_call(
        _outproj_kernel,
        grid=(t // TM,),
        in_specs=[
            pl.BlockSpec((TM, D_MODEL), row),
            pl.BlockSpec((TM, A_V), row),
            pl.BlockSpec((TM, B_Q), row),
            pl.BlockSpec((1, 6, D_MODEL), lambda i: (b_off + i // tpb, 0, 0)),
            pl.BlockSpec((A_V, D_MODEL), const),
            pl.BlockSpec((B_Q, D_MODEL), const),
            pl.BlockSpec((1, D_MODEL), const),
            pl.BlockSpec((N_EXPERTS, D_MODEL), const),
            pl.BlockSpec((N_EXPERTS, 1), const),
        ],
        out_specs=[
            pl.BlockSpec((TM, D_MODEL), lambda i: (i, 0)),
            pl.BlockSpec((TM, HALF), lambda i: (i, 0)),
            choice_rows, choice_rows, choice_rows,
            pl.BlockSpec((N_EXPERTS, LANES), const),
        ],
        out_shape=[
            jax.ShapeDtypeStruct((t, D_MODEL), F32),
            jax.ShapeDtypeStruct((t, HALF), I32),
            jax.ShapeDtypeStruct((TOP_K, t), F32),
            jax.ShapeDtypeStruct((TOP_K, t), F32),
            jax.ShapeDtypeStruct((TOP_K, t), F32),
            jax.ShapeDtypeStruct((N_EXPERTS, LANES), F32),
        ],
        compiler_params=_cparams(("arbitrary",), 48),
        name="outproj",
    )(x2d, aa, ab, mod, woa, wob, g2, wrT, rbias)


def _route_tile(logits, rb_ref, idx_ref, gate_ref, rank_ref, cnt_ref):
    i = pl.program_id(0)

    @pl.when(i == 0)
    def _():
        cnt_ref[...] = jnp.zeros_like(cnt_ref)

    scores = jax.nn.sigmoid(logits)
    tr = scores.shape[1]
    sel = scores + rb_ref[...]
    gsz = N_EXPERTS // N_GROUPS

    rows = []
    for g in range(N_GROUPS):
        blk = sel[g * gsz:(g + 1) * gsz, :]
        m1 = jnp.max(blk, axis=0, keepdims=True)
        eq = blk == m1
        n1 = jnp.sum(jnp.where(eq, 1.0, 0.0), axis=0, keepdims=True)
        m2 = jnp.max(jnp.where(eq, -jnp.inf, blk), axis=0, keepdims=True)
        rows.append(m1 + jnp.where(n1 >= 2.0, m1, m2))
    gs = jnp.concatenate(rows, axis=0)
    gi = lax.broadcasted_iota(I32, gs.shape, 0)
    beaten = jnp.zeros(gs.shape, F32)
    for gp in range(N_GROUPS):
        row = gs[gp:gp + 1, :]
        ahead = jnp.where(row > gs, 1.0, jnp.where(jnp.logical_and(row == gs, gp < gi), 1.0, 0.0))
        beaten = beaten + ahead
    keep = jnp.where(beaten < float(TOPK_GROUPS), 1.0, 0.0)
    emask = jnp.concatenate(
        [jnp.broadcast_to(keep[g:g + 1, :], (gsz, tr)) for g in range(N_GROUPS)], axis=0) > 0.5
    masked = jnp.where(emask, sel, -jnp.inf)

    ei = lax.broadcasted_iota(I32, masked.shape, 0).astype(F32)
    idxs, ws = [], []
    for _ in range(TOP_K):
        mk = jnp.max(masked, axis=0, keepdims=True)
        ik = jnp.min(jnp.where(masked == mk, ei, float(N_EXPERTS)), axis=0, keepdims=True)
        hit = ei == ik
        ws.append(jnp.sum(jnp.where(hit, scores, 0.0), axis=0, keepdims=True))
        masked = jnp.where(hit, -jnp.inf, masked)
        idxs.append(ik)
    wsum = ws[0]
    for w in ws[1:]:
        wsum = wsum + w
    gate_ref[...] = jnp.concatenate([w / wsum * ROUTE_SCALE for w in ws], axis=0)
    idx_ref[...] = jnp.concatenate(idxs, axis=0)

    chosen = jnp.logical_and(emask, masked == -jnp.inf)
    onehot = jnp.where(chosen, 1.0, 0.0)
    before = (lax.broadcasted_iota(I32, (tr, tr), 0) < lax.broadcasted_iota(I32, (tr, tr), 1))
    upper = jnp.where(before, 1.0, 0.0).astype(BF16)
    prior = jnp.dot(onehot.astype(BF16), upper, preferred_element_type=F32) + cnt_ref[:, 0:1]
    rank_ref[...] = jnp.concatenate(
        [jnp.sum(jnp.where(ei == ik, prior, 0.0), axis=0, keepdims=True) for ik in idxs], axis=0)
    cnt_ref[...] = cnt_ref[...] + jnp.sum(onehot, axis=1, keepdims=True)


def _dest_kernel(idx_ref, rank_ref, ps_ref, dest_ref):
    ei = lax.broadcasted_iota(I32, (N_EXPERTS, idx_ref.shape[1]), 0).astype(F32)
    ps = ps_ref[...]
    rows = []
    for k in range(TOP_K):
        hit = ei == idx_ref[k:k + 1, :]
        rows.append(jnp.sum(jnp.where(hit, ps, 0.0), axis=0, keepdims=True) + rank_ref[k:k + 1, :])
    dest_ref[...] = jnp.concatenate(rows, axis=0).astype(I32)


def _dest(idxT, rankT, pad_start):
    t = idxT.shape[1]
    row = pl.BlockSpec((TOP_K, TM), lambda i: (0, i))
    return pl.pallas_call(
        _dest_kernel,
        grid=(t // TM,),
        in_specs=[row, row, pl.BlockSpec((N_EXPERTS, 1), lambda i: (0, 0))],
        out_specs=row,
        out_shape=jax.ShapeDtypeStruct((TOP_K, t), I32),
        compiler_params=_cparams(("parallel",), 24),
        name="dest",
    )(idxT, rankT, pad_start)


def _sc_workers(mesh, n_tokens):
    n_workers = mesh.num_cores * mesh.num_subcores
    per_worker = n_tokens // n_workers
    assert per_worker * n_workers == n_tokens and per_worker % SC_CHUNK == 0
    return per_worker


def _sc_scratch(width):
    return ([pltpu.VMEM((SC_CHUNK,), I32)] * TOP_K
            + [pltpu.VMEM((SC_CHUNK, width), I32), pltpu.SemaphoreType.DMA])


def _sc_dispatch(rows, destT, n_slots):
    t, width = rows.shape
    mesh = plsc.VectorSubcoreMesh(core_axis_name="c", subcore_axis_name="s")
    per_worker = _sc_workers(mesh, t)

    @functools.partial(pl.kernel, mesh=mesh, out_type=jax.ShapeDtypeStruct((n_slots, width), I32),
                       scratch_types=_sc_scratch(width))
    def scatter_rows(rows_hbm, dest_hbm, xs_hbm, *scratch):
        idx, buf, sem = scratch[:TOP_K], scratch[TOP_K], scratch[TOP_K + 1]
        base = (lax.axis_index("s") * mesh.num_cores + lax.axis_index("c")) * per_worker

        @pl.loop(0, per_worker // SC_CHUNK)
        def _(i):
            t0 = pl.multiple_of(base + i * SC_CHUNK, SC_CHUNK)
            for k in range(TOP_K):
                pltpu.sync_copy(dest_hbm.at[k, pl.ds(t0, SC_CHUNK)], idx[k])
            pltpu.sync_copy(rows_hbm.at[pl.ds(t0, SC_CHUNK)], buf)
            copies = [pltpu.async_copy(buf, xs_hbm.at[idx[k]], sem) for k in range(TOP_K)]
            for c in copies:
                c.wait()

    return scatter_rows(rows, destT)


def _sc_combine(y, destT):
    width = y.shape[1]
    n_tok = destT.shape[1]
    mesh = plsc.VectorSubcoreMesh(core_axis_name="c", subcore_axis_name="s")
    per_worker = _sc_workers(mesh, n_tok)
    half = SC_CHUNK // 2

    @functools.partial(pl.kernel, mesh=mesh, out_type=jax.ShapeDtypeStruct((TOP_K, n_tok, width), I32),
                       scratch_types=([pltpu.VMEM((half,), I32)] * TOP_K + [pltpu.VMEM((half, width), I32)] * 2
                                      + [pltpu.SemaphoreType.DMA] * 4))
    def gather_rows(y_hbm, dest_hbm, out_hbm, *scratch):
        idx, bufs = scratch[:TOP_K], scratch[TOP_K:TOP_K + 2]
        gsem, wsem = scratch[TOP_K + 2:TOP_K + 4], scratch[TOP_K + 4:TOP_K + 6]
        base = (lax.axis_index("s") * mesh.num_cores + lax.axis_index("c")) * per_worker

        @pl.loop(0, per_worker // half)
        def _(i):
            t0 = pl.multiple_of(base + i * half, half)
            for k in range(TOP_K):
                pltpu.sync_copy(dest_hbm.at[k, pl.ds(t0, half)], idx[k])
            gathers = [None] * TOP_K
            writes = [None] * TOP_K
            gathers[0] = pltpu.async_copy(y_hbm.at[idx[0]], bufs[0], gsem[0])
            for k in range(TOP_K):
                if k + 1 < TOP_K:
                    if k >= 1:
                        writes[k - 1].wait()
                    gathers[k + 1] = pltpu.async_copy(y_hbm.at[idx[k + 1]], bufs[(k + 1) % 2], gsem[(k + 1) % 2])
                gathers[k].wait()
                writes[k] = pltpu.async_copy(bufs[k % 2], out_hbm.at[k, pl.ds(t0, half)], wsem[k % 2])
            writes[TOP_K - 2].wait()
            writes[TOP_K - 1].wait()

    return gather_rows(y, destT)


def _ffn_kernel(bs_ref, be_ref, bv_ref, nu_ref, wg_ref, wu_ref, wd_ref, xs_hbm, y_hbm,
                wgu_sc, wd_sc, xbuf, ybuf, lsem, ssem):
    e = pl.program_id(0)
    n_used = nu_ref[0]
    ring = FFN_AHEAD + FFN_GROUP

    def load(g):
        return pltpu.make_async_copy(xs_hbm.at[pl.ds(pl.multiple_of(g * BM, BM), BM)], xbuf.at[g % ring],
                                     lsem.at[g % ring])

    def store(g):
        return pltpu.make_async_copy(ybuf.at[g % FFN_OUT], y_hbm.at[pl.ds(pl.multiple_of(g * BM, BM), BM)],
                                     ssem.at[g % FFN_OUT])

    @pl.when(e == 0)
    def _():
        for g0 in range(FFN_AHEAD):
            @pl.when(g0 < n_used)
            def _():
                load(g0).start()

    wgu_sc[:, 0:D_EXPERT] = wg_ref[0].astype(BF16)
    wgu_sc[:, D_EXPERT:2 * D_EXPERT] = wu_ref[0].astype(BF16)
    wd_sc[...] = wd_ref[0].astype(BF16)

    def run_blocks(g, n):
        row = lax.broadcasted_iota(I32, (BM, 1), 0)
        packed = []
        for i in range(n):
            load(g + i).wait()
            packed.append(jnp.where(row < bv_ref[g + i], xbuf[(g + i) % ring], 0))
        for i in range(n):
            @pl.when(g + i + FFN_AHEAD < n_used)
            def _():
                load(g + i + FFN_AHEAD).start()

        hi, lo = _unpack_halves(jnp.concatenate(packed, axis=0))
        x = jnp.concatenate([hi, lo], axis=1).astype(BF16)
        gu = jnp.dot(x, wgu_sc[...], preferred_element_type=F32)
        a = (_silu(gu[:, 0:D_EXPERT]) * gu[:, D_EXPERT:2 * D_EXPERT]).astype(BF16)
        y = _pack_row(jnp.dot(a, wd_sc[...], preferred_element_type=F32))
        for i in range(n):
            @pl.when(g + i >= FFN_OUT)
            def _():
                store(g + i - FFN_OUT).wait()

            ybuf[(g + i) % FFN_OUT] = y[i * BM:(i + 1) * BM]
            store(g + i).start()

    first, n_mine = bs_ref[e], be_ref[e] - bs_ref[e]

    def group(i, c):
        run_blocks(first + FFN_GROUP * i, FFN_GROUP)
        return c

    lax.fori_loop(0, n_mine // FFN_GROUP, group, 0)
    done = first + (n_mine // FFN_GROUP) * FFN_GROUP
    size = FFN_GROUP // 2
    while size >= 1:
        take = (n_mine // size) % 2 == 1

        @pl.when(take)
        def _(done=done, size=size):
            run_blocks(done, size)

        done = done + jnp.where(take, size, 0)
        size //= 2

    @pl.when(e == pl.num_programs(0) - 1)
    def _():
        for back in range(FFN_OUT, 0, -1):
            @pl.when(n_used >= back)
            def _():
                store(n_used - back).wait()


def _ffn(blk_start, blk_end, blk_valid, n_used, xs, w_eg, w_eu, w_ed):
    n_slots = xs.shape[0]
    weights = lambda e, bs, be, bv, nu: (e, 0, 0)
    return pl.pallas_call(
        _ffn_kernel,
        grid_spec=pltpu.PrefetchScalarGridSpec(
            num_scalar_prefetch=4,
            grid=(N_EXPERTS,),
            in_specs=[
                pl.BlockSpec((1, D_MODEL, D_EXPERT), weights),
                pl.BlockSpec((1, D_MODEL, D_EXPERT), weights),
                pl.BlockSpec((1, D_EXPERT, D_MODEL), weights),
                pl.BlockSpec(memory_space=pl.ANY),
            ],
            out_specs=pl.BlockSpec(memory_space=pl.ANY),
            scratch_shapes=[pltpu.VMEM((D_MODEL, 2 * D_EXPERT), BF16),
                            pltpu.VMEM((D_EXPERT, D_MODEL), BF16),
                            pltpu.VMEM((FFN_AHEAD + FFN_GROUP, BM, HALF), I32),
                            pltpu.VMEM((FFN_OUT, BM, HALF), I32),
                            pltpu.SemaphoreType.DMA((FFN_AHEAD + FFN_GROUP,)),
                            pltpu.SemaphoreType.DMA((FFN_OUT,))],
        ),
        out_shape=jax.ShapeDtypeStruct((n_slots, HALF), I32),
        compiler_params=_cparams(("arbitrary",), 48),
        name="ffn",
    )(blk_start, blk_end, blk_valid, n_used, w_eg, w_eu, w_ed, xs)


def _final_kernel(x1_ref, h2_ref, yk_ref, gate_ref, mod_ref, wsgu_ref, wsd_ref, gf_ref, o_ref):
    hi, lo = _unpack_halves(h2_ref[...])
    hb = jnp.concatenate([hi, lo], axis=1).astype(BF16)
    gu = jnp.dot(hb, wsgu_ref[...], preferred_element_type=F32)
    a = (_silu(gu[:, 0:D_SHARED]) * gu[:, D_SHARED:2 * D_SHARED]).astype(BF16)
    shared = jnp.dot(a, wsd_ref[...], preferred_element_type=F32)

    gate = gate_ref[...]
    acc_hi = acc_lo = None
    for k in range(TOP_K):
        hi, lo = _unpack_halves(yk_ref[k])
        g = gate[:, k:k + 1]
        acc_hi = g * hi if acc_hi is None else acc_hi + g * hi
        acc_lo = g * lo if acc_lo is None else acc_lo + g * lo
    routed = jnp.concatenate([acc_hi, acc_lo], axis=1)
    x2 = x1_ref[...] + mod_ref[0, 5:6, :] * (routed + shared)
    o_ref[...] = _rms(x2, -1) * gf_ref[...]


def _final(x1, h2p, yk, gates, mod, wsgu, wsd, gf, nb, s, b_off):
    n_tiles = nb * s // TF
    tpb = s // TF
    const = lambda i: (0, 0)
    return pl.pallas_call(
        _final_kernel,
        grid=(n_tiles,),
        in_specs=[
            pl.BlockSpec((TF, D_MODEL), lambda i: (i, 0)),
            pl.BlockSpec((TF, HALF), lambda i: (i, 0)),
            pl.BlockSpec((TOP_K, TF, HALF), lambda i: (0, i, 0)),
            pl.BlockSpec((TF, TOP_K), lambda i: (i, 0)),
            pl.BlockSpec((1, 6, D_MODEL), lambda i: (b_off + i // tpb, 0, 0)),
            pl.BlockSpec((D_MODEL, 2 * D_SHARED), const),
            pl.BlockSpec((D_SHARED, D_MODEL), const),
            pl.BlockSpec((1, D_MODEL), const),
        ],
        out_specs=pl.BlockSpec((TF, D_MODEL), lambda i: (i, 0)),
        out_shape=jax.ShapeDtypeStruct((nb * s, D_MODEL), F32),
        compiler_params=_cparams(("parallel",), 40),
        name="final",
    )(x1, h2p, yk, gates, mod, wsgu, wsd, gf)


def _t5_bucket(rel):
    nb = N_BUCKETS // 2
    max_exact = nb // 2
    ret = jnp.where(rel > 0, nb, 0)
    n = jnp.abs(rel)
    nf = jnp.maximum(n, 1).astype(F32)
    large = max_exact + (jnp.log(nf / max_exact) / math.log(MAX_DISTANCE / max_exact)
                         * (nb - max_exact)).astype(I32)
    large = jnp.minimum(large, nb - 1)
    return ret + jnp.where(n < max_exact, n, large)


def _bias_tables(rel_bias):
    n_far = np.arange(TK + 1, 1 << 20, dtype=np.float64)
    assert np.all(N_BUCKETS // 4 + np.log(n_far / (N_BUCKETS // 4)) / math.log(MAX_DISTANCE / (N_BUCKETS // 4))
                  * (N_BUCKETS // 4) >= N_BUCKETS // 2), "relative-position buckets must saturate beyond one chunk"
    kk = jnp.arange(TK, dtype=I32)[:, None]
    qq = jnp.arange(TQ, dtype=I32)[None, :]
    rel = jnp.stack([(dd - 2) * TK + kk - qq for dd in range(BAND_TILES)])
    onehot = jax.nn.one_hot(_t5_bucket(rel), N_BUCKETS, dtype=F32)
    band = jnp.einsum("dkqb,bh->hdkq", onehot, rel_bias.astype(F32), precision=lax.Precision.HIGHEST)
    return band * LOG2E


def _rope_tables(s):
    rows = s // GRID_W
    row_id = jnp.repeat(jnp.arange(rows, dtype=F32), GRID_W)
    col_id = jnp.tile(jnp.arange(GRID_W, dtype=F32), rows)
    half = HD_B // 2
    inv = ROPE_THETA ** (-jnp.arange(0, half, 2, dtype=F32) / half)
    ang_r = row_id[:, None] * inv[None, :]
    ang_c = col_id[:, None] * inv[None, :]
    cr, sr, cc, sc = jnp.cos(ang_r), jnp.sin(ang_r), jnp.cos(ang_c), jnp.sin(ang_c)
    c64 = jnp.concatenate([cr, cr, cc, cc], axis=1)
    s64 = jnp.concatenate([-sr, sr, -sc, sc], axis=1)
    return c64.T, s64.T, jnp.concatenate([c64, c64], axis=1), jnp.concatenate([s64, s64], axis=1)


def kernel(x_prompt, x_sample, c_prompt, c_sample, rel_bias, w_ada, b_ada, g_norm1, w_in, lambda_q1, lambda_k1, lambda_q2, lambda_k2, g_subln, g_qnorm, g_knorm, w_out, g_norm2, w_router, router_bias, w_exp_gate, w_exp_up, w_exp_down, w_sh_gate, w_sh_up, w_sh_down, g_final):
    bp, sp, d = x_prompt.shape
    bs, ss, _ = x_sample.shape
    assert d == D_MODEL and sp % TM == 0 and ss % TM == 0 and TM % TK == 0 and TQ % TK == 0
    assert sp % (KSUB * TK) == 0 and ss % (KSUB * TK) == 0

    mod = _ada(jnp.concatenate([c_prompt, c_sample], axis=0), w_ada[LAYER], b_ada[LAYER])
    mod = mod.reshape(bp + bs, 6, d)

    w = w_in[LAYER]
    o1, o2, o3 = A_Q, A_Q + A_K, A_Q + A_K + A_V
    o4, o5 = o3 + B_Q, o3 + B_Q + B_K
    wqT = jnp.concatenate([w[:, :o1], w[:, o3:o4]], axis=1).T.astype(BF16)
    wk = jnp.concatenate([w[:, o1:o2], w[:, o4:o5]], axis=1).astype(BF16)
    wvT = jnp.concatenate([w[:, o2:o3], w[:, o5:]], axis=1).T.astype(BF16)
    g1 = g_norm1[LAYER].reshape(1, d)
    gqT = jnp.broadcast_to(g_qnorm[LAYER][:, None], (HD_B, TM))
    gk = jnp.tile(g_knorm[LAYER], KV_B).reshape(1, LANES)
    band = _bias_tables(rel_bias)
    lam_rows = jnp.stack([lambda_q1[LAYER], lambda_k1[LAYER], lambda_q2[LAYER], lambda_k2[LAYER]])
    gsub = jnp.broadcast_to(g_subln[LAYER][:, None], (2 * HD_A, TQ))
    wo = w_out[LAYER].astype(BF16)
    woa, wob = wo[:A_V], wo[A_V:]

    g2 = g_norm2[LAYER].reshape(1, d)
    wrT = w_router[LAYER].T.astype(BF16)
    rbias = router_bias[LAYER].reshape(N_EXPERTS, 1)
    wsgu = jnp.concatenate([w_sh_gate[LAYER], w_sh_up[LAYER]], axis=1).astype(BF16)
    wsd = w_sh_down[LAYER].astype(BF16)
    gf = g_final.reshape(1, d)

    def layer(x, nb, s, b_off):
        t = nb * s
        x2d = x.reshape(t, d)
        cT, sT, cN, sN = _rope_tables(s)
        qT, k2, vT4 = _inproj(x2d, nb, s, b_off, mod, g1, wqT, wk, wvT, gqT, gk, cT, sT, cN, sN)
        k3 = k2.reshape(nb, s, K_COLS)
        oa = _attention(qT, k3, vT4, True, (band, lam_rows, gsub)).reshape(t, A_V)
        ob = _attention(qT, k3, vT4, False, ()).reshape(t, B_Q)
        x1, h2p, idxT, gateT, rankT, cnt = _outproj(x2d, oa, ob, s, b_off, mod, woa, wob, g2, wrT, rbias)

        counts = cnt[:, 0].astype(I32)
        nblk = (counts + BM - 1) // BM
        blk_end = jnp.cumsum(nblk)
        blk_start = blk_end - nblk
        n_blocks = t * TOP_K // BM + N_EXPERTS
        n_used = blk_end[-1:].astype(I32)
        bidx = jnp.arange(n_blocks, dtype=I32)[:, None]
        owns = jnp.logical_and(bidx >= blk_start[None, :], bidx < blk_end[None, :])
        pad_start = (blk_start * BM).astype(F32).reshape(N_EXPERTS, 1)
        in_blk = jnp.clip(counts[None, :] - (bidx - blk_start[None, :]) * BM, 0, BM)
        blk_valid = jnp.sum(jnp.where(owns, in_blk, 0), axis=1).astype(I32)

        destT = _dest(idxT, rankT, pad_start)
        xs = _sc_dispatch(h2p, destT, n_blocks * BM)
        y = _ffn(blk_start.astype(I32), blk_end.astype(I32), blk_valid, n_used, xs,
                 w_exp_gate[LAYER], w_exp_up[LAYER], w_exp_down[LAYER])
        yk = _sc_combine(y, destT)
        out = _final(x1, h2p, yk, gateT.T, mod, wsgu, wsd, gf, nb, s, b_off)
        return out.reshape(nb, s, d)

    return (layer(x_prompt, bp, sp, 0), layer(x_sample, bs, ss, bp))
```
